```python
import math
import jax, jax.numpy as jnp
from jax import lax
import numpy as np

D_MODEL = 1024
BATCH = 8
SEQ = 2048
DEPTH = 2

PLE_DIM = 256
MIX_WIDTH = D_MODEL
GROUP_WIDTH = MIX_WIDTH // 4
CHUNK = 64
EPS = 1e-6

GLA_HEADS = 4
GLA_DK = GROUP_WIDTH // (2 * GLA_HEADS)
GLA_DV = GROUP_WIDTH // GLA_HEADS
GLA_RANK = 16
GLA_GATE_NORM = 16.0

S5_CH = 16
S5_GROUPS = GROUP_WIDTH // S5_CH
S5_STATE = 64
S5_DT_MIN = 1e-3
S5_DT_MAX = 1e-1

RET_HEADS = 4
RET_DK = GROUP_WIDTH // RET_HEADS
RET_DV = GROUP_WIDTH // RET_HEADS
ROPE_BASE = 10000.0

GDN_HEADS = 4
GDN_DK = GROUP_WIDTH // GDN_HEADS
GDN_DV = GROUP_WIDTH // GDN_HEADS
GDN_CONV = 4
GDN_DT_MIN = 1e-3
GDN_DT_MAX = 1e-1

FFN_HIDDEN = -(-8 * D_MODEL // (3 * 256)) * 256

IN_SPLITS = (
    GLA_HEADS * GLA_DK, GLA_HEADS * GLA_DK, GLA_HEADS * GLA_DV, GLA_RANK, GROUP_WIDTH,
    GROUP_WIDTH,
    RET_HEADS * RET_DK, RET_HEADS * RET_DK, RET_HEADS * RET_DV, GROUP_WIDTH,
    GDN_HEADS * GDN_DK, GDN_HEADS * GDN_DK, GDN_HEADS * GDN_DV, GDN_HEADS, GDN_HEADS, GROUP_WIDTH,
)
IN_WIDTH = sum(IN_SPLITS)

kernel_name = "hybrid_parallel_gla_s5_retnet_gdn"

F32 = jnp.float32


def rms_norm(x, g):
    xf = x.astype(F32)
    y = xf * lax.rsqrt(jnp.mean(xf * xf, axis=-1, keepdims=True) + EPS)
    return (y * g.astype(F32)).astype(x.dtype)


def head_rms_norm(x, g):
    return x * lax.rsqrt(jnp.mean(x * x, axis=-1, keepdims=True) + EPS) * g.astype(F32)


def l2_norm(x):
    return x * lax.rsqrt(jnp.sum(x * x, axis=-1, keepdims=True) + EPS)


def to_chunks(x):
    B, L, H, d = x.shape
    return x.reshape(B, L // CHUNK, CHUNK, H, d).transpose(0, 3, 1, 2, 4)


def from_chunks(x):
    B, H, nc, C, d = x.shape
    return x.transpose(0, 2, 3, 1, 4).reshape(B, nc * C, H, d)


def split_columns(z):
    offs = np.cumsum(IN_SPLITS)[:-1].tolist()
    return jnp.split(z, offs, axis=-1)


def scan_chunk_states(d_state, decay):
    ds = jnp.moveaxis(d_state, 2, 0)
    dc = jnp.moveaxis(decay, 2, 0)

    def step(S, inp):
        inc, a = inp
        return S * a + inc, S

    _, s_prev = lax.scan(step, jnp.zeros_like(ds[0]), (ds, dc))
    return jnp.moveaxis(s_prev, 0, 2)


def rope_tables(positions):
    inv_freq = ROPE_BASE ** (-jnp.linspace(0.0, 1.0, RET_DK // 2, dtype=F32))
    ang = positions.astype(F32)[..., None] * inv_freq
    return jnp.cos(ang)[:, :, None, :], jnp.sin(ang)[:, :, None, :]


def apply_rope(x, cos, sin):
    x1, x2 = jnp.split(x, 2, axis=-1)
    return jnp.concatenate([x1 * cos - x2 * sin, x1 * sin + x2 * cos], axis=-1)


def causal_depthwise_conv(x, w):
    K, C = w.shape
    return lax.conv_general_dilated(x, w[:, None, :], window_strides=(1,), padding=[(K - 1, 0)],
                                    dimension_numbers=("NWC", "WIO", "NWC"), feature_group_count=C)


def gla_mixer(q, k, v, a_low, r, w_a2, b_a, norm_g):
    B, L = q.shape[:2]
    q = q.astype(F32).reshape(B, L, GLA_HEADS, GLA_DK) * GLA_DK ** -0.5
    k = k.astype(F32).reshape(B, L, GLA_HEADS, GLA_DK)
    v = v.astype(F32).reshape(B, L, GLA_HEADS, GLA_DV)
    gk = jax.nn.log_sigmoid(a_low.astype(F32) @ w_a2.astype(F32) + b_a.astype(F32)) / GLA_GATE_NORM
    gk = gk.reshape(B, L, GLA_HEADS, GLA_DK)
    qc, kc, vc, gc = to_chunks(q), to_chunks(k), to_chunks(v), to_chunks(gk)
    b = jnp.cumsum(gc, axis=3)
    b_last = b[:, :, :, -1:, :]
    q_t = qc * jnp.exp(b)
    k_t = kc * jnp.exp(-b)
    k_end = kc * jnp.exp(b_last - b)
    causal = jnp.tril(jnp.ones((CHUNK, CHUNK), dtype=bool))
    attn = jnp.where(causal, jnp.einsum('bhncd,bhnsd->bhncs', q_t, k_t), 0.0)
    o = jnp.einsum('bhncs,bhnsv->bhncv', attn, vc)
    d_state = jnp.einsum('bhnsd,bhnsv->bhndv', k_end, vc)
    s_prev = scan_chunk_states(d_state, jnp.swapaxes(jnp.exp(b_last), -1, -2))
    o = o + jnp.einsum('bhncd,bhndv->bhncv', q_t, s_prev)
    o = head_rms_norm(from_chunks(o), norm_g)
    return o.reshape(B, L, GLA_HEADS * GLA_DV) * jax.nn.silu(r.astype(F32))


def s5_mixer(u, lam_re, lam_im, log_dt, b_re, b_im, c_re, c_im, d_skip, w_glu, b_glu):
    B, L, _ = u.shape
    uf = u.astype(F32).reshape(B, L, S5_GROUPS, S5_CH)
    lam = lax.complex(jnp.minimum(lam_re.astype(F32), -1e-4), lam_im.astype(F32))
    dt = jnp.exp(log_dt.astype(F32))[:, None]
    a_bar = jnp.exp(lam * dt)
    b_bar = ((a_bar - 1.0) / lam)[..., None] * lax.complex(b_re.astype(F32), b_im.astype(F32))
    bu = jnp.einsum('blgh,gph->blgp', uf, b_bar)
    a_elems = jnp.broadcast_to(a_bar, (L,) + a_bar.shape)[None]

    def combine(e1, e2):
        a1, s1 = e1
        a2, s2 = e2
        return a2 * a1, a2 * s1 + s2

    _, states = lax.associative_scan(combine, (a_elems, bu), axis=1)
    c = lax.complex(c_re.astype(F32), c_im.astype(F32))
    y = jnp.real(jnp.einsum('blgp,ghp->blgh', states, c)) + d_skip.astype(F32).reshape(S5_GROUPS, S5_CH) * uf
    y = jax.nn.gelu(y.reshape(B, L, GROUP_WIDTH))
    return y * jax.nn.sigmoid(y @ w_glu.astype(F32) + b_glu.astype(F32))


def retention_mixer(q, k, v, g, cos, sin, norm_g):
    B, L = q.shape[:2]
    q = apply_rope(q.astype(F32).reshape(B, L, RET_HEADS, RET_DK), cos, sin)
    k = apply_rope(k.astype(F32).reshape(B, L, RET_HEADS, RET_DK), cos, sin) * RET_DK ** -0.5
    v = v.astype(F32).reshape(B, L, RET_HEADS, RET_DV)
    log_gamma = jnp.log1p(-(2.0 ** (-5.0 - jnp.arange(RET_HEADS, dtype=F32))))
    qc, kc, vc = to_chunks(q), to_chunks(k), to_chunks(v)
    nc = qc.shape[2]
    idx = jnp.arange(CHUNK, dtype=F32)
    rel = idx[:, None] - idx[None, :]
    dmask = jnp.where(rel >= 0, jnp.exp(jnp.maximum(rel, 0.0) * log_gamma[:, None, None]), 0.0)
    inner = jnp.einsum('bhncs,bhnsv->bhncv', jnp.einsum('bhncd,bhnsd->bhncs', qc, kc) * dmask[None, :, None], vc)
    xi = jnp.exp((idx + 1.0) * log_gamma[:, None])
    zeta = jnp.exp((CHUNK - 1.0 - idx) * log_gamma[:, None])
    d_state = jnp.einsum('bhnsd,bhnsv->bhndv', kc * zeta[None, :, None, :, None], vc)
    chunk_decay = jnp.broadcast_to(jnp.exp(CHUNK * log_gamma)[None, :, None, None, None], (B, RET_HEADS, nc, 1, 1))
    s_prev = scan_chunk_states(d_state, chunk_decay)
    cross = jnp.einsum('bhncd,bhndv->bhncv', qc, s_prev) * xi[None, :, None, :, None]
    o = from_chunks(inner + cross)
    o = o - jnp.mean(o, axis=-1, keepdims=True)
    o = o * lax.rsqrt(jnp.mean(o * o, axis=-1, keepdims=True) + EPS)
    o = o.reshape(B, L, RET_HEADS * RET_DV) * norm_g.astype(F32)
    return o * jax.nn.silu(g.astype(F32))


def gdn_mixer(q, k, v, beta_raw, a_raw, gate, conv_w, a_log, dt_bias, norm_g):
    B, L = q.shape[:2]
    qkv = jnp.concatenate([q, k, v], axis=-1).astype(F32)
    qkv = jax.nn.silu(causal_depthwise_conv(qkv, conv_w.astype(F32)))
    nq = GDN_HEADS * GDN_DK
    q, k, v = qkv[..., :nq], qkv[..., nq:2 * nq], qkv[..., 2 * nq:]
    q = l2_norm(q.reshape(B, L, GDN_HEADS, GDN_DK)) * GDN_DK ** -0.5
    k = l2_norm(k.reshape(B, L, GDN_HEADS, GDN_DK))
    v = v.reshape(B, L, GDN_HEADS, GDN_DV)
    beta = jax.nn.sigmoid(beta_raw.astype(F32))
    g = -jnp.exp(a_log.astype(F32)) * jax.nn.softplus(a_raw.astype(F32) + dt_bias.astype(F32))
    qc, kc, vc = to_chunks(q), to_chunks(k), to_chunks(v)
    bc = to_chunks(beta[..., None])[..., 0]
    gcum = jnp.cumsum(to_chunks(g[..., None])[..., 0], axis=-1)
    pos = jnp.arange(CHUNK)
    incl = pos[:, None] >= pos[None, :]
    strict = pos[:, None] > pos[None, :]
    diff = gcum[..., :, None] - gcum[..., None, :]
    decay_incl = jnp.exp(jnp.where(incl, diff, -jnp.inf))
    kk = jnp.einsum('bhncd,bhnsd->bhncs', kc, kc)
    a_mat = jnp.where(strict, kk * decay_incl, 0.0) * bc[..., :, None]
    rhs = jnp.concatenate([vc * bc[..., None], kc * (bc * jnp.exp(gcum))[..., None]], axis=-1)
    sol = lax.linalg.triangular_solve(a_mat, rhs, left_side=True, lower=True, unit_diagonal=True)
    u_c, w_c = sol[..., :GDN_DV], sol[..., GDN_DV:]
    aqk = jnp.einsum('bhncd,bhnsd->bhncs', qc, kc) * decay_incl
    q_dec = qc * jnp.exp(gcum)[..., None]
    k_end = kc * jnp.exp(gcum[..., -1:] - gcum)[..., None]
    dec_last = jnp.exp(gcum[..., -1])
    xs = tuple(jnp.moveaxis(t, 2, 0) for t in (u_c, w_c, q_dec, aqk, k_end, dec_last))

    def step(S, inp):
        u_i, w_i, qd_i, a_i, ke_i, dl_i = inp
        v_new = u_i - jnp.einsum('bhck,bhkv->bhcv', w_i, S)
        o_i = jnp.einsum('bhck,bhkv->bhcv', qd_i, S) + jnp.einsum('bhcs,bhsv->bhcv', a_i, v_new)
        S = S * dl_i[..., None, None] + jnp.einsum('bhck,bhcv->bhkv', ke_i, v_new)
        return S, o_i

    s0 = jnp.zeros((B, GDN_HEADS, GDN_DK, GDN_DV), F32)
    _, o = lax.scan(step, s0, xs)
    o = head_rms_norm(from_chunks(jnp.moveaxis(o, 0, 2)), norm_g)
    return o.reshape(B, L, GDN_HEADS * GDN_DV) * jax.nn.silu(gate.astype(F32))


def swiglu(x, w_up, w_down):
    gu = x @ w_up
    g, u = jnp.split(gu, 2, axis=-1)
    return (jax.nn.silu(g) * u) @ w_down


def setup_inputs(seed: int = 0) -> dict:
    key = jax.random.key(seed)
    ks = jax.random.split(key, 32)
    nrm = jax.random.normal

    def gain(k, shape):
        return 1.0 + 0.02 * nrm(k, shape, F32)

    x = nrm(ks[0], (BATCH, SEQ, D_MODEL), F32)
    p = nrm(ks[1], (DEPTH, BATCH, SEQ, PLE_DIM), F32)
    positions = jnp.arange(SEQ, dtype=jnp.int32)[None, :] + jax.random.randint(ks[2], (BATCH, 1), 0, SEQ, dtype=jnp.int32)
    norm_mix = gain(ks[3], (DEPTH, D_MODEL))
    w_in = nrm(ks[4], (DEPTH, D_MODEL, IN_WIDTH), F32) * D_MODEL ** -0.5
    w_out = nrm(ks[5], (DEPTH, MIX_WIDTH, D_MODEL), F32) * MIX_WIDTH ** -0.5
    gla_w_a2 = nrm(ks[6], (DEPTH, GLA_RANK, GLA_HEADS * GLA_DK), F32) * GLA_RANK ** -0.5
    gla_b_a = 0.1 * nrm(ks[7], (DEPTH, GLA_HEADS * GLA_DK), F32)
    gla_norm = gain(ks[8], (DEPTH, GLA_DV))
    s5_lam_re = -0.5 + 0.01 * nrm(ks[9], (DEPTH, S5_GROUPS, S5_STATE), F32)
    s5_lam_im = math.pi * jnp.arange(S5_STATE, dtype=F32) + 0.01 * nrm(ks[10], (DEPTH, S5_GROUPS, S5_STATE), F32)
    s5_log_dt = jax.random.uniform(ks[11], (DEPTH, S5_GROUPS), F32, math.log(S5_DT_MIN), math.log(S5_DT_MAX))
    s5_b_re = nrm(ks[12], (DEPTH, S5_GROUPS, S5_STATE, S5_CH), F32) * (2 * S5_CH) ** -0.5
    s5_b_im = nrm(ks[13], (DEPTH, S5_GROUPS, S5_STATE, S5_CH), F32) * (2 * S5_CH) ** -0.5
    s5_c_re = nrm(ks[14], (DEPTH, S5_GROUPS, S5_CH, S5_STATE), F32) * S5_STATE ** -0.5
    s5_c_im = nrm(ks[15], (DEPTH, S5_GROUPS, S5_CH, S5_STATE), F32) * S5_STATE ** -0.5
    s5_d = nrm(ks[16], (DEPTH, GROUP_WIDTH), F32)
    s5_w_glu = nrm(ks[17], (DEPTH, GROUP_WIDTH, GROUP_WIDTH), F32) * GROUP_WIDTH ** -0.5
    s5_b_glu = 0.01 * nrm(ks[18], (DEPTH, GROUP_WIDTH), F32)
    ret_norm = gain(ks[19], (DEPTH, GROUP_WIDTH))
    gdn_conv = nrm(ks[20], (DEPTH, GDN_CONV, 3 * GDN_HEADS * GDN_DK), F32) * GDN_CONV ** -0.5
    gdn_a_log = jnp.log(jax.random.uniform(ks[21], (DEPTH, GDN_HEADS), F32, 1.0, 16.0))
    dt = jnp.exp(jax.random.uniform(ks[22], (DEPTH, GDN_HEADS), F32, math.log(GDN_DT_MIN), math.log(GDN_DT_MAX)))
    gdn_dt_bias = dt + jnp.log(-jnp.expm1(-dt))
    gdn_norm = gain(ks[23], (DEPTH, GDN_DV))
    norm_ffn = gain(ks[24], (DEPTH, D_MODEL))
    w_ffn_up = nrm(ks[25], (DEPTH, D_MODEL, 2 * FFN_HIDDEN), F32) * D_MODEL ** -0.5
    w_ffn_down = nrm(ks[26], (DEPTH, FFN_HIDDEN, D_MODEL), F32) * FFN_HIDDEN ** -0.5
    norm_ple = gain(ks[27], (DEPTH, D_MODEL))
    w_ple_gate = nrm(ks[28], (DEPTH, D_MODEL, D_MODEL), F32) * D_MODEL ** -0.5
    w_ple_proj = nrm(ks[29], (DEPTH, PLE_DIM, D_MODEL), F32) * PLE_DIM ** -0.5
    norm_final = gain(ks[30], (D_MODEL,))
    return {"x": x, "p": p, "positions": positions, "norm_mix": norm_mix, "w_in": w_in, "w_out": w_out,
            "gla_w_a2": gla_w_a2, "gla_b_a": gla_b_a, "gla_norm": gla_norm,
            "s5_lam_re": s5_lam_re, "s5_lam_im": s5_lam_im, "s5_log_dt": s5_log_dt,
            "s5_b_re": s5_b_re, "s5_b_im": s5_b_im, "s5_c_re": s5_c_re, "s5_c_im": s5_c_im,
            "s5_d": s5_d, "s5_w_glu": s5_w_glu, "s5_b_glu": s5_b_glu, "ret_norm": ret_norm,
            "gdn_conv": gdn_conv, "gdn_a_log": gdn_a_log, "gdn_dt_bias": gdn_dt_bias, "gdn_norm": gdn_norm,
            "norm_ffn": norm_ffn, "w_ffn_up": w_ffn_up, "w_ffn_down": w_ffn_down,
            "norm_ple": norm_ple, "w_ple_gate": w_ple_gate, "w_ple_proj": w_ple_proj, "norm_final": norm_final}


def reference(x, p, positions, norm_mix, w_in, w_out, gla_w_a2, gla_b_a, gla_norm,
              s5_lam_re, s5_lam_im, s5_log_dt, s5_b_re, s5_b_im, s5_c_re, s5_c_im,
              s5_d, s5_w_glu, s5_b_glu, ret_norm, gdn_conv, gdn_a_log, gdn_dt_bias, gdn_norm,
              norm_ffn, w_ffn_up, w_ffn_down, norm_ple, w_ple_gate, w_ple_proj, norm_final):
    cos, sin = rope_tables(positions)
    h = x
    for i in range(DEPTH):
        hn = rms_norm(h, norm_mix[i])
        z = hn @ w_in[i]
        (aq, ak, av, a_low, ar, su, rq, rk, rv, rg, dq, dk, dv, db, da, dg) = split_columns(z)
        o_a = gla_mixer(aq, ak, av, a_low, ar, gla_w_a2[i], gla_b_a[i], gla_norm[i])
        o_b = s5_mixer(su, s5_lam_re[i], s5_lam_im[i], s5_log_dt[i], s5_b_re[i], s5_b_im[i],
                       s5_c_re[i], s5_c_im[i], s5_d[i], s5_w_glu[i], s5_b_glu[i])
        o_c = retention_mixer(rq, rk, rv, rg, cos, sin, ret_norm[i])
        o_d = gdn_mixer(dq, dk, dv, db, da, dg, gdn_conv[i], gdn_a_log[i], gdn_dt_bias[i], gdn_norm[i])
        mix = jnp.concatenate([o_a, o_b, o_c, o_d], axis=-1).astype(h.dtype)
        h = h + mix @ w_out[i]
        h = h + swiglu(rms_norm(h, norm_ffn[i]), w_ffn_up[i], w_ffn_down[i])
        ple_gate = jax.nn.sigmoid(rms_norm(h, norm_ple[i]) @ w_ple_gate[i])
        h = h + (p[i] @ w_ple_proj[i]) * ple_gate
    return rms_norm(h, norm_final)
```

```python
import functools
import math

import numpy as np
import jax
import jax.numpy as jnp
from jax import lax
from jax.experimental import pallas as pl
from jax.experimental.pallas import tpu as pltpu

F32 = jnp.float32
BF16 = jnp.bfloat16
HI = lax.Precision.HIGHEST

D_MODEL = 1024
PLE_DIM = 256
GROUP_WIDTH = 256
NUM_HEADS = 4
CHUNK = 64
EPS = 1e-6

GLA_DK = 32
GLA_RANK = 16
GLA_GATE_NORM = 16.0
S5_CH = 16
S5_GROUPS = 16
S5_STATE = 64
S5_SUB = 8
HEAD_DIM = 64
ROPE_BASE = 10000.0
GDN_CONV = 4
FFN_HIDDEN = 2816

MIX_TL = 256
S5_ROWS = 64
FFN_TM = 512
FFN_SLAB = 256

VMEM_LIMIT = 56 * 1024 * 1024

_IN_OFFS = np.cumsum([0, 128, 128, 256, 16, 256, 256, 256, 256, 256, 256, 256, 256, 256, 4, 4, 256])
(_AQ, _AK, _AV, _ALOW, _AR, _SU, _RQ, _RK, _RV, _RG, _DQ, _DK, _DV, _DB, _DA, _DG) = [
    (int(_IN_OFFS[i]), int(_IN_OFFS[i + 1])) for i in range(16)]
Z_GLA = 0
Z_RET = 768
Z_GDN = 1792
Z_SMALL = 2816
Z_WIDTH = 2944
SM_BETA = 16
SM_A = 20


def _dot(a, b):
    return jnp.dot(a.astype(BF16), b.astype(BF16), preferred_element_type=F32)


def _dot_nt(a, b):
    return lax.dot_general(a.astype(BF16), b.astype(BF16), (((1,), (1,)), ((), ())),
                           preferred_element_type=F32)


def _dot_hi(a, b):
    return jnp.dot(a, b, preferred_element_type=F32, precision=HI)


def _sigmoid(x):
    return 1.0 / (1.0 + jnp.exp(-x))


def _silu(x):
    return x * _sigmoid(x)


def _softplus(x):
    return jnp.maximum(x, 0.0) + jnp.log1p(jnp.exp(-jnp.abs(x)))


def _rms(x, g):
    return x * lax.rsqrt(jnp.mean(x * x, axis=-1, keepdims=True) + EPS) * g


def _stack4(x):
    return jnp.concatenate([x, x, x, x], axis=0)


def _rope_kernel(pos_ref, invf_ref, sign_ref, cos_ref, sin_ref):
    ang = pos_ref[...].astype(F32) * invf_ref[...]
    cos_ref[...] = jnp.cos(ang)
    sin_ref[...] = jnp.sin(ang) * sign_ref[...]


def _rope_tables(positions):
    B, L = positions.shape
    inv_freq = ROPE_BASE ** (-jnp.linspace(0.0, 1.0, HEAD_DIM // 2, dtype=F32))
    invf = jnp.tile(inv_freq, 4)[None, :]
    sign = np.where((np.arange(128) % HEAD_DIM) < HEAD_DIM // 2, -1.0, 1.0).astype(np.float32)[None, :]
    spec = pl.BlockSpec((None, L, 128), lambda b: (b, 0, 0))
    vec = pl.BlockSpec((1, 128), lambda b: (0, 0))
    return pl.pallas_call(
        _rope_kernel,
        grid=(B,),
        in_specs=[pl.BlockSpec((None, L, 1), lambda b: (b, 0, 0)), vec, vec],
        out_specs=[spec, spec],
        out_shape=[jax.ShapeDtypeStruct((B, L, 128), F32)] * 2,
        name="rope_tables",
    )(positions[:, :, None], invf, jnp.asarray(sign))


def _s5_kernel(*refs):
    h_refs = refs[:S5_SUB]
    (g_ref, wsu_ref, kbig_ref, bbig_ref, cbig_ref, pwr_ref, pwi_ref, d8_ref,
     y_ref, cr_ref, ci_ref) = refs[S5_SUB:]
    n_state = S5_GROUPS * S5_STATE

    @pl.when(pl.program_id(1) == 0)
    def _():
        cr_ref[...] = jnp.zeros_like(cr_ref)
        ci_ref[...] = jnp.zeros_like(ci_ref)

    g = g_ref[...]
    wsu = wsu_ref[...]
    u8 = jnp.concatenate([_dot(_rms(h_refs[j][...], g), wsu) for j in range(S5_SUB)], axis=1)
    u8b = u8.astype(BF16)
    y = jnp.dot(u8b, kbig_ref[...], preferred_element_type=F32) + d8_ref[...] * u8
    inc = jnp.dot(u8b, bbig_ref[...], preferred_element_type=F32)
    xr = inc[:, :n_state]
    xi = inc[:, n_state:]

    rows = lax.broadcasted_iota(jnp.int32, (S5_ROWS, n_state), 0)
    ar0 = pwr_ref[0:1, :]
    ai0 = pwi_ref[0:1, :]
    cr = cr_ref[...]
    ci = ci_ref[...]
    first = rows == 0
    xr = xr + jnp.where(first, ar0 * cr - ai0 * ci, 0.0)
    xi = xi + jnp.where(first, ar0 * ci + ai0 * cr, 0.0)
    shift = 1
    lvl = 0
    while shift < S5_ROWS:
        ar = pwr_ref[lvl:lvl + 1, :]
        ai = pwi_ref[lvl:lvl + 1, :]
        keep = rows >= shift
        sr = jnp.where(keep, pltpu.roll(xr, shift, 0), 0.0)
        si = jnp.where(keep, pltpu.roll(xi, shift, 0), 0.0)
        xr, xi = xr + ar * sr - ai * si, xi + ar * si + ai * sr
        shift *= 2
        lvl += 1
    pr = jnp.where(first, cr, pltpu.roll(xr, 1, 0))
    pi = jnp.where(first, ci, pltpu.roll(xi, 1, 0))
    cr_ref[...] = xr[S5_ROWS - 1:S5_ROWS, :]
    ci_ref[...] = xi[S5_ROWS - 1:S5_ROWS, :]
    xprev = jnp.concatenate([pr, pi], axis=1).astype(BF16)
    y_ref[...] = y + jnp.dot(xprev, cbig_ref[...], preferred_element_type=F32)


def _s5_prepare(lam_re, lam_im, log_dt, b_re, b_im, c_re, c_im, d_skip):
    G, P, H, M = S5_GROUPS, S5_STATE, S5_CH, S5_SUB
    lr = jnp.minimum(lam_re.astype(F32), -1e-4)
    li = lam_im.astype(F32)
    dt = jnp.exp(log_dt.astype(F32))[:, None]

    def apow(t):
        mag = jnp.exp(lr * dt * t)
        ang = li * dt * t
        return mag * jnp.cos(ang), mag * jnp.sin(ang)

    ar, ai = apow(1.0)
    nr, ni = ar - 1.0, ai
    den = lr * lr + li * li
    fr = (nr * lr + ni * li) / den
    fi = (ni * lr - nr * li) / den
    bbr = fr[..., None] * b_re - fi[..., None] * b_im
    bbi = fr[..., None] * b_im + fi[..., None] * b_re
    pw = [apow(float(t)) for t in range(M + 1)]
    eye = jnp.eye(G, dtype=F32)

    ks = []
    for t in range(M):
        pr, pi = pw[t]
        car = c_re * pr[:, None, :] - c_im * pi[:, None, :]
        cai = c_re * pi[:, None, :] + c_im * pr[:, None, :]
        ks.append(jnp.einsum('ghp,gpi->ghi', car, bbr, precision=HI)
                  - jnp.einsum('ghp,gpi->ghi', cai, bbi, precision=HI))
    kt = jnp.stack(ks)
    lag = np.arange(M)[None, :] - np.arange(M)[:, None]
    kjt = jnp.where((lag >= 0)[:, :, None, None, None], kt[np.maximum(lag, 0)], 0.0)
    kbig = (kjt.transpose(0, 2, 4, 1, 3)[:, :, :, :, None, :]
            * eye[None, :, None, None, :, None]).reshape(M * G * H, M * G * H)

    br, bi = [], []
    for j in range(M):
        pr, pi = pw[M - 1 - j]
        br.append(pr[..., None] * bbr - pi[..., None] * bbi)
        bi.append(pr[..., None] * bbi + pi[..., None] * bbr)
    bst = jnp.stack([jnp.stack(br), jnp.stack(bi)])
    bbig = (bst.transpose(1, 2, 4, 0, 3)[:, :, :, :, None, :]
            * eye[None, :, None, None, :, None]).reshape(M * G * H, 2 * G * P)

    qr, qi = [], []
    for t in range(M):
        pr, pi = pw[t + 1]
        qr.append(c_re * pr[:, None, :] - c_im * pi[:, None, :])
        qi.append(-(c_re * pi[:, None, :] + c_im * pr[:, None, :]))
    qst = jnp.stack([jnp.stack(qr), jnp.stack(qi)])
    cbig = (qst.transpose(0, 2, 4, 1, 3)[:, :, :, :, None, :]
            * eye[None, :, None, None, :, None]).reshape(2 * G * P, M * G * H)

    n_lvl = int(math.log2(S5_ROWS))
    pws = [apow(float(M * 2 ** s)) for s in range(n_lvl)]
    pwr = jnp.stack([p[0].reshape(-1) for p in pws])
    pwi = jnp.stack([p[1].reshape(-1) for p in pws])
    pad = (-n_lvl) % 8
    pwr = jnp.pad(pwr, ((0, pad), (0, 0)))
    pwi = jnp.pad(pwi, ((0, pad), (0, 0)))
    d8 = jnp.tile(d_skip.astype(F32), M)[None, :]
    return kbig.astype(BF16), bbig.astype(BF16), cbig.astype(BF16), pwr, pwi, d8


def _s5_call(h, norm_g, w_su, prep):
    B, L, D = h.shape
    kbig, bbig, cbig, pwr, pwi, d8 = prep
    rows = L // S5_SUB
    wide = S5_SUB * GROUP_WIDTH
    n_state = S5_GROUPS * S5_STATE
    h8 = h.reshape(B, rows, S5_SUB * D)

    def const(shape):
        return pl.BlockSpec(shape, lambda b, i: (0,) * len(shape), pipeline_mode=pl.Buffered(1))

    h_specs = [pl.BlockSpec((None, S5_ROWS, D), functools.partial(lambda b, i, j: (b, i, j), j=j))
               for j in range(S5_SUB)]
    y8 = pl.pallas_call(
        _s5_kernel,
        grid=(B, rows // S5_ROWS),
        in_specs=h_specs + [const((1, D)), const((D, GROUP_WIDTH)), const((wide, wide)),
                            const((wide, 2 * n_state)), const((2 * n_state, wide)),
                            const(pwr.shape), const(pwi.shape), const((1, wide))],
        out_specs=pl.BlockSpec((None, S5_ROWS, wide), lambda b, i: (b, i, 0)),
        out_shape=jax.ShapeDtypeStruct((B, rows, wide), F32),
        scratch_shapes=[pltpu.VMEM((1, n_state), F32), pltpu.VMEM((1, n_state), F32)],
        compiler_params=pltpu.CompilerParams(dimension_semantics=("arbitrary", "arbitrary"),
                                             vmem_limit_bytes=VMEM_LIMIT),
        name="s5_mixer",
    )(*([h8] * S5_SUB), norm_g, w_su, kbig, bbig, cbig, pwr, pwi, d8)
    return y8.reshape(B, L, GROUP_WIDTH)


def _mixer_consts():
    TL, C, H = MIX_TL, CHUNK, NUM_HEADS
    t = np.arange(TL)
    ltri = ((t[:, None] // C == t[None, :] // C) & (t[None, :] <= t[:, None])).astype(np.float32)
    lane256 = np.arange(256)
    lane128 = np.arange(128)
    hm128 = (lane128[None, :] // GLA_DK == np.arange(H)[:, None]).astype(np.float32)
    hm256 = (lane256[None, :] // HEAD_DIM == np.arange(H)[:, None]).astype(np.float32)
    bm = (lane256[:, None] // HEAD_DIM == lane256[None, :] // HEAD_DIM).astype(np.float32)
    bmt = (lane256[:, None] // HEAD_DIM == lane128[None, :] // GLA_DK).astype(np.float32)
    c = np.arange(C)
    caus4 = np.tile((c[:, None] >= c[None, :]).astype(np.float32), (H, 1))
    s_lane = lane256 % HEAD_DIM
    incl = (c[:, None] >= s_lane[None, :]).astype(np.float32)
    strict = (c[:, None] > s_lane[None, :]).astype(np.float32)
    idiag = (c[:, None] == s_lane[None, :]).astype(np.float32)
    log_gamma = np.log1p(-(2.0 ** (-5.0 - np.arange(H, dtype=np.float32)))).astype(np.float32)
    idx = np.arange(TL, dtype=np.float32)
    rel = idx[:, None] - idx[None, :]
    dmask = np.where(rel >= 0, np.exp(np.maximum(rel, 0.0)[None] * log_gamma[:, None, None]), 0.0)
    dmask = dmask.reshape(H * TL, TL).astype(np.float32)
    lg_lane = log_gamma[lane256 // HEAD_DIM]
    xi = np.exp((idx[:, None] + 1.0) * lg_lane[None, :]).astype(np.float32)
    zeta = np.exp((TL - 1.0 - idx[:, None]) * lg_lane[None, :]).astype(np.float32)
    gchunk = np.exp(TL * lg_lane)[None, :].astype(np.float32)
    e_beta = np.zeros((128, 256), np.float32)
    e_a = np.zeros((128, 256), np.float32)
    for h in range(H):
        e_beta[SM_BETA + h, h * HEAD_DIM:(h + 1) * HEAD_DIM] = 1.0
        e_a[SM_A + h, h * HEAD_DIM:(h + 1) * HEAD_DIM] = 1.0
    return dict(ltri=ltri, hm128=hm128, hm256=hm256, bm=bm, bmt=bmt, caus4=caus4, incl=incl, strict=strict,
                idiag=idiag, dmask=dmask, xi=xi, zeta=zeta, gchunk=gchunk, e_beta=e_beta, e_a=e_a,
                m64=bm / HEAD_DIM)


_CONST_ORDER = ("ltri", "hm128", "hm256", "bm", "bmt", "caus4", "incl", "strict", "idiag", "dmask", "xi",
                "zeta", "gchunk", "e_beta", "e_a", "m64")
_PARAM_ORDER = ("norm_mix", "w_z", "w_out", "w_a2", "b_a", "gla_g", "ret_g", "gdn_g", "conv_w", "alog_v",
                "dtb_v", "w_glu", "b_glu")


def _mixer_kernel(*refs):
    n_c, n_p = len(_CONST_ORDER), len(_PARAM_ORDER)
    h_ref, ys5_ref, cos_ref, sin_ref = refs[:4]
    cst = dict(zip(_CONST_ORDER, refs[4:4 + n_c]))
    prm = dict(zip(_PARAM_ORDER, refs[4 + n_c:4 + n_c + n_p]))
    o_ref = refs[4 + n_c + n_p]
    st_gla, s_ret, s_gdn, xpad = refs[4 + n_c + n_p + 1:]
    TL, C, H = MIX_TL, CHUNK, NUM_HEADS

    @pl.when(pl.program_id(1) == 0)
    def _():
        st_gla[...] = jnp.zeros_like(st_gla)
        s_ret[...] = jnp.zeros_like(s_ret)
        s_gdn[...] = jnp.zeros_like(s_gdn)
        xpad[0:8, :] = jnp.zeros((8, 3 * GROUP_WIDTH), F32)

    h = h_ref[...]
    hn = _rms(h, prm["norm_mix"][...]).astype(BF16)
    w_z = prm["w_z"]

    def proj(lo, hi):
        return jnp.dot(hn, w_z[:, lo:hi], preferred_element_type=F32)

    hm128 = cst["hm128"][...]
    hm256 = cst["hm256"][...]
    bm = cst["bm"][...]
    m64 = cst["m64"][...]
    ltri = cst["ltri"][...]
    sm = proj(Z_SMALL, Z_WIDTH)

    def take_heads(stacked, rows):
        out = stacked[0:rows] * hm256[0:1]
        for hh in range(1, H):
            out = out + stacked[hh * rows:(hh + 1) * rows] * hm256[hh:hh + 1]
        return out

    zg = proj(Z_GLA, Z_GLA + 768)
    q, k, v, r = zg[:, 0:128], zg[:, 128:256], zg[:, 256:512], zg[:, 512:768]
    gk = _dot_hi(sm, prm["w_a2"][...]) + prm["b_a"][...]
    gk = (jnp.minimum(gk, 0.0) - jnp.log1p(jnp.exp(-jnp.abs(gk)))) / GLA_GATE_NORM
    bcum = _dot_hi(ltri, gk)
    caus4 = cst["caus4"][...]
    bmt = cst["bmt"][...]
    st = st_gla[...]
    o_chunks = []
    for c in range(TL // C):
        sl = slice(c * C, (c + 1) * C)
        b_c = bcum[sl]
        b_last = b_c[C - 1:C, :]
        q_t = q[sl] * jnp.exp(b_c) * (GLA_DK ** -0.5)
        k_t = k[sl] * jnp.exp(-b_c)
        k_end = k[sl] * jnp.exp(b_last - b_c)
        v_c = v[sl]
        qs = jnp.concatenate([q_t * hm128[hh:hh + 1] for hh in range(H)], axis=0)
        att = _dot_nt(qs, k_t) * caus4
        o_c = take_heads(_dot(att, v_c), C) + _dot_nt(q_t, st)
        st = st * jnp.exp(b_last) + bmt * _dot(v_c.T, k_end)
        o_chunks.append(o_c)
    st_gla[...] = st
    o_a = jnp.concatenate(o_chunks, axis=0)
    o_a = o_a * lax.rsqrt(_dot(o_a * o_a, m64) + EPS) * prm["gla_g"][...] * _silu(r)

    zr = proj(Z_RET, Z_RET + 1024)
    cos2 = jnp.concatenate([cos_ref[...], cos_ref[...]], axis=1)
    sin2 = jnp.concatenate([sin_ref[...], sin_ref[...]], axis=1)
    first_half = (lax.broadcasted_iota(jnp.int32, (TL, 256), 1) % HEAD_DIM) < HEAD_DIM // 2

    def rope(x):
        rot = jnp.where(first_half, pltpu.roll(x, 256 - HEAD_DIM // 2, 1), pltpu.roll(x, HEAD_DIM // 2, 1))
        return x * cos2 + rot * sin2

    rq = rope(zr[:, 0:256])
    rk = rope(zr[:, 256:512]) * (HEAD_DIM ** -0.5)
    rv = zr[:, 512:768]
    rg = zr[:, 768:1024]
    qs = jnp.concatenate([rq * hm256[hh:hh + 1] for hh in range(H)], axis=0)
    sc = _dot_nt(qs, rk) * cst["dmask"][...]
    s_prev = s_ret[...]
    o_c = take_heads(_dot(sc, rv), TL) + _dot(rq, s_prev) * cst["xi"][...]
    s_ret[...] = s_prev * cst["gchunk"][...] + bm * _dot((rk * cst["zeta"][...]).T, rv)
    o_c = o_c - _dot_hi(o_c, m64)
    o_c = o_c * lax.rsqrt(_dot(o_c * o_c, m64) + EPS) * prm["ret_g"][...] * _silu(rg)

    zd = proj(Z_GDN, Z_GDN + 1024)
    xpad[8:8 + TL, :] = zd[:, 0:768]
    cw = prm["conv_w"][...]
    xc = cw[GDN_CONV - 1:GDN_CONV, :] * zd[:, 0:768]
    for j in range(GDN_CONV - 1):
        xc = xc + cw[j:j + 1, :] * xpad[5 + j:5 + j + TL, :]
    xpad[0:8, :] = zd[TL - 8:TL, 0:768]
    xc = _silu(xc)
    dq, dk, dv = xc[:, 0:256], xc[:, 256:512], xc[:, 512:768]
    ones64 = bm
    dq = dq * lax.rsqrt(_dot(dq * dq, ones64) + EPS) * (HEAD_DIM ** -0.5)
    dk = dk * lax.rsqrt(_dot(dk * dk, ones64) + EPS)
    beta = _dot_hi(_sigmoid(sm), cst["e_beta"][...])
    g_log = _dot_hi(-jnp.exp(prm["alog_v"][...]) * _softplus(sm + prm["dtb_v"][...]), cst["e_a"][...])
    gcum = _dot_hi(ltri, g_log)
    incl = cst["incl"][...]
    strict = cst["strict"][...]
    idiag = cst["idiag"][...]
    sg = s_gdn[...]
    o_chunks = []
    for c in range(TL // C):
        sl = slice(c * C, (c + 1) * C)
        gc = gcum[sl]
        g_last = gc[C - 1:C, :]
        k_c, q_c, v_c, b_c = dk[sl], dq[sl], dv[sl], beta[sl]
        g_row = jnp.sum(gc * idiag, axis=0, keepdims=True)
        diff = gc - g_row
        dec = jnp.where(incl > 0.0, jnp.exp(jnp.where(incl > 0.0, diff, 0.0)), 0.0)
        kq = _dot_nt(jnp.concatenate([k_c, q_c], axis=0), _stack4(k_c) * bm)
        a_mat = kq[0:C] * dec * strict * b_c
        aqk = kq[C:2 * C] * dec
        p_mat = -a_mat
        t_mat = idiag + p_mat
        p_mat = _dot_hi(p_mat, _stack4(p_mat) * bm)
        for _ in range(int(math.log2(C)) - 2):
            pt = _dot_hi(jnp.concatenate([p_mat, t_mat], axis=0), _stack4(p_mat) * bm)
            p_mat = pt[0:C]
            t_mat = t_mat + pt[C:2 * C]
        t_mat = t_mat + _dot_hi(t_mat, _stack4(p_mat) * bm)
        eg = jnp.exp(gc)
        u_c = _dot_hi(t_mat, _stack4(v_c * b_c) * bm)
        w_c = _dot_hi(t_mat, _stack4(k_c * b_c * eg) * bm)
        rr = _dot(jnp.concatenate([w_c, q_c * eg], axis=0), sg)
        v_new = u_c - rr[0:C]
        o_chunks.append(rr[C:2 * C] + _dot(aqk, _stack4(v_new) * bm))
        k_end = k_c * jnp.exp(g_last - gc)
        sg = sg * jnp.exp(g_last) + bm * _dot(k_end.T, v_new)
    s_gdn[...] = sg
    o_d = jnp.concatenate(o_chunks, axis=0)
    o_d = o_d * lax.rsqrt(_dot(o_d * o_d, m64) + EPS) * prm["gdn_g"][...] * _silu(zd[:, 768:1024])

    y = ys5_ref[...]
    y = 0.5 * y * (1.0 + jnp.tanh(math.sqrt(2.0 / math.pi) * (y + 0.044715 * (y * y * y))))
    o_b = y * _sigmoid(_dot(y, prm["w_glu"][...]) + prm["b_glu"][...])

    mix = jnp.concatenate([o_a, o_b, o_c, o_d], axis=1).astype(BF16)
    o_ref[...] = h + jnp.dot(mix, prm["w_out"][...], preferred_element_type=F32)


def _mixer_call(h, ys5, cos_t, sin_t, consts, params):
    B, L, D = h.shape
    TL = MIX_TL

    def const(arr):
        nd = arr.ndim
        return pl.BlockSpec(arr.shape, lambda b, i: (0,) * nd, pipeline_mode=pl.Buffered(1))

    def tile(width):
        return pl.BlockSpec((None, TL, width), lambda b, i: (b, i, 0))

    c_list = [consts[n] for n in _CONST_ORDER]
    p_list = [params[n] for n in _PARAM_ORDER]
    return pl.pallas_call(
        _mixer_kernel,
        grid=(B, L // TL),
        in_specs=[tile(D), tile(GROUP_WIDTH), tile(128), tile(128)] + [const(a) for a in c_list + p_list],
        out_specs=tile(D),
        out_shape=jax.ShapeDtypeStruct((B, L, D), F32),
        scratch_shapes=[pltpu.VMEM((256, 128), F32), pltpu.VMEM((256, 256), F32), pltpu.VMEM((256, 256), F32),
                        pltpu.VMEM((TL + 8, 3 * GROUP_WIDTH), F32)],
        compiler_params=pltpu.CompilerParams(dimension_semantics=("arbitrary", "arbitrary"),
                                             vmem_limit_bytes=VMEM_LIMIT),
        name="mixer",
    )(h, ys5, cos_t, sin_t, *c_list, *p_list)


def _ffn_kernel(h_ref, p_ref, gf_ref, wup_ref, wdn_ref, gp_ref, wg_ref, wp_ref, gl_ref, o_ref, acc_ref, *,
                final):
    h = h_ref[...]
    hn = _rms(h, gf_ref[...]).astype(BF16)
    for s in range(FFN_HIDDEN // FFN_SLAB):
        lo = s * FFN_SLAB
        g = jnp.dot(hn, wup_ref[:, lo:lo + FFN_SLAB], preferred_element_type=F32)
        u = jnp.dot(hn, wup_ref[:, FFN_HIDDEN + lo:FFN_HIDDEN + lo + FFN_SLAB], preferred_element_type=F32)
        part = jnp.dot((_silu(g) * u).astype(BF16), wdn_ref[lo:lo + FFN_SLAB, :], preferred_element_type=F32)
        if s == 0:
            acc_ref[...] = part
        else:
            acc_ref[...] += part
    h2 = h + acc_ref[...]
    gate = _sigmoid(jnp.dot(_rms(h2, gp_ref[...]).astype(BF16), wg_ref[...], preferred_element_type=F32))
    h3 = h2 + jnp.dot(p_ref[...].astype(BF16), wp_ref[...], preferred_element_type=F32) * gate
    if final:
        h3 = _rms(h3, gl_ref[...])
    o_ref[...] = h3


def _ffn_call(h2d, p2d, g_ffn, w_up, w_down, g_ple, w_gate, w_proj, g_final, final):
    T, D = h2d.shape
    TM = FFN_TM

    def const(arr):
        nd = arr.ndim
        return pl.BlockSpec(arr.shape, lambda i: (0,) * nd, pipeline_mode=pl.Buffered(1))

    weights = [g_ffn, w_up, w_down, g_ple, w_gate, w_proj, g_final]
    return pl.pallas_call(
        functools.partial(_ffn_kernel, final=final),
        grid=(T // TM,),
        in_specs=[pl.BlockSpec((TM, D), lambda i: (i, 0)), pl.BlockSpec((TM, PLE_DIM), lambda i: (i, 0))]
        + [const(w) for w in weights],
        out_specs=pl.BlockSpec((TM, D), lambda i: (i, 0)),
        out_shape=jax.ShapeDtypeStruct((T, D), F32),
        scratch_shapes=[pltpu.VMEM((TM, D), F32)],
        compiler_params=pltpu.CompilerParams(dimension_semantics=("arbitrary",), vmem_limit_bytes=VMEM_LIMIT),
        name="ffn_ple",
    )(h2d, p2d, *weights)


def _mixer_params(i, norm_mix, w_in, w_out, gla_w_a2, gla_b_a, gla_norm, ret_norm, gdn_conv, gdn_a_log,
                  gdn_dt_bias, gdn_norm, s5_w_glu, s5_b_glu):
    w = w_in[i]

    def cols(seg):
        return w[:, seg[0]:seg[1]]

    small = jnp.concatenate([cols(_ALOW), cols(_DB), cols(_DA),
                             jnp.zeros((D_MODEL, 128 - GLA_RANK - 2 * NUM_HEADS), F32)], axis=1)
    w_z = jnp.concatenate([cols(_AQ), cols(_AK), cols(_AV), cols(_AR), cols(_RQ), cols(_RK), cols(_RV),
                           cols(_RG), cols(_DQ), cols(_DK), cols(_DV), cols(_DG), small], axis=1).astype(BF16)
    w_a2 = jnp.zeros((128, 128), F32).at[0:GLA_RANK, :].set(gla_w_a2[i])
    alog_v = jnp.zeros((1, 128), F32).at[0, SM_A:SM_A + NUM_HEADS].set(gdn_a_log[i])
    dtb_v = jnp.zeros((1, 128), F32).at[0, SM_A:SM_A + NUM_HEADS].set(gdn_dt_bias[i])
    return dict(norm_mix=norm_mix[i][None, :], w_z=w_z, w_out=w_out[i].astype(BF16), w_a2=w_a2,
                b_a=gla_b_a[i][None, :], gla_g=jnp.tile(gla_norm[i], NUM_HEADS)[None, :],
                ret_g=ret_norm[i][None, :], gdn_g=jnp.tile(gdn_norm[i], NUM_HEADS)[None, :],
                conv_w=gdn_conv[i], alog_v=alog_v, dtb_v=dtb_v, w_glu=s5_w_glu[i].astype(BF16),
                b_glu=s5_b_glu[i][None, :])


def kernel(x, p, positions, norm_mix, w_in, w_out, gla_w_a2, gla_b_a, gla_norm, s5_lam_re, s5_lam_im, s5_log_dt,
           s5_b_re, s5_b_im, s5_c_re, s5_c_im, s5_d, s5_w_glu, s5_b_glu, ret_norm, gdn_conv, gdn_a_log,
           gdn_dt_bias, gdn_norm, norm_ffn, w_ffn_up, w_ffn_down, norm_ple, w_ple_gate, w_ple_proj, norm_final):
    B, L, D = x.shape
    depth = w_in.shape[0]
    assert D == D_MODEL and L % max(MIX_TL, S5_SUB * S5_ROWS) == 0 and (B * L) % FFN_TM == 0
    consts = {k: jnp.asarray(v) for k, v in _mixer_consts().items()}
    cos_t, sin_t = _rope_tables(positions)
    h = x
    for i in range(depth):
        prep = _s5_prepare(s5_lam_re[i], s5_lam_im[i], s5_log_dt[i], s5_b_re[i], s5_b_im[i], s5_c_re[i],
                           s5_c_im[i], s5_d[i])
        w_su = w_in[i][:, _SU[0]:_SU[1]].astype(BF16)
        ys5 = _s5_call(h, norm_mix[i][None, :], w_su, prep)
        params = _mixer_params(i, norm_mix, w_in, w_out, gla_w_a2, gla_b_a, gla_norm, ret_norm, gdn_conv,
                               gdn_a_log, gdn_dt_bias, gdn_norm, s5_w_glu, s5_b_glu)
        h = _mixer_call(h, ys5, cos_t, sin_t, consts, params)
        h2 = _ffn_call(h.reshape(B * L, D), p[i].reshape(B * L, PLE_DIM), norm_ffn[i][None, :],
                       w_ffn_up[i].astype(BF16), w_ffn_down[i].astype(BF16), norm_ple[i][None, :],
                       w_ple_gate[i].astype(BF16), w_ple_proj[i].astype(BF16), norm_final[None, :],
                       final=(i == depth - 1))
        h = h2.reshape(B, L, D)
    return h
```

```python
import functools
import math

import numpy as np
import jax
import jax.numpy as jnp
from jax import lax
from jax.experimental import pallas as pl
from jax.experimental.pallas import tpu as pltpu

F32 = jnp.float32
BF16 = jnp.bfloat16
HI = lax.Precision.HIGHEST

D_MODEL = 1024
PLE_DIM = 256
GROUP_WIDTH = 256
NUM_HEADS = 4
CHUNK = 64
EPS = 1e-6

GLA_DK = 32
GLA_RANK = 16
GLA_GATE_NORM = 16.0
S5_CH = 16
S5_GROUPS = 16
S5_STATE = 64
S5_SUB = 8
HEAD_DIM = 64
ROPE_BASE = 10000.0
GDN_CONV = 4
FFN_HIDDEN = 2816

MIX_TL = 256
S5_ROWS = 128
FFN_TM = 512
FFN_SLAB = 256

VMEM_LIMIT = 56 * 1024 * 1024

_IN_OFFS = np.cumsum([0, 128, 128, 256, 16, 256, 256, 256, 256, 256, 256, 256, 256, 256, 4, 4, 256])
(_AQ, _AK, _AV, _ALOW, _AR, _SU, _RQ, _RK, _RV, _RG, _DQ, _DK, _DV, _DB, _DA, _DG) = [
    (int(_IN_OFFS[i]), int(_IN_OFFS[i + 1])) for i in range(16)]
Z_GLA = 0
Z_RET = 768
Z_GDN = 1792
Z_SMALL = 2816
Z_WIDTH = 2944
SM_BETA = 16
SM_A = 20


def _dot(a, b):
    return jnp.dot(a.astype(BF16), b.astype(BF16), preferred_element_type=F32)


def _dot_nt(a, b):
    return lax.dot_general(a.astype(BF16), b.astype(BF16), (((1,), (1,)), ((), ())),
                           preferred_element_type=F32)


def _split(x):
    hi = x.astype(BF16)
    return hi, (x - hi.astype(F32)).astype(BF16)


def _dot_exact_lhs(a, b):
    hi, lo = _split(b)
    n = b.shape[1]
    if n <= 128:
        r = jnp.dot(a.astype(BF16), jnp.concatenate([hi, lo], axis=1), preferred_element_type=F32)
        return r[:, :n] + r[:, n:]
    return (jnp.dot(a.astype(BF16), hi, preferred_element_type=F32)
            + jnp.dot(a.astype(BF16), lo, preferred_element_type=F32))


def _dot_exact_rhs(a, b):
    hi, lo = _split(a)
    m = a.shape[0]
    r = jnp.dot(jnp.concatenate([hi, lo], axis=0), b.astype(BF16), preferred_element_type=F32)
    return r[:m] + r[m:]


def _sigmoid(x):
    return 1.0 / (1.0 + jnp.exp(-x))


def _silu(x):
    return x * _sigmoid(x)


def _softplus(x):
    return jnp.maximum(x, 0.0) + jnp.log1p(jnp.exp(-jnp.abs(x)))


def _rms(x, g):
    return x * lax.rsqrt(jnp.mean(x * x, axis=-1, keepdims=True) + EPS) * g


def _stack4(x):
    return jnp.concatenate([x, x, x, x], axis=0)


def _rope_kernel(pos_ref, invf_ref, sign_ref, cos_ref, sin_ref):
    ang = pos_ref[...].astype(F32) * invf_ref[...]
    cos_ref[...] = jnp.cos(ang)
    sin_ref[...] = jnp.sin(ang) * sign_ref[...]


def _rope_tables(positions):
    B, L = positions.shape
    inv_freq = ROPE_BASE ** (-jnp.linspace(0.0, 1.0, HEAD_DIM // 2, dtype=F32))
    invf = jnp.tile(inv_freq, 4)[None, :]
    sign = np.where((np.arange(128) % HEAD_DIM) < HEAD_DIM // 2, -1.0, 1.0).astype(np.float32)[None, :]
    spec = pl.BlockSpec((None, L, 128), lambda b: (b, 0, 0))
    vec = pl.BlockSpec((1, 128), lambda b: (0, 0))
    return pl.pallas_call(
        _rope_kernel,
        grid=(B,),
        in_specs=[pl.BlockSpec((None, L, 1), lambda b: (b, 0, 0)), vec, vec],
        out_specs=[spec, spec],
        out_shape=[jax.ShapeDtypeStruct((B, L, 128), F32)] * 2,
        name="rope_tables",
    )(positions[:, :, None], invf, jnp.asarray(sign))


def _s5_kernel(*refs):
    h_refs = refs[:S5_SUB]
    (g_ref, wsu_ref, krev_ref, bbig_ref, cbig_ref, pwr_ref, pwi_ref, d8_ref,
     y_ref, cr_ref, ci_ref) = refs[S5_SUB:]
    n_state = S5_GROUPS * S5_STATE
    W = GROUP_WIDTH

    @pl.when(pl.program_id(1) == 0)
    def _():
        cr_ref[...] = jnp.zeros_like(cr_ref)
        ci_ref[...] = jnp.zeros_like(ci_ref)

    g = g_ref[...]
    wsu = wsu_ref[...]
    u8 = jnp.concatenate([_dot(_rms(h_refs[j][...], g), wsu) for j in range(S5_SUB)], axis=1)
    u8b = u8.astype(BF16)
    y = jnp.concatenate(
        [jnp.dot(u8b[:, 0:(t + 1) * W], krev_ref[(S5_SUB - 1 - t) * W:S5_SUB * W, :], preferred_element_type=F32)
         for t in range(S5_SUB)], axis=1) + d8_ref[...] * u8
    inc = jnp.dot(u8b, bbig_ref[...], preferred_element_type=F32)
    xr = inc[:, :n_state]
    xi = inc[:, n_state:]

    rows = lax.broadcasted_iota(jnp.int32, (S5_ROWS, n_state), 0)
    ar0 = pwr_ref[0:1, :]
    ai0 = pwi_ref[0:1, :]
    cr = cr_ref[...]
    ci = ci_ref[...]
    first = rows == 0
    xr = xr + jnp.where(first, ar0 * cr - ai0 * ci, 0.0)
    xi = xi + jnp.where(first, ar0 * ci + ai0 * cr, 0.0)
    shift = 1
    lvl = 0
    while shift < S5_ROWS:
        ar = pwr_ref[lvl:lvl + 1, :]
        ai = pwi_ref[lvl:lvl + 1, :]
        keep = rows >= shift
        sr = jnp.where(keep, pltpu.roll(xr, shift, 0), 0.0)
        si = jnp.where(keep, pltpu.roll(xi, shift, 0), 0.0)
        xr, xi = xr + ar * sr - ai * si, xi + ar * si + ai * sr
        shift *= 2
        lvl += 1
    pr = jnp.where(first, cr, pltpu.roll(xr, 1, 0))
    pi = jnp.where(first, ci, pltpu.roll(xi, 1, 0))
    cr_ref[...] = xr[S5_ROWS - 1:S5_ROWS, :]
    ci_ref[...] = xi[S5_ROWS - 1:S5_ROWS, :]
    xprev = jnp.concatenate([pr, pi], axis=1).astype(BF16)
    y_ref[...] = y + jnp.dot(xprev, cbig_ref[...], preferred_element_type=F32)


def _s5_prepare(lam_re, lam_im, log_dt, b_re, b_im, c_re, c_im, d_skip):
    G, P, H, M = S5_GROUPS, S5_STATE, S5_CH, S5_SUB
    lr = jnp.minimum(lam_re.astype(F32), -1e-4)
    li = lam_im.astype(F32)
    dt = jnp.exp(log_dt.astype(F32))[:, None]

    def apow(t):
        mag = jnp.exp(lr * dt * t)
        ang = li * dt * t
        return mag * jnp.cos(ang), mag * jnp.sin(ang)

    ar, ai = apow(1.0)
    nr, ni = ar - 1.0, ai
    den = lr * lr + li * li
    fr = (nr * lr + ni * li) / den
    fi = (ni * lr - nr * li) / den
    bbr = fr[..., None] * b_re - fi[..., None] * b_im
    bbi = fr[..., None] * b_im + fi[..., None] * b_re
    tpow = jnp.arange(M + 1, dtype=F32)[:, None, None]
    pwr_all, pwi_all = apow(tpow)

    def embed(narrow, tile, mask):
        return (jnp.dot(narrow, tile, precision=HI) * mask).astype(BF16)

    gi_rows = np.arange(G * H) // H
    car = c_re[None] * pwr_all[:M, :, None, :] - c_im[None] * pwi_all[:M, :, None, :]
    cai = c_re[None] * pwi_all[:M, :, None, :] + c_im[None] * pwr_all[:M, :, None, :]
    kt = (jnp.einsum('tghp,gpi->tgih', car, bbr, precision=HI)
          - jnp.einsum('tghp,gpi->tgih', cai, bbi, precision=HI))
    k_narrow = kt[::-1].reshape(M * G * H, H)
    k_tile = np.tile(np.eye(H, dtype=np.float32), (1, G))
    k_mask = np.tile((gi_rows[:, None] == (np.arange(G * H) // H)[None, :]).astype(np.float32), (M, 1))
    krev = embed(k_narrow, k_tile, k_mask)

    pr = pwr_all[M - 1::-1][:M][..., None]
    pi = pwi_all[M - 1::-1][:M][..., None]
    b_narrow = jnp.stack([pr * bbr[None] - pi * bbi[None], pr * bbi[None] + pi * bbr[None]])
    b_narrow = b_narrow.transpose(1, 2, 4, 0, 3).reshape(M * G * H, 2 * P)
    cp = np.arange(2 * P)
    col_c, col_g, col_p = np.arange(2 * G * P) // (G * P), (np.arange(2 * G * P) // P) % G, np.arange(2 * G * P) % P
    b_tile = ((cp[:, None] // P == col_c[None, :]) & (cp[:, None] % P == col_p[None, :])).astype(np.float32)
    b_mask = np.tile((gi_rows[:, None] == col_g[None, :]).astype(np.float32), (M, 1))
    bbig = embed(b_narrow, b_tile, b_mask)

    qr = c_re[None] * pwr_all[1:, :, None, :] - c_im[None] * pwi_all[1:, :, None, :]
    qi = -(c_re[None] * pwi_all[1:, :, None, :] + c_im[None] * pwr_all[1:, :, None, :])
    c_narrow = jnp.stack([qr, qi]).transpose(0, 2, 4, 1, 3).reshape(2 * G * P, M * H)
    th = np.arange(M * H)
    col_t, col_g2, col_h = np.arange(M * G * H) // (G * H), (np.arange(M * G * H) // H) % G, np.arange(M * G * H) % H
    c_tile = ((th[:, None] // H == col_t[None, :]) & (th[:, None] % H == col_h[None, :])).astype(np.float32)
    c_mask = (col_g[:, None] == col_g2[None, :]).astype(np.float32)
    cbig = embed(c_narrow, c_tile, c_mask)

    n_lvl = int(math.log2(S5_ROWS))
    pws = [apow(float(M * 2 ** s)) for s in range(n_lvl)]
    pwr = jnp.stack([p[0].reshape(-1) for p in pws])
    pwi = jnp.stack([p[1].reshape(-1) for p in pws])
    pad = (-n_lvl) % 8
    pwr = jnp.pad(pwr, ((0, pad), (0, 0)))
    pwi = jnp.pad(pwi, ((0, pad), (0, 0)))
    d8 = jnp.tile(d_skip.astype(F32), M)[None, :]
    return krev, bbig, cbig, pwr, pwi, d8


def _s5_call(h, norm_g, w_su, prep):
    B, L, D = h.shape
    krev, bbig, cbig, pwr, pwi, d8 = prep
    rows = L // S5_SUB
    wide = S5_SUB * GROUP_WIDTH
    n_state = S5_GROUPS * S5_STATE
    h8 = h.reshape(B, rows, S5_SUB * D)

    def const(shape):
        return pl.BlockSpec(shape, lambda b, i: (0,) * len(shape), pipeline_mode=pl.Buffered(1))

    h_specs = [pl.BlockSpec((None, S5_ROWS, D), functools.partial(lambda b, i, j: (b, i, j), j=j))
               for j in range(S5_SUB)]
    y8 = pl.pallas_call(
        _s5_kernel,
        grid=(B, rows // S5_ROWS),
        in_specs=h_specs + [const((1, D)), const((D, GROUP_WIDTH)), const((wide, GROUP_WIDTH)),
                            const((wide, 2 * n_state)), const((2 * n_state, wide)),
                            const(pwr.shape), const(pwi.shape), const((1, wide))],
        out_specs=pl.BlockSpec((None, S5_ROWS, wide), lambda b, i: (b, i, 0)),
        out_shape=jax.ShapeDtypeStruct((B, rows, wide), F32),
        scratch_shapes=[pltpu.VMEM((1, n_state), F32), pltpu.VMEM((1, n_state), F32)],
        compiler_params=pltpu.CompilerParams(dimension_semantics=("arbitrary", "arbitrary"),
                                             vmem_limit_bytes=VMEM_LIMIT),
        name="s5_mixer",
    )(*([h8] * S5_SUB), norm_g, w_su, krev, bbig, cbig, pwr, pwi, d8)
    return y8.reshape(B, L, GROUP_WIDTH)


def _mixer_consts():
    TL, C, H = MIX_TL, CHUNK, NUM_HEADS
    t = np.arange(TL)
    ltri = ((t[:, None] // C == t[None, :] // C) & (t[None, :] <= t[:, None])).astype(np.float32)
    lane256 = np.arange(256)
    lane128 = np.arange(128)
    hm128 = (lane128[None, :] // GLA_DK == np.arange(H)[:, None]).astype(np.float32)
    hm256 = (lane256[None, :] // HEAD_DIM == np.arange(H)[:, None]).astype(np.float32)
    bm = (lane256[:, None] // HEAD_DIM == lane256[None, :] // HEAD_DIM).astype(np.float32)
    bmt = (lane256[:, None] // HEAD_DIM == lane128[None, :] // GLA_DK).astype(np.float32)
    c = np.arange(C)
    caus4 = np.tile((c[:, None] >= c[None, :]).astype(np.float32), (H, 1))
    s_lane = lane256 % HEAD_DIM
    incl = (c[:, None] >= s_lane[None, :]).astype(np.float32)
    strict = (c[:, None] > s_lane[None, :]).astype(np.float32)
    idiag = (c[:, None] == s_lane[None, :]).astype(np.float32)
    log_gamma = np.log1p(-(2.0 ** (-5.0 - np.arange(H, dtype=np.float32)))).astype(np.float32)
    idx = np.arange(TL, dtype=np.float32)
    rel = idx[:, None] - idx[None, :]
    dmask = np.where(rel >= 0, np.exp(np.maximum(rel, 0.0)[None] * log_gamma[:, None, None]), 0.0)
    dmask = dmask.reshape(H * TL, TL).astype(np.float32)
    lg_lane = log_gamma[lane256 // HEAD_DIM]
    xi = np.exp((idx[:, None] + 1.0) * lg_lane[None, :]).astype(np.float32)
    zeta = np.exp((TL - 1.0 - idx[:, None]) * lg_lane[None, :]).astype(np.float32)
    gchunk = np.exp(TL * lg_lane)[None, :].astype(np.float32)
    e_beta = np.zeros((128, 256), np.float32)
    e_a = np.zeros((128, 256), np.float32)
    for h in range(H):
        e_beta[SM_BETA + h, h * HEAD_DIM:(h + 1) * HEAD_DIM] = 1.0
        e_a[SM_A + h, h * HEAD_DIM:(h + 1) * HEAD_DIM] = 1.0
    return dict(ltri=ltri, hm128=hm128, hm256=hm256, bm=bm, bmt=bmt, caus4=caus4, incl=incl, strict=strict,
                idiag=idiag, dmask=dmask, xi=xi, zeta=zeta, gchunk=gchunk, e_beta=e_beta, e_a=e_a,
                m64=bm / HEAD_DIM)


_CONST_ORDER = ("ltri", "hm128", "hm256", "bm", "bmt", "caus4", "incl", "strict", "idiag", "dmask", "xi",
                "zeta", "gchunk", "e_beta", "e_a", "m64")
_PARAM_ORDER = ("norm_mix", "w_z", "w_out", "w_a2", "b_a", "gla_g", "ret_g", "gdn_g", "conv_w", "alog_v",
                "dtb_v", "w_glu", "b_glu")


def _mixer_kernel(*refs):
    n_c, n_p = len(_CONST_ORDER), len(_PARAM_ORDER)
    h_ref, ys5_ref, cos_ref, sin_ref = refs[:4]
    cst = dict(zip(_CONST_ORDER, refs[4:4 + n_c]))
    prm = dict(zip(_PARAM_ORDER, refs[4 + n_c:4 + n_c + n_p]))
    o_ref = refs[4 + n_c + n_p]
    st_gla, s_ret, s_gdn, xpad = refs[4 + n_c + n_p + 1:]
    TL, C, H = MIX_TL, CHUNK, NUM_HEADS

    @pl.when(pl.program_id(1) == 0)
    def _():
        st_gla[...] = jnp.zeros_like(st_gla)
        s_ret[...] = jnp.zeros_like(s_ret)
        s_gdn[...] = jnp.zeros_like(s_gdn)
        xpad[0:8, :] = jnp.zeros((8, 3 * GROUP_WIDTH), F32)

    h = h_ref[...]
    hn = _rms(h, prm["norm_mix"][...]).astype(BF16)
    w_z = prm["w_z"]

    def proj(lo, hi):
        return jnp.dot(hn, w_z[:, lo:hi], preferred_element_type=F32)

    hm128 = cst["hm128"][...]
    hm256 = cst["hm256"][...]
    bm = cst["bm"][...]
    m64 = cst["m64"][...]
    ltri = cst["ltri"][...]
    sm = proj(Z_SMALL, Z_WIDTH)

    def take_heads(stacked, rows):
        out = stacked[0:rows] * hm256[0:1]
        for hh in range(1, H):
            out = out + stacked[hh * rows:(hh + 1) * rows] * hm256[hh:hh + 1]
        return out

    zg = proj(Z_GLA, Z_GLA + 768)
    q, k, v, r = zg[:, 0:128], zg[:, 128:256], zg[:, 256:512], zg[:, 512:768]
    gk = _dot(sm, prm["w_a2"][...]) + prm["b_a"][...]
    gk = (jnp.minimum(gk, 0.0) - jnp.log1p(jnp.exp(-jnp.abs(gk)))) / GLA_GATE_NORM
    bcum = _dot_exact_lhs(ltri, gk)
    caus4 = cst["caus4"][...]
    bmt = cst["bmt"][...]
    st = st_gla[...]
    o_chunks = []
    for c in range(TL // C):
        sl = slice(c * C, (c + 1) * C)
        b_c = bcum[sl]
        b_last = b_c[C - 1:C, :]
        q_t = q[sl] * jnp.exp(b_c) * (GLA_DK ** -0.5)
        k_t = k[sl] * jnp.exp(-b_c)
        k_end = k[sl] * jnp.exp(b_last - b_c)
        v_c = v[sl]
        qs = jnp.concatenate([q_t * hm128[hh:hh + 1] for hh in range(H)], axis=0)
        att = _dot_nt(qs, k_t) * caus4
        o_c = take_heads(_dot(att, v_c), C) + _dot_nt(q_t, st)
        st = st * jnp.exp(b_last) + bmt * _dot(v_c.T, k_end)
        o_chunks.append(o_c)
    st_gla[...] = st
    o_a = jnp.concatenate(o_chunks, axis=0)
    o_a = o_a * lax.rsqrt(_dot(o_a * o_a, m64) + EPS) * prm["gla_g"][...] * _silu(r)

    zr = proj(Z_RET, Z_RET + 1024)
    cos2 = jnp.concatenate([cos_ref[...], cos_ref[...]], axis=1)
    sin2 = jnp.concatenate([sin_ref[...], sin_ref[...]], axis=1)
    first_half = (lax.broadcasted_iota(jnp.int32, (TL, 256), 1) % HEAD_DIM) < HEAD_DIM // 2

    def rope(x):
        rot = jnp.where(first_half, pltpu.roll(x, 256 - HEAD_DIM // 2, 1), pltpu.roll(x, HEAD_DIM // 2, 1))
        return x * cos2 + rot * sin2

    rq = rope(zr[:, 0:256])
    rk = rope(zr[:, 256:512]) * (HEAD_DIM ** -0.5)
    rv = zr[:, 512:768]
    rg = zr[:, 768:1024]
    qs = jnp.concatenate([rq * hm256[hh:hh + 1] for hh in range(H)], axis=0)
    sc = _dot_nt(qs, rk) * cst["dmask"][...]
    s_prev = s_ret[...]
    o_c = take_heads(_dot(sc, rv), TL) + _dot(rq, s_prev) * cst["xi"][...]
    s_ret[...] = s_prev * cst["gchunk"][...] + bm * _dot((rk * cst["zeta"][...]).T, rv)
    o_c = o_c - _dot_exact_rhs(o_c, m64)
    o_c = o_c * lax.rsqrt(_dot(o_c * o_c, m64) + EPS) * prm["ret_g"][...] * _silu(rg)

    zd = proj(Z_GDN, Z_GDN + 1024)
    xpad[8:8 + TL, :] = zd[:, 0:768]
    cw = prm["conv_w"][...]
    xc = cw[GDN_CONV - 1:GDN_CONV, :] * zd[:, 0:768]
    for j in range(GDN_CONV - 1):
        xc = xc + cw[j:j + 1, :] * xpad[5 + j:5 + j + TL, :]
    xpad[0:8, :] = zd[TL - 8:TL, 0:768]
    xc = _silu(xc)
    dq, dk, dv = xc[:, 0:256], xc[:, 256:512], xc[:, 512:768]
    ones64 = bm
    dq = dq * lax.rsqrt(_dot(dq * dq, ones64) + EPS) * (HEAD_DIM ** -0.5)
    dk = dk * lax.rsqrt(_dot(dk * dk, ones64) + EPS)
    beta = _dot_exact_rhs(_sigmoid(sm), cst["e_beta"][...])
    g_log = _dot_exact_rhs(-jnp.exp(prm["alog_v"][...]) * _softplus(sm + prm["dtb_v"][...]), cst["e_a"][...])
    gcum = _dot_exact_lhs(ltri, g_log)
    incl = cst["incl"][...]
    strict = cst["strict"][...]
    idiag = cst["idiag"][...]
    sg = s_gdn[...]
    o_chunks = []
    for c in range(TL // C):
        sl = slice(c * C, (c + 1) * C)
        gc = gcum[sl]
        g_last = gc[C - 1:C, :]
        k_c, q_c, v_c, b_c = dk[sl], dq[sl], dv[sl], beta[sl]
        g_row = jnp.sum(gc * idiag, axis=0, keepdims=True)
        diff = gc - g_row
        dec = jnp.where(incl > 0.0, jnp.exp(jnp.where(incl > 0.0, diff, 0.0)), 0.0)
        kq = _dot_nt(jnp.concatenate([k_c, q_c], axis=0), _stack4(k_c) * bm)
        a_mat = kq[0:C] * dec * strict * b_c
        aqk = kq[C:2 * C] * dec
        p_mat = -a_mat
        t_mat = idiag + p_mat
        p_mat = _dot(p_mat, _stack4(p_mat) * bm)
        for _ in range(int(math.log2(C)) - 2):
            pt = _dot(jnp.concatenate([p_mat, t_mat], axis=0), _stack4(p_mat) * bm)
            p_mat = pt[0:C]
            t_mat = t_mat + pt[C:2 * C]
        t_mat = t_mat + _dot(t_mat, _stack4(p_mat) * bm)
        eg = jnp.exp(gc)
        u_c = _dot(t_mat, _stack4(v_c * b_c) * bm)
        w_c = _dot(t_mat, _stack4(k_c * b_c * eg) * bm)
        rr = _dot(jnp.concatenate([w_c, q_c * eg], axis=0), sg)
        v_new = u_c - rr[0:C]
        o_chunks.append(rr[C:2 * C] + _dot(aqk, _stack4(v_new) * bm))
        k_end = k_c * jnp.exp(g_last - gc)
        sg = sg * jnp.exp(g_last) + bm * _dot(k_end.T, v_new)
    s_gdn[...] = sg
    o_d = jnp.concatenate(o_chunks, axis=0)
    o_d = o_d * lax.rsqrt(_dot(o_d * o_d, m64) + EPS) * prm["gdn_g"][...] * _silu(zd[:, 768:1024])

    y = ys5_ref[...]
    y = 0.5 * y * (1.0 + jnp.tanh(math.sqrt(2.0 / math.pi) * (y + 0.044715 * (y * y * y))))
    o_b = y * _sigmoid(_dot(y, prm["w_glu"][...]) + prm["b_glu"][...])

    mix = jnp.concatenate([o_a, o_b, o_c, o_d], axis=1).astype(BF16)
    o_ref[...] = h + jnp.dot(mix, prm["w_out"][...], preferred_element_type=F32)


def _mixer_call(h, ys5, cos_t, sin_t, consts, params):
    B, L, D = h.shape
    TL = MIX_TL

    def const(arr):
        nd = arr.ndim
        return pl.BlockSpec(arr.shape, lambda b, i: (0,) * nd, pipeline_mode=pl.Buffered(1))

    def tile(width):
        return pl.BlockSpec((None, TL, width), lambda b, i: (b, i, 0))

    c_list = [consts[n] for n in _CONST_ORDER]
    p_list = [params[n] for n in _PARAM_ORDER]
    return pl.pallas_call(
        _mixer_kernel,
        grid=(B, L // TL),
        in_specs=[tile(D), tile(GROUP_WIDTH), tile(128), tile(128)] + [const(a) for a in c_list + p_list],
        out_specs=tile(D),
        out_shape=jax.ShapeDtypeStruct((B, L, D), F32),
        scratch_shapes=[pltpu.VMEM((256, 128), F32), pltpu.VMEM((256, 256), F32), pltpu.VMEM((256, 256), F32),
                        pltpu.VMEM((TL + 8, 3 * GROUP_WIDTH), F32)],
        compiler_params=pltpu.CompilerParams(dimension_semantics=("arbitrary", "arbitrary"),
                                             vmem_limit_bytes=VMEM_LIMIT),
        name="mixer",
    )(h, ys5, cos_t, sin_t, *c_list, *p_list)


def _ffn_kernel(h_ref, p_ref, gf_ref, wup_ref, wdn_ref, gp_ref, wg_ref, wp_ref, gl_ref, o_ref, acc_ref, *,
                final):
    h = h_ref[...]
    hn = _rms(h, gf_ref[...]).astype(BF16)
    for s in range(FFN_HIDDEN // FFN_SLAB):
        lo = s * FFN_SLAB
        g = jnp.dot(hn, wup_ref[:, lo:lo + FFN_SLAB], preferred_element_type=F32)
        u = jnp.dot(hn, wup_ref[:, FFN_HIDDEN + lo:FFN_HIDDEN + lo + FFN_SLAB], preferred_element_type=F32)
        part = jnp.dot((_silu(g) * u).astype(BF16), wdn_ref[lo:lo + FFN_SLAB, :], preferred_element_type=F32)
        if s == 0:
            acc_ref[...] = part
        else:
            acc_ref[...] += part
    h2 = h + acc_ref[...]
    gate = _sigmoid(jnp.dot(_rms(h2, gp_ref[...]).astype(BF16), wg_ref[...], preferred_element_type=F32))
    h3 = h2 + jnp.dot(p_ref[...].astype(BF16), wp_ref[...], preferred_element_type=F32) * gate
    if final:
        h3 = _rms(h3, gl_ref[...])
    o_ref[...] = h3


def _ffn_call(h2d, p2d, g_ffn, w_up, w_down, g_ple, w_gate, w_proj, g_final, final):
    T, D = h2d.shape
    TM = FFN_TM

    def const(arr):
        nd = arr.ndim
        return pl.BlockSpec(arr.shape, lambda i: (0,) * nd, pipeline_mode=pl.Buffered(1))

    weights = [g_ffn, w_up, w_down, g_ple, w_gate, w_proj, g_final]
    return pl.pallas_call(
        functools.partial(_ffn_kernel, final=final),
        grid=(T // TM,),
        in_specs=[pl.BlockSpec((TM, D), lambda i: (i, 0)), pl.BlockSpec((TM, PLE_DIM), lambda i: (i, 0))]
        + [const(w) for w in weights],
        out_specs=pl.BlockSpec((TM, D), lambda i: (i, 0)),
        out_shape=jax.ShapeDtypeStruct((T, D), F32),
        scratch_shapes=[pltpu.VMEM((TM, D), F32)],
        compiler_params=pltpu.CompilerParams(dimension_semantics=("arbitrary",), vmem_limit_bytes=VMEM_LIMIT),
        name="ffn_ple",
    )(h2d, p2d, *weights)


def _mixer_params(i, norm_mix, w_in, w_out, gla_w_a2, gla_b_a, gla_norm, ret_norm, gdn_conv, gdn_a_log,
                  gdn_dt_bias, gdn_norm, s5_w_glu, s5_b_glu):
    w = w_in[i]

    def cols(seg):
        return w[:, seg[0]:seg[1]]

    small = jnp.concatenate([cols(_ALOW), cols(_DB), cols(_DA),
                             jnp.zeros((D_MODEL, 128 - GLA_RANK - 2 * NUM_HEADS), F32)], axis=1)
    w_z = jnp.concatenate([cols(_AQ), cols(_AK), cols(_AV), cols(_AR), cols(_RQ), cols(_RK), cols(_RV),
                           cols(_RG), cols(_DQ), cols(_DK), cols(_DV), cols(_DG), small], axis=1).astype(BF16)
    w_a2 = jnp.zeros((128, 128), F32).at[0:GLA_RANK, :].set(gla_w_a2[i])
    alog_v = jnp.zeros((1, 128), F32).at[0, SM_A:SM_A + NUM_HEADS].set(gdn_a_log[i])
    dtb_v = jnp.zeros((1, 128), F32).at[0, SM_A:SM_A + NUM_HEADS].set(gdn_dt_bias[i])
    return dict(norm_mix=norm_mix[i][None, :], w_z=w_z, w_out=w_out[i].astype(BF16), w_a2=w_a2,
                b_a=gla_b_a[i][None, :], gla_g=jnp.tile(gla_norm[i], NUM_HEADS)[None, :],
                ret_g=ret_norm[i][None, :], gdn_g=jnp.tile(gdn_norm[i], NUM_HEADS)[None, :],
                conv_w=gdn_conv[i], alog_v=alog_v, dtb_v=dtb_v, w_glu=s5_w_glu[i].astype(BF16),
                b_glu=s5_b_glu[i][None, :])


def kernel(x, p, positions, norm_mix, w_in, w_out, gla_w_a2, gla_b_a, gla_norm, s5_lam_re, s5_lam_im, s5_log_dt,
           s5_b_re, s5_b_im, s5_c_re, s5_c_im, s5_d, s5_w_glu, s5_b_glu, ret_norm, gdn_conv, gdn_a_log,
           gdn_dt_bias, gdn_norm, norm_ffn, w_ffn_up, w_ffn_down, norm_ple, w_ple_gate, w_ple_proj, norm_final):
    B, L, D = x.shape
    depth = w_in.shape[0]
    assert D == D_MODEL and L % max(MIX_TL, S5_SUB * S5_ROWS) == 0 and (B * L) % FFN_TM == 0
    consts = {k: jnp.asarray(v) for k, v in _mixer_consts().items()}
    cos_t, sin_t = _rope_tables(positions)
    h = x
    for i in range(depth):
        prep = _s5_prepare(s5_lam_re[i], s5_lam_im[i], s5_log_dt[i], s5_b_re[i], s5_b_im[i], s5_c_re[i],
                           s5_c_im[i], s5_d[i])
        w_su = w_in[i][:, _SU[0]:_SU[1]].astype(BF16)
        ys5 = _s5_call(h, norm_mix[i][None, :], w_su, prep)
        params = _mixer_params(i, norm_mix, w_in, w_out, gla_w_a2, gla_b_a, gla_norm, ret_norm, gdn_conv,
                               gdn_a_log, gdn_dt_bias, gdn_norm, s5_w_glu, s5_b_glu)
        h = _mixer_call(h, ys5, cos_t, sin_t, consts, params)
        h2 = _ffn_call(h.reshape(B * L, D), p[i].reshape(B * L, PLE_DIM), norm_ffn[i][None, :],
                       w_ffn_up[i].astype(BF16), w_ffn_down[i].astype(BF16), norm_ple[i][None, :],
                       w_ple_gate[i].astype(BF16), w_ple_proj[i].astype(BF16), norm_final[None, :],
                       final=(i == depth - 1))
        h = h2.reshape(B, L, D)
    return h
```

```python
import functools
import math

import numpy as np
import jax
import jax.numpy as jnp
from jax import lax
from jax.experimental import pallas as pl
from jax.experimental.pallas import tpu as pltpu

F32 = jnp.float32
BF16 = jnp.bfloat16
HI = lax.Precision.HIGHEST

D_MODEL = 1024
PLE_DIM = 256
GROUP_WIDTH = 256
NUM_HEADS = 4
CHUNK = 64
EPS = 1e-6

GLA_DK = 32
GLA_RANK = 16
GLA_GATE_NORM = 16.0
S5_CH = 16
S5_GROUPS = 16
S5_STATE = 64
S5_SUB = 8
HEAD_DIM = 64
ROPE_BASE = 10000.0
GDN_CONV = 4
FFN_HIDDEN = 2816

MIX_TL = 256
S5_ROWS = 128
FFN_TM = 512
FFN_SLAB = 256

VMEM_LIMIT = 56 * 1024 * 1024

_IN_OFFS = np.cumsum([0, 128, 128, 256, 16, 256, 256, 256, 256, 256, 256, 256, 256, 256, 4, 4, 256])
(_AQ, _AK, _AV, _ALOW, _AR, _SU, _RQ, _RK, _RV, _RG, _DQ, _DK, _DV, _DB, _DA, _DG) = [
    (int(_IN_OFFS[i]), int(_IN_OFFS[i + 1])) for i in range(16)]
Z_GLA = 0
Z_RET = 768
Z_GDN = 1792
Z_SMALL = 2816
Z_WIDTH = 2944
SM_BETA = 16
SM_A = 20


def _dot(a, b):
    return jnp.dot(a.astype(BF16), b.astype(BF16), preferred_element_type=F32)


def _dot_nt(a, b):
    return lax.dot_general(a.astype(BF16), b.astype(BF16), (((1,), (1,)), ((), ())),
                           preferred_element_type=F32)


def _split(x):
    hi = x.astype(BF16)
    return hi, (x - hi.astype(F32)).astype(BF16)


def _dot_exact_lhs(a, b):
    hi, lo = _split(b)
    n = b.shape[1]
    if n <= 128:
        r = jnp.dot(a.astype(BF16), jnp.concatenate([hi, lo], axis=1), preferred_element_type=F32)
        return r[:, :n] + r[:, n:]
    return (jnp.dot(a.astype(BF16), hi, preferred_element_type=F32)
            + jnp.dot(a.astype(BF16), lo, preferred_element_type=F32))


def _dot_exact_rhs(a, b):
    hi, lo = _split(a)
    m = a.shape[0]
    r = jnp.dot(jnp.concatenate([hi, lo], axis=0), b.astype(BF16), preferred_element_type=F32)
    return r[:m] + r[m:]


def _sigmoid(x):
    return 1.0 / (1.0 + jnp.exp(-x))


def _silu(x):
    return x * _sigmoid(x)


def _softplus(x):
    return jnp.maximum(x, 0.0) + jnp.log1p(jnp.exp(-jnp.abs(x)))


def _rms(x, g):
    return x * lax.rsqrt(jnp.mean(x * x, axis=-1, keepdims=True) + EPS) * g


def _stack4(x):
    return jnp.concatenate([x, x, x, x], axis=0)


def _rope_kernel(pos_ref, invf_ref, sign_ref, cos_ref, sin_ref):
    ang = pos_ref[...].astype(F32) * invf_ref[...]
    cos_ref[...] = jnp.cos(ang)
    sin_ref[...] = jnp.sin(ang) * sign_ref[...]


def _rope_tables(positions):
    B, L = positions.shape
    inv_freq = ROPE_BASE ** (-jnp.linspace(0.0, 1.0, HEAD_DIM // 2, dtype=F32))
    invf = jnp.tile(inv_freq, 4)[None, :]
    sign = np.where((np.arange(128) % HEAD_DIM) < HEAD_DIM // 2, -1.0, 1.0).astype(np.float32)[None, :]
    spec = pl.BlockSpec((None, L, 128), lambda b: (b, 0, 0))
    vec = pl.BlockSpec((1, 128), lambda b: (0, 0))
    return pl.pallas_call(
        _rope_kernel,
        grid=(B,),
        in_specs=[pl.BlockSpec((None, L, 1), lambda b: (b, 0, 0)), vec, vec],
        out_specs=[spec, spec],
        out_shape=[jax.ShapeDtypeStruct((B, L, 128), F32)] * 2,
        name="rope_tables",
    )(positions[:, :, None], invf, jnp.asarray(sign))


def _s5_kernel(h_ref, g_ref, wsu_ref, krev_ref, bbig_ref, cbig_ref, pwr_ref, pwi_ref, d8_ref,
               y_ref, cr_ref, ci_ref, su_scr, y_scr):
    n_state = S5_GROUPS * S5_STATE
    W = GROUP_WIDTH
    n_slab = W // 128

    @pl.when(pl.program_id(1) == 0)
    def _():
        cr_ref[...] = jnp.zeros_like(cr_ref)
        ci_ref[...] = jnp.zeros_like(ci_ref)

    su = _dot(_rms(h_ref[...], g_ref[...]), wsu_ref[...])
    for s in range(n_slab):
        su_scr[s] = su[:, s * 128:(s + 1) * 128]
    u8 = jnp.concatenate([su_scr[s, pl.ds(j, S5_ROWS, stride=S5_SUB), :]
                          for j in range(S5_SUB) for s in range(n_slab)], axis=1)
    u8b = u8.astype(BF16)
    y = jnp.concatenate(
        [jnp.dot(u8b[:, 0:(t + 1) * W], krev_ref[(S5_SUB - 1 - t) * W:S5_SUB * W, :], preferred_element_type=F32)
         for t in range(S5_SUB)], axis=1) + d8_ref[...] * u8
    inc = jnp.dot(u8b, bbig_ref[...], preferred_element_type=F32)
    xr = inc[:, :n_state]
    xi = inc[:, n_state:]

    rows = lax.broadcasted_iota(jnp.int32, (S5_ROWS, n_state), 0)
    ar0 = pwr_ref[0:1, :]
    ai0 = pwi_ref[0:1, :]
    cr = cr_ref[...]
    ci = ci_ref[...]
    first = rows == 0
    xr = xr + jnp.where(first, ar0 * cr - ai0 * ci, 0.0)
    xi = xi + jnp.where(first, ar0 * ci + ai0 * cr, 0.0)
    shift = 1
    lvl = 0
    while shift < S5_ROWS:
        ar = pwr_ref[lvl:lvl + 1, :]
        ai = pwi_ref[lvl:lvl + 1, :]
        keep = rows >= shift
        sr = jnp.where(keep, pltpu.roll(xr, shift, 0), 0.0)
        si = jnp.where(keep, pltpu.roll(xi, shift, 0), 0.0)
        xr, xi = xr + ar * sr - ai * si, xi + ar * si + ai * sr
        shift *= 2
        lvl += 1
    pr = jnp.where(first, cr, pltpu.roll(xr, 1, 0))
    pi = jnp.where(first, ci, pltpu.roll(xi, 1, 0))
    cr_ref[...] = xr[S5_ROWS - 1:S5_ROWS, :]
    ci_ref[...] = xi[S5_ROWS - 1:S5_ROWS, :]
    xprev = jnp.concatenate([pr, pi], axis=1).astype(BF16)
    y = y + jnp.dot(xprev, cbig_ref[...], preferred_element_type=F32)
    for j in range(S5_SUB):
        for s in range(n_slab):
            y_scr[s, pl.ds(j, S5_ROWS, stride=S5_SUB), :] = y[:, j * W + s * 128:j * W + (s + 1) * 128]
    y_ref[...] = jnp.concatenate([y_scr[s] for s in range(n_slab)], axis=1)


def _s5_prepare(lam_re, lam_im, log_dt, b_re, b_im, c_re, c_im, d_skip):
    G, P, H, M = S5_GROUPS, S5_STATE, S5_CH, S5_SUB
    lr = jnp.minimum(lam_re.astype(F32), -1e-4)
    li = lam_im.astype(F32)
    dt = jnp.exp(log_dt.astype(F32))[:, None]

    def apow(t):
        mag = jnp.exp(lr * dt * t)
        ang = li * dt * t
        return mag * jnp.cos(ang), mag * jnp.sin(ang)

    ar, ai = apow(1.0)
    nr, ni = ar - 1.0, ai
    den = lr * lr + li * li
    fr = (nr * lr + ni * li) / den
    fi = (ni * lr - nr * li) / den
    bbr = fr[..., None] * b_re - fi[..., None] * b_im
    bbi = fr[..., None] * b_im + fi[..., None] * b_re
    tpow = jnp.arange(M + 1, dtype=F32)[:, None, None]
    pwr_all, pwi_all = apow(tpow)

    def embed(narrow, tile, mask):
        return (jnp.dot(narrow, tile, precision=HI) * mask).astype(BF16)

    gi_rows = np.arange(G * H) // H
    car = c_re[None] * pwr_all[:M, :, None, :] - c_im[None] * pwi_all[:M, :, None, :]
    cai = c_re[None] * pwi_all[:M, :, None, :] + c_im[None] * pwr_all[:M, :, None, :]
    kt = (jnp.einsum('tghp,gpi->tgih', car, bbr, precision=HI)
          - jnp.einsum('tghp,gpi->tgih', cai, bbi, precision=HI))
    k_narrow = kt[::-1].reshape(M * G * H, H)
    k_tile = np.tile(np.eye(H, dtype=np.float32), (1, G))
    k_mask = np.tile((gi_rows[:, None] == (np.arange(G * H) // H)[None, :]).astype(np.float32), (M, 1))
    krev = embed(k_narrow, k_tile, k_mask)

    pr = pwr_all[M - 1::-1][:M][..., None]
    pi = pwi_all[M - 1::-1][:M][..., None]
    b_narrow = jnp.stack([pr * bbr[None] - pi * bbi[None], pr * bbi[None] + pi * bbr[None]])
    b_narrow = b_narrow.transpose(1, 2, 4, 0, 3).reshape(M * G * H, 2 * P)
    cp = np.arange(2 * P)
    col_c, col_g, col_p = np.arange(2 * G * P) // (G * P), (np.arange(2 * G * P) // P) % G, np.arange(2 * G * P) % P
    b_tile = ((cp[:, None] // P == col_c[None, :]) & (cp[:, None] % P == col_p[None, :])).astype(np.float32)
    b_mask = np.tile((gi_rows[:, None] == col_g[None, :]).astype(np.float32), (M, 1))
    bbig = embed(b_narrow, b_tile, b_mask)

    qr = c_re[None] * pwr_all[1:, :, None, :] - c_im[None] * pwi_all[1:, :, None, :]
    qi = -(c_re[None] * pwi_all[1:, :, None, :] + c_im[None] * pwr_all[1:, :, None, :])
    c_narrow = jnp.stack([qr, qi]).transpose(0, 2, 4, 1, 3).reshape(2 * G * P, M * H)
    th = np.arange(M * H)
    col_t, col_g2, col_h = np.arange(M * G * H) // (G * H), (np.arange(M * G * H) // H) % G, np.arange(M * G * H) % H
    c_tile = ((th[:, None] // H == col_t[None, :]) & (th[:, None] % H == col_h[None, :])).astype(np.float32)
    c_mask = (col_g[:, None] == col_g2[None, :]).astype(np.float32)
    cbig = embed(c_narrow, c_tile, c_mask)

    n_lvl = int(math.log2(S5_ROWS))
    pws = [apow(float(M * 2 ** s)) for s in range(n_lvl)]
    pwr = jnp.stack([p[0].reshape(-1) for p in pws])
    pwi = jnp.stack([p[1].reshape(-1) for p in pws])
    pad = (-n_lvl) % 8
    pwr = jnp.pad(pwr, ((0, pad), (0, 0)))
    pwi = jnp.pad(pwi, ((0, pad), (0, 0)))
    d8 = jnp.tile(d_skip.astype(F32), M)[None, :]
    return krev, bbig, cbig, pwr, pwi, d8


def _s5_call(h, norm_g, w_su, prep):
    B, L, D = h.shape
    krev, bbig, cbig, pwr, pwi, d8 = prep
    tokens = S5_SUB * S5_ROWS
    wide = S5_SUB * GROUP_WIDTH
    n_state = S5_GROUPS * S5_STATE
    n_slab = GROUP_WIDTH // 128

    def const(shape):
        return pl.BlockSpec(shape, lambda b, i: (0,) * len(shape), pipeline_mode=pl.Buffered(1))

    return pl.pallas_call(
        _s5_kernel,
        grid=(B, L // tokens),
        in_specs=[pl.BlockSpec((None, tokens, D), lambda b, i: (b, i, 0)),
                  const((1, D)), const((D, GROUP_WIDTH)), const((wide, GROUP_WIDTH)),
                  const((wide, 2 * n_state)), const((2 * n_state, wide)),
                  const(pwr.shape), const(pwi.shape), const((1, wide))],
        out_specs=pl.BlockSpec((None, tokens, GROUP_WIDTH), lambda b, i: (b, i, 0)),
        out_shape=jax.ShapeDtypeStruct((B, L, GROUP_WIDTH), F32),
        scratch_shapes=[pltpu.VMEM((1, n_state), F32), pltpu.VMEM((1, n_state), F32),
                        pltpu.VMEM((n_slab, tokens, 128), F32), pltpu.VMEM((n_slab, tokens, 128), F32)],
        compiler_params=pltpu.CompilerParams(dimension_semantics=("arbitrary", "arbitrary"),
                                             vmem_limit_bytes=VMEM_LIMIT),
        name="s5_mixer",
    )(h, norm_g, w_su, krev, bbig, cbig, pwr, pwi, d8)


def _mixer_consts():
    TL, C, H = MIX_TL, CHUNK, NUM_HEADS
    t = np.arange(TL)
    ltri = ((t[:, None] // C == t[None, :] // C) & (t[None, :] <= t[:, None])).astype(np.float32)
    lane256 = np.arange(256)
    lane128 = np.arange(128)
    hm128 = (lane128[None, :] // GLA_DK == np.arange(H)[:, None]).astype(np.float32)
    hm256 = (lane256[None, :] // HEAD_DIM == np.arange(H)[:, None]).astype(np.float32)
    bm = (lane256[:, None] // HEAD_DIM == lane256[None, :] // HEAD_DIM).astype(np.float32)
    bmt = (lane256[:, None] // HEAD_DIM == lane128[None, :] // GLA_DK).astype(np.float32)
    c = np.arange(C)
    caus4 = np.tile((c[:, None] >= c[None, :]).astype(np.float32), (H, 1))
    s_lane = lane128 % HEAD_DIM
    bm2 = bm[0:128, 0:128]
    incl = (c[:, None] >= s_lane[None, :]).astype(np.float32)
    strict = (c[:, None] > s_lane[None, :]).astype(np.float32)
    idiag = (c[:, None] == s_lane[None, :]).astype(np.float32)
    log_gamma = np.log1p(-(2.0 ** (-5.0 - np.arange(H, dtype=np.float32)))).astype(np.float32)
    idx = np.arange(TL, dtype=np.float32)
    rel = idx[:, None] - idx[None, :]
    dmask = np.where(rel >= 0, np.exp(np.maximum(rel, 0.0)[None] * log_gamma[:, None, None]), 0.0)
    dmask = dmask.reshape(H * TL, TL).astype(np.float32)
    lg_lane = log_gamma[lane256 // HEAD_DIM]
    xi = np.exp((idx[:, None] + 1.0) * lg_lane[None, :]).astype(np.float32)
    zeta = np.exp((TL - 1.0 - idx[:, None]) * lg_lane[None, :]).astype(np.float32)
    gchunk = np.exp(TL * lg_lane)[None, :].astype(np.float32)
    e_beta = np.zeros((128, 256), np.float32)
    e_a = np.zeros((128, 256), np.float32)
    for h in range(H):
        e_beta[SM_BETA + h, h * HEAD_DIM:(h + 1) * HEAD_DIM] = 1.0
        e_a[SM_A + h, h * HEAD_DIM:(h + 1) * HEAD_DIM] = 1.0
    return dict(ltri=ltri, hm128=hm128, hm256=hm256, bm=bm, bmt=bmt, caus4=caus4, incl=incl, strict=strict,
                idiag=idiag, bm2=bm2, dmask=dmask, xi=xi, zeta=zeta, gchunk=gchunk, e_beta=e_beta, e_a=e_a,
                m64=bm / HEAD_DIM)


_CONST_ORDER = ("ltri", "hm128", "hm256", "bm", "bmt", "caus4", "incl", "strict", "idiag", "bm2", "dmask", "xi",
                "zeta", "gchunk", "e_beta", "e_a", "m64")
_PARAM_ORDER = ("norm_mix", "w_z", "w_out", "w_a2", "b_a", "gla_g", "ret_g", "gdn_g", "conv_w", "alog_v",
                "dtb_v", "w_glu", "b_glu")


def _mixer_kernel(*refs):
    n_c, n_p = len(_CONST_ORDER), len(_PARAM_ORDER)
    h_ref, ys5_ref, cos_ref, sin_ref = refs[:4]
    cst = dict(zip(_CONST_ORDER, refs[4:4 + n_c]))
    prm = dict(zip(_PARAM_ORDER, refs[4 + n_c:4 + n_c + n_p]))
    o_ref = refs[4 + n_c + n_p]
    st_gla, s_ret, s_gdn, xpad = refs[4 + n_c + n_p + 1:]
    TL, C, H = MIX_TL, CHUNK, NUM_HEADS

    @pl.when(pl.program_id(1) == 0)
    def _():
        st_gla[...] = jnp.zeros_like(st_gla)
        s_ret[...] = jnp.zeros_like(s_ret)
        s_gdn[...] = jnp.zeros_like(s_gdn)
        xpad[0:8, :] = jnp.zeros((8, 3 * GROUP_WIDTH), F32)

    h = h_ref[...]
    hn = _rms(h, prm["norm_mix"][...]).astype(BF16)
    w_z = prm["w_z"]

    def proj(lo, hi):
        return jnp.dot(hn, w_z[:, lo:hi], preferred_element_type=F32)

    hm128 = cst["hm128"][...]
    hm256 = cst["hm256"][...]
    bm = cst["bm"][...]
    m64 = cst["m64"][...]
    ltri = cst["ltri"][...]
    sm = proj(Z_SMALL, Z_WIDTH)

    def take_heads(stacked, rows):
        out = stacked[0:rows] * hm256[0:1]
        for hh in range(1, H):
            out = out + stacked[hh * rows:(hh + 1) * rows] * hm256[hh:hh + 1]
        return out

    zg = proj(Z_GLA, Z_GLA + 768)
    q, k, v, r = zg[:, 0:128], zg[:, 128:256], zg[:, 256:512], zg[:, 512:768]
    gk = _dot(sm, prm["w_a2"][...]) + prm["b_a"][...]
    gk = (jnp.minimum(gk, 0.0) - jnp.log1p(jnp.exp(-jnp.abs(gk)))) / GLA_GATE_NORM
    bcum = _dot_exact_lhs(ltri, gk)
    caus4 = cst["caus4"][...]
    bmt = cst["bmt"][...]
    st = st_gla[...]
    o_chunks = []
    for c in range(TL // C):
        sl = slice(c * C, (c + 1) * C)
        b_c = bcum[sl]
        b_last = b_c[C - 1:C, :]
        q_t = q[sl] * jnp.exp(b_c) * (GLA_DK ** -0.5)
        k_t = k[sl] * jnp.exp(-b_c)
        k_end = k[sl] * jnp.exp(b_last - b_c)
        v_c = v[sl]
        qs = jnp.concatenate([q_t * hm128[hh:hh + 1] for hh in range(H)], axis=0)
        att = _dot_nt(qs, k_t) * caus4
        o_c = take_heads(_dot(att, v_c), C) + _dot_nt(q_t, st)
        st = st * jnp.exp(b_last) + bmt * _dot(v_c.T, k_end)
        o_chunks.append(o_c)
    st_gla[...] = st
    o_a = jnp.concatenate(o_chunks, axis=0)
    o_a = o_a * lax.rsqrt(_dot(o_a * o_a, m64) + EPS) * prm["gla_g"][...] * _silu(r)

    zr = proj(Z_RET, Z_RET + 1024)
    cos2 = jnp.concatenate([cos_ref[...], cos_ref[...]], axis=1)
    sin2 = jnp.concatenate([sin_ref[...], sin_ref[...]], axis=1)
    first_half = (lax.broadcasted_iota(jnp.int32, (TL, 256), 1) % HEAD_DIM) < HEAD_DIM // 2

    def rope(x):
        rot = jnp.where(first_half, pltpu.roll(x, 256 - HEAD_DIM // 2, 1), pltpu.roll(x, HEAD_DIM // 2, 1))
        return x * cos2 + rot * sin2

    rq = rope(zr[:, 0:256])
    rk = rope(zr[:, 256:512]) * (HEAD_DIM ** -0.5)
    rv = zr[:, 512:768]
    rg = zr[:, 768:1024]
    qs = jnp.concatenate([rq * hm256[hh:hh + 1] for hh in range(H)], axis=0)
    sc = _dot_nt(qs, rk) * cst["dmask"][...]
    s_prev = s_ret[...]
    o_c = take_heads(_dot(sc, rv), TL) + _dot(rq, s_prev) * cst["xi"][...]
    s_ret[...] = s_prev * cst["gchunk"][...] + bm * _dot((rk * cst["zeta"][...]).T, rv)
    o_c = o_c - _dot_exact_rhs(o_c, m64)
    o_c = o_c * lax.rsqrt(_dot(o_c * o_c, m64) + EPS) * prm["ret_g"][...] * _silu(rg)

    zd = proj(Z_GDN, Z_GDN + 1024)
    xpad[8:8 + TL, :] = zd[:, 0:768]
    cw = prm["conv_w"][...]
    xc = cw[GDN_CONV - 1:GDN_CONV, :] * zd[:, 0:768]
    for j in range(GDN_CONV - 1):
        xc = xc + cw[j:j + 1, :] * xpad[5 + j:5 + j + TL, :]
    xpad[0:8, :] = zd[TL - 8:TL, 0:768]
    xc = _silu(xc)
    dq, dk, dv = xc[:, 0:256], xc[:, 256:512], xc[:, 512:768]
    ones64 = bm
    dq = dq * lax.rsqrt(_dot(dq * dq, ones64) + EPS) * (HEAD_DIM ** -0.5)
    dk = dk * lax.rsqrt(_dot(dk * dk, ones64) + EPS)
    beta = _dot_exact_rhs(_sigmoid(sm), cst["e_beta"][...])
    g_log = _dot_exact_rhs(-jnp.exp(prm["alog_v"][...]) * _softplus(sm + prm["dtb_v"][...]), cst["e_a"][...])
    gcum = _dot_exact_lhs(ltri, g_log)
    incl = cst["incl"][...]
    strict = cst["strict"][...]
    idiag = cst["idiag"][...]
    bm2 = cst["bm2"][...]
    bm2b = bm2.astype(BF16)
    n_pair = NUM_HEADS // 2
    probs = [(c, p) for c in range(TL // C) for p in range(n_pair)]

    def blk(x, c, p):
        return x[c * C:(c + 1) * C, p * 128:(p + 1) * 128]

    def bd2(x):
        xb = x.astype(BF16)
        return jnp.concatenate([xb, xb], axis=0) * bm2b

    def bdot(a, b):
        return jnp.dot(a.astype(BF16), b, preferred_element_type=F32)

    gk_, gq_, gv_, gb_, gg_, p_, t_, aqk_ = {}, {}, {}, {}, {}, {}, {}, {}
    for cp in probs:
        k_c, q_c, g_c = blk(dk, *cp), blk(dq, *cp), blk(gcum, *cp)
        gk_[cp], gq_[cp], gv_[cp], gb_[cp], gg_[cp] = k_c, q_c, blk(dv, *cp), blk(beta, *cp), g_c
        g_row = jnp.sum(g_c * idiag, axis=0, keepdims=True)
        dec = jnp.where(incl > 0.0, jnp.exp(jnp.where(incl > 0.0, g_c - g_row, 0.0)), 0.0)
        kq = lax.dot_general(jnp.concatenate([k_c, q_c], axis=0).astype(BF16), bd2(k_c),
                             (((1,), (1,)), ((), ())), preferred_element_type=F32)
        aqk_[cp] = kq[C:2 * C] * dec
        p_[cp] = -(kq[0:C] * dec * strict * gb_[cp])
        t_[cp] = idiag + p_[cp]
    for cp in probs:
        p_[cp] = bdot(p_[cp], bd2(p_[cp]))
    for _ in range(int(math.log2(C)) - 2):
        for cp in probs:
            pt = bdot(jnp.concatenate([p_[cp], t_[cp]], axis=0), bd2(p_[cp]))
            p_[cp] = pt[0:C]
            t_[cp] = t_[cp] + pt[C:2 * C]
    for cp in probs:
        t_[cp] = t_[cp] + bdot(t_[cp], bd2(p_[cp]))
    mm_, nn_, dl_, qp_, op_ = {}, {}, {}, {}, {}
    for cp in probs:
        k_c, g_c, b_c = gk_[cp], gg_[cp], gb_[cp]
        g_last = g_c[C - 1:C, :]
        eg = jnp.exp(g_c)
        uw = bdot(t_[cp], jnp.concatenate([bd2(gv_[cp] * b_c), bd2(k_c * b_c * eg)], axis=1))
        u_c, w_c = uw[:, 0:128], uw[:, 128:256]
        k_end_t = (k_c * jnp.exp(g_last - g_c)).T
        mn = bdot(k_end_t, jnp.concatenate([w_c, u_c], axis=1).astype(BF16))
        mm_[cp] = bm2 * mn[:, 0:128]
        nn_[cp] = bm2 * mn[:, 128:256]
        dl_[cp] = jnp.exp(g_last)
        qo = bdot(aqk_[cp], jnp.concatenate([bd2(w_c), bd2(u_c)], axis=1))
        qp_[cp] = gq_[cp] * eg - qo[:, 0:128]
        op_[cp] = qo[:, 128:256]
    o_rows = []
    sgp = [s_gdn[p] for p in range(n_pair)]
    for c in range(TL // C):
        o_pair = []
        for p in range(n_pair):
            cp = (c, p)
            s_b = sgp[p].astype(BF16)
            o_pair.append(bdot(qp_[cp], s_b) + op_[cp])
            sgp[p] = sgp[p] * dl_[cp] - bdot(mm_[cp], s_b) + nn_[cp]
        o_rows.append(jnp.concatenate(o_pair, axis=1))
    for p in range(n_pair):
        s_gdn[p] = sgp[p]
    o_d = jnp.concatenate(o_rows, axis=0)
    o_d = o_d * lax.rsqrt(_dot(o_d * o_d, m64) + EPS) * prm["gdn_g"][...] * _silu(zd[:, 768:1024])

    y = ys5_ref[...]
    y = 0.5 * y * (1.0 + jnp.tanh(math.sqrt(2.0 / math.pi) * (y + 0.044715 * (y * y * y))))
    o_b = y * _sigmoid(_dot(y, prm["w_glu"][...]) + prm["b_glu"][...])

    mix = jnp.concatenate([o_a, o_b, o_c, o_d], axis=1).astype(BF16)
    o_ref[...] = h + jnp.dot(mix, prm["w_out"][...], preferred_element_type=F32)


def _mixer_call(h, ys5, cos_t, sin_t, consts, params):
    B, L, D = h.shape
    TL = MIX_TL

    def const(arr):
        nd = arr.ndim
        return pl.BlockSpec(arr.shape, lambda b, i: (0,) * nd, pipeline_mode=pl.Buffered(1))

    def tile(width):
        return pl.BlockSpec((None, TL, width), lambda b, i: (b, i, 0))

    c_list = [consts[n] for n in _CONST_ORDER]
    p_list = [params[n] for n in _PARAM_ORDER]
    return pl.pallas_call(
        _mixer_kernel,
        grid=(B, L // TL),
        in_specs=[tile(D), tile(GROUP_WIDTH), tile(128), tile(128)] + [const(a) for a in c_list + p_list],
        out_specs=tile(D),
        out_shape=jax.ShapeDtypeStruct((B, L, D), F32),
        scratch_shapes=[pltpu.VMEM((256, 128), F32), pltpu.VMEM((256, 256), F32), pltpu.VMEM((2, 128, 128), F32),
                        pltpu.VMEM((TL + 8, 3 * GROUP_WIDTH), F32)],
        compiler_params=pltpu.CompilerParams(dimension_semantics=("arbitrary", "arbitrary"),
                                             vmem_limit_bytes=VMEM_LIMIT),
        name="mixer",
    )(h, ys5, cos_t, sin_t, *c_list, *p_list)


def _ffn_kernel(h_ref, p_ref, gf_ref, wup_ref, wdn_ref, gp_ref, wg_ref, wp_ref, gl_ref, o_ref, acc_ref, *,
                final):
    h = h_ref[...]
    hn = _rms(h, gf_ref[...]).astype(BF16)
    for s in range(FFN_HIDDEN // FFN_SLAB):
        lo = s * FFN_SLAB
        g = jnp.dot(hn, wup_ref[:, lo:lo + FFN_SLAB], preferred_element_type=F32)
        u = jnp.dot(hn, wup_ref[:, FFN_HIDDEN + lo:FFN_HIDDEN + lo + FFN_SLAB], preferred_element_type=F32)
        part = jnp.dot((_silu(g) * u).astype(BF16), wdn_ref[lo:lo + FFN_SLAB, :], preferred_element_type=F32)
        if s == 0:
            acc_ref[...] = part
        else:
            acc_ref[...] += part
    h2 = h + acc_ref[...]
    gate = _sigmoid(jnp.dot(_rms(h2, gp_ref[...]).astype(BF16), wg_ref[...], preferred_element_type=F32))
    h3 = h2 + jnp.dot(p_ref[...].astype(BF16), wp_ref[...], preferred_element_type=F32) * gate
    if final:
        h3 = _rms(h3, gl_ref[...])
    o_ref[...] = h3


def _ffn_call(h2d, p2d, g_ffn, w_up, w_down, g_ple, w_gate, w_proj, g_final, final):
    T, D = h2d.shape
    TM = FFN_TM

    def const(arr):
        nd = arr.ndim
        return pl.BlockSpec(arr.shape, lambda i: (0,) * nd, pipeline_mode=pl.Buffered(1))

    weights = [g_ffn, w_up, w_down, g_ple, w_gate, w_proj, g_final]
    return pl.pallas_call(
        functools.partial(_ffn_kernel, final=final),
        grid=(T // TM,),
        in_specs=[pl.BlockSpec((TM, D), lambda i: (i, 0)), pl.BlockSpec((TM, PLE_DIM), lambda i: (i, 0))]
        + [const(w) for w in weights],
        out_specs=pl.BlockSpec((TM, D), lambda i: (i, 0)),
        out_shape=jax.ShapeDtypeStruct((T, D), F32),
        scratch_shapes=[pltpu.VMEM((TM, D), F32)],
        compiler_params=pltpu.CompilerParams(dimension_semantics=("arbitrary",), vmem_limit_bytes=VMEM_LIMIT),
        name="ffn_ple",
    )(h2d, p2d, *weights)


def _mixer_params(i, norm_mix, w_in, w_out, gla_w_a2, gla_b_a, gla_norm, ret_norm, gdn_conv, gdn_a_log,
                  gdn_dt_bias, gdn_norm, s5_w_glu, s5_b_glu):
    w = w_in[i]

    def cols(seg):
        return w[:, seg[0]:seg[1]]

    small = jnp.concatenate([cols(_ALOW), cols(_DB), cols(_DA),
                             jnp.zeros((D_MODEL, 128 - GLA_RANK - 2 * NUM_HEADS), F32)], axis=1)
    w_z = jnp.concatenate([cols(_AQ), cols(_AK), cols(_AV), cols(_AR), cols(_RQ), cols(_RK), cols(_RV),
                           cols(_RG), cols(_DQ), cols(_DK), cols(_DV), cols(_DG), small], axis=1).astype(BF16)
    w_a2 = jnp.zeros((128, 128), F32).at[0:GLA_RANK, :].set(gla_w_a2[i])
    alog_v = jnp.zeros((1, 128), F32).at[0, SM_A:SM_A + NUM_HEADS].set(gdn_a_log[i])
    dtb_v = jnp.zeros((1, 128), F32).at[0, SM_A:SM_A + NUM_HEADS].set(gdn_dt_bias[i])
    return dict(norm_mix=norm_mix[i][None, :], w_z=w_z, w_out=w_out[i].astype(BF16), w_a2=w_a2,
                b_a=gla_b_a[i][None, :], gla_g=jnp.tile(gla_norm[i], NUM_HEADS)[None, :],
                ret_g=ret_norm[i][None, :], gdn_g=jnp.tile(gdn_norm[i], NUM_HEADS)[None, :],
                conv_w=gdn_conv[i], alog_v=alog_v, dtb_v=dtb_v, w_glu=s5_w_glu[i].astype(BF16),
                b_glu=s5_b_glu[i][None, :])


def kernel(x, p, positions, norm_mix, w_in, w_out, gla_w_a2, gla_b_a, gla_norm, s5_lam_re, s5_lam_im, s5_log_dt,
           s5_b_re, s5_b_im, s5_c_re, s5_c_im, s5_d, s5_w_glu, s5_b_glu, ret_norm, gdn_conv, gdn_a_log,
           gdn_dt_bias, gdn_norm, norm_ffn, w_ffn_up, w_ffn_down, norm_ple, w_ple_gate, w_ple_proj, norm_final):
    B, L, D = x.shape
    depth = w_in.shape[0]
    assert D == D_MODEL and L % max(MIX_TL, S5_SUB * S5_ROWS) == 0 and (B * L) % FFN_TM == 0
    consts = {k: jnp.asarray(v) for k, v in _mixer_consts().items()}
    cos_t, sin_t = _rope_tables(positions)
    h = x
    for i in range(depth):
        prep = _s5_prepare(s5_lam_re[i], s5_lam_im[i], s5_log_dt[i], s5_b_re[i], s5_b_im[i], s5_c_re[i],
                           s5_c_im[i], s5_d[i])
        w_su = w_in[i][:, _SU[0]:_SU[1]].astype(BF16)
        ys5 = _s5_call(h, norm_mix[i][None, :], w_su, prep)
        params = _mixer_params(i, norm_mix, w_in, w_out, gla_w_a2, gla_b_a, gla_norm, ret_norm, gdn_conv,
                               gdn_a_log, gdn_dt_bias, gdn_norm, s5_w_glu, s5_b_glu)
        h = _mixer_call(h, ys5, cos_t, sin_t, consts, params)
        h2 = _ffn_call(h.reshape(B * L, D), p[i].reshape(B * L, PLE_DIM), norm_ffn[i][None, :],
                       w_ffn_up[i].astype(BF16), w_ffn_down[i].astype(BF16), norm_ple[i][None, :],
                       w_ple_gate[i].astype(BF16), w_ple_proj[i].astype(BF16), norm_final[None, :],
                       final=(i == depth - 1))
        h = h2.reshape(B, L, D)
    return h
```

```python
import functools
import math

import numpy as np
import jax
import jax.numpy as jnp
from jax import lax
from jax.experimental import pallas as pl
from jax.experimental.pallas import tpu as pltpu

F32 = jnp.float32
BF16 = jnp.bfloat16
HI = lax.Precision.HIGHEST

D_MODEL = 1024
PLE_DIM = 256
GROUP_WIDTH = 256
NUM_HEADS = 4
CHUNK = 64
EPS = 1e-6

GLA_DK = 32
GLA_RANK = 16
GLA_GATE_NORM = 16.0
S5_CH = 16
S5_GROUPS = 16
S5_STATE = 64
S5_SUB = 8
HEAD_DIM = 64
ROPE_BASE = 10000.0
GDN_CONV = 4
FFN_HIDDEN = 2816

MIX_TL = 256
S5_ROWS = 128
FFN_TM = 512
FFN_SLAB = 256

VMEM_LIMIT = 56 * 1024 * 1024

_IN_OFFS = np.cumsum([0, 128, 128, 256, 16, 256, 256, 256, 256, 256, 256, 256, 256, 256, 4, 4, 256])
(_AQ, _AK, _AV, _ALOW, _AR, _SU, _RQ, _RK, _RV, _RG, _DQ, _DK, _DV, _DB, _DA, _DG) = [
    (int(_IN_OFFS[i]), int(_IN_OFFS[i + 1])) for i in range(16)]
Z_GLA = 0
Z_RET = 768
Z_GDN = 1792
Z_SMALL = 2816
Z_WIDTH = 2944
SM_BETA = 16
SM_A = 20


def _dot(a, b):
    return jnp.dot(a.astype(BF16), b.astype(BF16), preferred_element_type=F32)


def _dot_nt(a, b):
    return lax.dot_general(a.astype(BF16), b.astype(BF16), (((1,), (1,)), ((), ())),
                           preferred_element_type=F32)


def _split(x):
    hi = x.astype(BF16)
    return hi, (x - hi.astype(F32)).astype(BF16)


def _dot_exact_lhs(a, b):
    hi, lo = _split(b)
    n = b.shape[1]
    if n <= 128:
        r = jnp.dot(a.astype(BF16), jnp.concatenate([hi, lo], axis=1), preferred_element_type=F32)
        return r[:, :n] + r[:, n:]
    return (jnp.dot(a.astype(BF16), hi, preferred_element_type=F32)
            + jnp.dot(a.astype(BF16), lo, preferred_element_type=F32))


def _dot_exact_rhs(a, b):
    hi, lo = _split(a)
    m = a.shape[0]
    r = jnp.dot(jnp.concatenate([hi, lo], axis=0), b.astype(BF16), preferred_element_type=F32)
    return r[:m] + r[m:]


def _sigmoid(x):
    return 1.0 / (1.0 + jnp.exp(-x))


def _silu(x):
    return x * _sigmoid(x)


def _softplus(x):
    return jnp.maximum(x, 0.0) + jnp.log1p(jnp.exp(-jnp.abs(x)))


def _rms(x, g):
    return x * lax.rsqrt(jnp.mean(x * x, axis=-1, keepdims=True) + EPS) * g


def _stack4(x):
    return jnp.concatenate([x, x, x, x], axis=0)


def _rope_kernel(pos_ref, invf_ref, sign_ref, cos_ref, sin_ref):
    ang = pos_ref[...].astype(F32) * invf_ref[...]
    cos_ref[...] = jnp.cos(ang)
    sin_ref[...] = jnp.sin(ang) * sign_ref[...]


def _rope_tables(positions):
    B, L = positions.shape
    inv_freq = ROPE_BASE ** (-jnp.linspace(0.0, 1.0, HEAD_DIM // 2, dtype=F32))
    invf = jnp.tile(inv_freq, 4)[None, :]
    sign = np.where((np.arange(128) % HEAD_DIM) < HEAD_DIM // 2, -1.0, 1.0).astype(np.float32)[None, :]
    spec = pl.BlockSpec((None, L, 128), lambda b: (b, 0, 0))
    vec = pl.BlockSpec((1, 128), lambda b: (0, 0))
    return pl.pallas_call(
        _rope_kernel,
        grid=(B,),
        in_specs=[pl.BlockSpec((None, L, 1), lambda b: (b, 0, 0)), vec, vec],
        out_specs=[spec, spec],
        out_shape=[jax.ShapeDtypeStruct((B, L, 128), F32)] * 2,
        name="rope_tables",
    )(positions[:, :, None], invf, jnp.asarray(sign))


def _s5_kernel(h_ref, g_ref, wsu_ref, krev_ref, bbig_ref, cbig_ref, pwr_ref, pwi_ref, d8_ref,
               y_ref, cr_ref, ci_ref, su_scr, y_scr):
    n_state = S5_GROUPS * S5_STATE
    W = GROUP_WIDTH
    n_slab = W // 128

    @pl.when(pl.program_id(1) == 0)
    def _():
        cr_ref[...] = jnp.zeros_like(cr_ref)
        ci_ref[...] = jnp.zeros_like(ci_ref)

    su = _dot(_rms(h_ref[...], g_ref[...]), wsu_ref[...])
    for s in range(n_slab):
        su_scr[s] = su[:, s * 128:(s + 1) * 128]
    u8 = jnp.concatenate([su_scr[s, pl.ds(j, S5_ROWS, stride=S5_SUB), :]
                          for j in range(S5_SUB) for s in range(n_slab)], axis=1)
    u8b = u8.astype(BF16)
    y = jnp.concatenate(
        [jnp.dot(u8b[:, 0:(t + 1) * W], krev_ref[(S5_SUB - 1 - t) * W:S5_SUB * W, :], preferred_element_type=F32)
         for t in range(S5_SUB)], axis=1) + d8_ref[...] * u8
    inc = jnp.dot(u8b, bbig_ref[...], preferred_element_type=F32)
    xr = inc[:, :n_state]
    xi = inc[:, n_state:]

    rows = lax.broadcasted_iota(jnp.int32, (S5_ROWS, n_state), 0)
    ar0 = pwr_ref[0:1, :]
    ai0 = pwi_ref[0:1, :]
    cr = cr_ref[...]
    ci = ci_ref[...]
    first = rows == 0
    xr = xr + jnp.where(first, ar0 * cr - ai0 * ci, 0.0)
    xi = xi + jnp.where(first, ar0 * ci + ai0 * cr, 0.0)
    shift = 1
    lvl = 0
    while shift < S5_ROWS:
        ar = pwr_ref[lvl:lvl + 1, :]
        ai = pwi_ref[lvl:lvl + 1, :]
        keep = rows >= shift
        sr = jnp.where(keep, pltpu.roll(xr, shift, 0), 0.0)
        si = jnp.where(keep, pltpu.roll(xi, shift, 0), 0.0)
        xr, xi = xr + ar * sr - ai * si, xi + ar * si + ai * sr
        shift *= 2
        lvl += 1
    pr = jnp.where(first, cr, pltpu.roll(xr, 1, 0))
    pi = jnp.where(first, ci, pltpu.roll(xi, 1, 0))
    cr_ref[...] = xr[S5_ROWS - 1:S5_ROWS, :]
    ci_ref[...] = xi[S5_ROWS - 1:S5_ROWS, :]
    xprev = jnp.concatenate([pr, pi], axis=1).astype(BF16)
    y = y + jnp.dot(xprev, cbig_ref[...], preferred_element_type=F32)
    for j in range(S5_SUB):
        for s in range(n_slab):
            y_scr[s, pl.ds(j, S5_ROWS, stride=S5_SUB), :] = y[:, j * W + s * 128:j * W + (s + 1) * 128]
    y_ref[...] = jnp.concatenate([y_scr[s] for s in range(n_slab)], axis=1)


def _s5_prepare(lam_re, lam_im, log_dt, b_re, b_im, c_re, c_im, d_skip):
    G, P, H, M = S5_GROUPS, S5_STATE, S5_CH, S5_SUB
    depth = lam_re.shape[0]
    lr = jnp.minimum(lam_re.astype(F32), -1e-4)
    li = lam_im.astype(F32)
    dt = jnp.exp(log_dt.astype(F32))[:, :, None]

    def apow(t):
        tt = jnp.asarray(t, F32)[None, :, None, None]
        mag = jnp.exp((lr * dt)[:, None] * tt)
        ang = (li * dt)[:, None] * tt
        return mag * jnp.cos(ang), mag * jnp.sin(ang)

    n_lvl = int(math.log2(S5_ROWS))
    exps = list(range(M + 1)) + list(range(M - 1, -1, -1)) + [M * 2 ** s for s in range(n_lvl)]
    pw_r, pw_i = apow(np.asarray(exps, np.float32))
    rev_r, rev_i = pw_r[:, M + 1:2 * M + 1], pw_i[:, M + 1:2 * M + 1]
    ar, ai = pw_r[:, 1], pw_i[:, 1]
    nr, ni = ar - 1.0, ai
    den = lr * lr + li * li
    fr = ((nr * lr + ni * li) / den)[..., None]
    fi = ((ni * lr - nr * li) / den)[..., None]
    bbr = fr * b_re - fi * b_im
    bbi = fr * b_im + fi * b_re

    def embed(narrow, tile, row_div, col_div):
        wide = jnp.einsum('dnk,km->dnm', narrow.astype(BF16), jnp.asarray(tile, BF16),
                          preferred_element_type=F32)
        rg = (lax.broadcasted_iota(jnp.int32, wide.shape, 1) // row_div) % G
        cg = (lax.broadcasted_iota(jnp.int32, wide.shape, 2) // col_div) % G
        return jnp.where(rg == cg, wide, 0.0).astype(BF16)

    pr, pi = rev_r[:, :, :, None, :], rev_i[:, :, :, None, :]
    car = c_re[:, None] * pr - c_im[:, None] * pi
    cai = c_re[:, None] * pi + c_im[:, None] * pr
    bbr_t = bbr.transpose(0, 1, 3, 2)[:, None, :, :, None, :]
    bbi_t = bbi.transpose(0, 1, 3, 2)[:, None, :, :, None, :]
    kt = jnp.sum(car[:, :, :, None] * bbr_t - cai[:, :, :, None] * bbi_t, axis=-1)
    k_narrow = kt.reshape(depth, M * G * H, H)
    k_tile = np.tile(np.eye(H, dtype=np.float32), (1, G))
    krev = embed(k_narrow, k_tile, H, H)

    pr, pi = rev_r[..., None], rev_i[..., None]
    b_narrow = jnp.stack([pr * bbr[:, None] - pi * bbi[:, None], pr * bbi[:, None] + pi * bbr[:, None]],
                         axis=1)
    b_narrow = b_narrow.transpose(0, 2, 3, 5, 1, 4).reshape(depth, M * G * H, 2 * P)
    cp = np.arange(2 * P)
    col = np.arange(2 * G * P)
    b_tile = ((cp[:, None] // P == (col // (G * P))[None, :]) & (cp[:, None] % P == (col % P)[None, :]))
    bbig = embed(b_narrow, b_tile.astype(np.float32), H, P)

    pr, pi = pw_r[:, 1:M + 1, :, None, :], pw_i[:, 1:M + 1, :, None, :]
    qr = c_re[:, None] * pr - c_im[:, None] * pi
    qi = -(c_re[:, None] * pi + c_im[:, None] * pr)
    c_narrow = jnp.stack([qr, qi], axis=1).transpose(0, 1, 3, 5, 2, 4).reshape(depth, 2 * G * P, M * H)
    th = np.arange(M * H)
    col = np.arange(M * G * H)
    c_tile = ((th[:, None] // H == (col // (G * H))[None, :]) & (th[:, None] % H == (col % H)[None, :]))
    cbig = embed(c_narrow, c_tile.astype(np.float32), P, H)

    pad = (-n_lvl) % 8
    pwr = jnp.pad(pw_r[:, 2 * M + 1:].reshape(depth, n_lvl, G * P), ((0, 0), (0, pad), (0, 0)))
    pwi = jnp.pad(pw_i[:, 2 * M + 1:].reshape(depth, n_lvl, G * P), ((0, 0), (0, pad), (0, 0)))
    d8 = jnp.tile(d_skip.astype(F32), (1, M))[:, None, :]
    return krev, bbig, cbig, pwr, pwi, d8


def _layer_spec(arr, layer, n_grid):
    nd = arr.ndim - 1
    zeros = (0,) * nd
    if n_grid == 1:
        return pl.BlockSpec((None,) + arr.shape[1:], lambda i: (layer,) + zeros, pipeline_mode=pl.Buffered(1))
    return pl.BlockSpec((None,) + arr.shape[1:], lambda b, i: (layer,) + zeros, pipeline_mode=pl.Buffered(1))


def _s5_call(h, layer, norm_g, w_su, prep):
    B, L, D = h.shape
    krev, bbig, cbig, pwr, pwi, d8 = prep
    tokens = S5_SUB * S5_ROWS
    n_state = S5_GROUPS * S5_STATE
    n_slab = GROUP_WIDTH // 128
    params = [norm_g, w_su, krev, bbig, cbig, pwr, pwi, d8]

    return pl.pallas_call(
        _s5_kernel,
        grid=(B, L // tokens),
        in_specs=[pl.BlockSpec((None, tokens, D), lambda b, i: (b, i, 0))]
        + [_layer_spec(a, layer, 2) for a in params],
        out_specs=pl.BlockSpec((None, tokens, GROUP_WIDTH), lambda b, i: (b, i, 0)),
        out_shape=jax.ShapeDtypeStruct((B, L, GROUP_WIDTH), F32),
        scratch_shapes=[pltpu.VMEM((1, n_state), F32), pltpu.VMEM((1, n_state), F32),
                        pltpu.VMEM((n_slab, tokens, 128), F32), pltpu.VMEM((n_slab, tokens, 128), F32)],
        compiler_params=pltpu.CompilerParams(dimension_semantics=("arbitrary", "arbitrary"),
                                             vmem_limit_bytes=VMEM_LIMIT),
        name="s5_mixer",
    )(h, *params)


def _mixer_consts():
    TL, C, H = MIX_TL, CHUNK, NUM_HEADS
    t = np.arange(TL)
    ltri = ((t[:, None] // C == t[None, :] // C) & (t[None, :] <= t[:, None])).astype(np.float32)
    lane256 = np.arange(256)
    lane128 = np.arange(128)
    hm128 = (lane128[None, :] // GLA_DK == np.arange(H)[:, None]).astype(np.float32)
    hm256 = (lane256[None, :] // HEAD_DIM == np.arange(H)[:, None]).astype(np.float32)
    bm = (lane256[:, None] // HEAD_DIM == lane256[None, :] // HEAD_DIM).astype(np.float32)
    bmt = (lane256[:, None] // HEAD_DIM == lane128[None, :] // GLA_DK).astype(np.float32)
    c = np.arange(C)
    caus4 = np.tile((c[:, None] >= c[None, :]).astype(np.float32), (H, 1))
    s_lane = lane128 % HEAD_DIM
    bm2 = bm[0:128, 0:128]
    incl = (c[:, None] >= s_lane[None, :]).astype(np.float32)
    strict = (c[:, None] > s_lane[None, :]).astype(np.float32)
    idiag = (c[:, None] == s_lane[None, :]).astype(np.float32)
    log_gamma = np.log1p(-(2.0 ** (-5.0 - np.arange(H, dtype=np.float32)))).astype(np.float32)
    idx = np.arange(TL, dtype=np.float32)
    rel = idx[:, None] - idx[None, :]
    dmask = np.where(rel >= 0, np.exp(np.maximum(rel, 0.0)[None] * log_gamma[:, None, None]), 0.0)
    dmask = dmask.reshape(H * TL, TL).astype(np.float32)
    lg_lane = log_gamma[lane256 // HEAD_DIM]
    xi = np.exp((idx[:, None] + 1.0) * lg_lane[None, :]).astype(np.float32)
    zeta = np.exp((TL - 1.0 - idx[:, None]) * lg_lane[None, :]).astype(np.float32)
    gchunk = np.exp(TL * lg_lane)[None, :].astype(np.float32)
    e_beta = np.zeros((128, 256), np.float32)
    e_a = np.zeros((128, 256), np.float32)
    for h in range(H):
        e_beta[SM_BETA + h, h * HEAD_DIM:(h + 1) * HEAD_DIM] = 1.0
        e_a[SM_A + h, h * HEAD_DIM:(h + 1) * HEAD_DIM] = 1.0
    return dict(ltri=ltri, hm128=hm128, hm256=hm256, bm=bm, bmt=bmt, caus4=caus4, incl=incl, strict=strict,
                idiag=idiag, bm2=bm2, dmask=dmask, xi=xi, zeta=zeta, gchunk=gchunk, e_beta=e_beta, e_a=e_a,
                m64=bm / HEAD_DIM)


_CONST_ORDER = ("ltri", "hm128", "hm256", "bm", "bmt", "caus4", "incl", "strict", "idiag", "bm2", "dmask", "xi",
                "zeta", "gchunk", "e_beta", "e_a", "m64")
_PARAM_ORDER = ("norm_mix", "w_z", "w_out", "w_a2", "b_a", "gla_g", "ret_g", "gdn_g", "conv_w", "alog_v",
                "dtb_v", "w_glu", "b_glu")


def _mixer_kernel(*refs):
    n_c, n_p = len(_CONST_ORDER), len(_PARAM_ORDER)
    h_ref, ys5_ref, cos_ref, sin_ref = refs[:4]
    cst = dict(zip(_CONST_ORDER, refs[4:4 + n_c]))
    prm = dict(zip(_PARAM_ORDER, refs[4 + n_c:4 + n_c + n_p]))
    o_ref = refs[4 + n_c + n_p]
    st_gla, s_ret, s_gdn, xpad = refs[4 + n_c + n_p + 1:]
    TL, C, H = MIX_TL, CHUNK, NUM_HEADS

    @pl.when(pl.program_id(1) == 0)
    def _():
        st_gla[...] = jnp.zeros_like(st_gla)
        s_ret[...] = jnp.zeros_like(s_ret)
        s_gdn[...] = jnp.zeros_like(s_gdn)
        xpad[0:8, :] = jnp.zeros((8, 3 * GROUP_WIDTH), F32)

    h = h_ref[...]
    hn = _rms(h, prm["norm_mix"][...]).astype(BF16)
    w_z = prm["w_z"]

    def proj(lo, hi):
        return jnp.dot(hn, w_z[:, lo:hi], preferred_element_type=F32)

    hm128 = cst["hm128"][...]
    hm256 = cst["hm256"][...]
    bm = cst["bm"][...]
    m64 = cst["m64"][...]
    ltri = cst["ltri"][...]
    sm = proj(Z_SMALL, Z_WIDTH)

    def take_heads(stacked, rows):
        out = stacked[0:rows] * hm256[0:1]
        for hh in range(1, H):
            out = out + stacked[hh * rows:(hh + 1) * rows] * hm256[hh:hh + 1]
        return out

    zg = proj(Z_GLA, Z_GLA + 768)
    q, k, v, r = zg[:, 0:128], zg[:, 128:256], zg[:, 256:512], zg[:, 512:768]
    gk = _dot(sm, prm["w_a2"][...]) + prm["b_a"][...]
    gk = (jnp.minimum(gk, 0.0) - jnp.log1p(jnp.exp(-jnp.abs(gk)))) / GLA_GATE_NORM
    bcum = _dot_exact_lhs(ltri, gk)
    caus4 = cst["caus4"][...]
    bmt = cst["bmt"][...]
    st = st_gla[...]
    o_chunks = []
    for c in range(TL // C):
        sl = slice(c * C, (c + 1) * C)
        b_c = bcum[sl]
        b_last = b_c[C - 1:C, :]
        q_t = q[sl] * jnp.exp(b_c) * (GLA_DK ** -0.5)
        k_t = k[sl] * jnp.exp(-b_c)
        k_end = k[sl] * jnp.exp(b_last - b_c)
        v_c = v[sl]
        qs = jnp.concatenate([q_t * hm128[hh:hh + 1] for hh in range(H)], axis=0)
        att = _dot_nt(qs, k_t) * caus4
        o_c = take_heads(_dot(att, v_c), C) + _dot_nt(q_t, st)
        st = st * jnp.exp(b_last) + bmt * _dot(v_c.T, k_end)
        o_chunks.append(o_c)
    st_gla[...] = st
    o_a = jnp.concatenate(o_chunks, axis=0)
    o_a = o_a * lax.rsqrt(_dot(o_a * o_a, m64) + EPS) * prm["gla_g"][...] * _silu(r)

    zr = proj(Z_RET, Z_RET + 1024)
    cos2 = jnp.concatenate([cos_ref[...], cos_ref[...]], axis=1)
    sin2 = jnp.concatenate([sin_ref[...], sin_ref[...]], axis=1)
    first_half = (lax.broadcasted_iota(jnp.int32, (TL, 256), 1) % HEAD_DIM) < HEAD_DIM // 2

    def rope(x):
        rot = jnp.where(first_half, pltpu.roll(x, 256 - HEAD_DIM // 2, 1), pltpu.roll(x, HEAD_DIM // 2, 1))
        return x * cos2 + rot * sin2

    rq = rope(zr[:, 0:256])
    rk = rope(zr[:, 256:512]) * (HEAD_DIM ** -0.5)
    rv = zr[:, 512:768]
    rg = zr[:, 768:1024]
    qs = jnp.concatenate([rq * hm256[hh:hh + 1] for hh in range(H)], axis=0)
    sc = _dot_nt(qs, rk) * cst["dmask"][...]
    s_prev = s_ret[...]
    o_c = take_heads(_dot(sc, rv), TL) + _dot(rq, s_prev) * cst["xi"][...]
    s_ret[...] = s_prev * cst["gchunk"][...] + bm * _dot((rk * cst["zeta"][...]).T, rv)
    o_c = o_c - _dot_exact_rhs(o_c, m64)
    o_c = o_c * lax.rsqrt(_dot(o_c * o_c, m64) + EPS) * prm["ret_g"][...] * _silu(rg)

    zd = proj(Z_GDN, Z_GDN + 1024)
    xpad[8:8 + TL, :] = zd[:, 0:768]
    cw = prm["conv_w"][...]
    xc = cw[GDN_CONV - 1:GDN_CONV, :] * zd[:, 0:768]
    for j in range(GDN_CONV - 1):
        xc = xc + cw[j:j + 1, :] * xpad[5 + j:5 + j + TL, :]
    xpad[0:8, :] = zd[TL - 8:TL, 0:768]
    xc = _silu(xc)
    dq, dk, dv = xc[:, 0:256], xc[:, 256:512], xc[:, 512:768]
    ones64 = bm
    dq = dq * lax.rsqrt(_dot(dq * dq, ones64) + EPS) * (HEAD_DIM ** -0.5)
    dk = dk * lax.rsqrt(_dot(dk * dk, ones64) + EPS)
    beta = _dot_exact_rhs(_sigmoid(sm), cst["e_beta"][...])
    g_log = _dot_exact_rhs(-jnp.exp(prm["alog_v"][...]) * _softplus(sm + prm["dtb_v"][...]), cst["e_a"][...])
    gcum = _dot_exact_lhs(ltri, g_log)
    incl = cst["incl"][...]
    strict = cst["strict"][...]
    idiag = cst["idiag"][...]
    bm2 = cst["bm2"][...]
    bm2b = bm2.astype(BF16)
    n_pair = NUM_HEADS // 2
    probs = [(c, p) for c in range(TL // C) for p in range(n_pair)]

    def blk(x, c, p):
        return x[c * C:(c + 1) * C, p * 128:(p + 1) * 128]

    def bd2(x):
        xb = x.astype(BF16)
        return jnp.concatenate([xb, xb], axis=0) * bm2b

    def bdot(a, b):
        return jnp.dot(a.astype(BF16), b, preferred_element_type=F32)

    gk_, gq_, gv_, gb_, gg_, p_, t_, aqk_ = {}, {}, {}, {}, {}, {}, {}, {}
    for cp in probs:
        k_c, q_c, g_c = blk(dk, *cp), blk(dq, *cp), blk(gcum, *cp)
        gk_[cp], gq_[cp], gv_[cp], gb_[cp], gg_[cp] = k_c, q_c, blk(dv, *cp), blk(beta, *cp), g_c
        g_row = jnp.sum(g_c * idiag, axis=0, keepdims=True)
        dec = jnp.where(incl > 0.0, jnp.exp(jnp.where(incl > 0.0, g_c - g_row, 0.0)), 0.0)
        kq = lax.dot_general(jnp.concatenate([k_c, q_c], axis=0).astype(BF16), bd2(k_c),
                             (((1,), (1,)), ((), ())), preferred_element_type=F32)
        aqk_[cp] = kq[C:2 * C] * dec
        p_[cp] = -(kq[0:C] * dec * strict * gb_[cp])
        t_[cp] = idiag + p_[cp]
    for cp in probs:
        p_[cp] = bdot(p_[cp], bd2(p_[cp]))
    for _ in range(int(math.log2(C)) - 2):
        for cp in probs:
            pt = bdot(jnp.concatenate([p_[cp], t_[cp]], axis=0), bd2(p_[cp]))
            p_[cp] = pt[0:C]
            t_[cp] = t_[cp] + pt[C:2 * C]
    for cp in probs:
        t_[cp] = t_[cp] + bdot(t_[cp], bd2(p_[cp]))
    mm_, nn_, dl_, qp_, op_ = {}, {}, {}, {}, {}
    for cp in probs:
        k_c, g_c, b_c = gk_[cp], gg_[cp], gb_[cp]
        g_last = g_c[C - 1:C, :]
        eg = jnp.exp(g_c)
        uw = bdot(t_[cp], jnp.concatenate([bd2(gv_[cp] * b_c), bd2(k_c * b_c * eg)], axis=1))
        u_c, w_c = uw[:, 0:128], uw[:, 128:256]
        k_end_t = (k_c * jnp.exp(g_last - g_c)).T
        mn = bdot(k_end_t, jnp.concatenate([w_c, u_c], axis=1).astype(BF16))
        mm_[cp] = bm2 * mn[:, 0:128]
        nn_[cp] = bm2 * mn[:, 128:256]
        dl_[cp] = jnp.exp(g_last)
        qo = bdot(aqk_[cp], jnp.concatenate([bd2(w_c), bd2(u_c)], axis=1))
        qp_[cp] = gq_[cp] * eg - qo[:, 0:128]
        op_[cp] = qo[:, 128:256]
    o_rows = []
    sgp = [s_gdn[p] for p in range(n_pair)]
    for c in range(TL // C):
        o_pair = []
        for p in range(n_pair):
            cp = (c, p)
            s_b = sgp[p].astype(BF16)
            o_pair.append(bdot(qp_[cp], s_b) + op_[cp])
            sgp[p] = sgp[p] * dl_[cp] - bdot(mm_[cp], s_b) + nn_[cp]
        o_rows.append(jnp.concatenate(o_pair, axis=1))
    for p in range(n_pair):
        s_gdn[p] = sgp[p]
    o_d = jnp.concatenate(o_rows, axis=0)
    o_d = o_d * lax.rsqrt(_dot(o_d * o_d, m64) + EPS) * prm["gdn_g"][...] * _silu(zd[:, 768:1024])

    y = ys5_ref[...]
    y = 0.5 * y * (1.0 + jnp.tanh(math.sqrt(2.0 / math.pi) * (y + 0.044715 * (y * y * y))))
    o_b = y * _sigmoid(_dot(y, prm["w_glu"][...]) + prm["b_glu"][...])

    mix = jnp.concatenate([o_a, o_b, o_c, o_d], axis=1).astype(BF16)
    o_ref[...] = h + jnp.dot(mix, prm["w_out"][...], preferred_element_type=F32)


def _mixer_call(h, ys5, cos_t, sin_t, consts, params, layer):
    B, L, D = h.shape
    TL = MIX_TL

    def const(arr):
        nd = arr.ndim
        return pl.BlockSpec(arr.shape, lambda b, i: (0,) * nd, pipeline_mode=pl.Buffered(1))

    def tile(width):
        return pl.BlockSpec((None, TL, width), lambda b, i: (b, i, 0))

    c_list = [consts[n] for n in _CONST_ORDER]
    p_list = [params[n] for n in _PARAM_ORDER]
    return pl.pallas_call(
        _mixer_kernel,
        grid=(B, L // TL),
        in_specs=[tile(D), tile(GROUP_WIDTH), tile(128), tile(128)] + [const(a) for a in c_list]
        + [_layer_spec(a, layer, 2) for a in p_list],
        out_specs=tile(D),
        out_shape=jax.ShapeDtypeStruct((B, L, D), F32),
        scratch_shapes=[pltpu.VMEM((256, 128), F32), pltpu.VMEM((256, 256), F32), pltpu.VMEM((2, 128, 128), F32),
                        pltpu.VMEM((TL + 8, 3 * GROUP_WIDTH), F32)],
        compiler_params=pltpu.CompilerParams(dimension_semantics=("arbitrary", "arbitrary"),
                                             vmem_limit_bytes=VMEM_LIMIT),
        name="mixer",
    )(h, ys5, cos_t, sin_t, *c_list, *p_list)


def _ffn_kernel(h_ref, p_ref, gf_ref, wup_ref, wdn_ref, gp_ref, wg_ref, wp_ref, gl_ref, o_ref, act_ref, *,
                final):
    h = h_ref[...]
    hn = _rms(h, gf_ref[...]).astype(BF16)
    for s in range(FFN_HIDDEN // FFN_SLAB):
        lo = s * FFN_SLAB
        g = jnp.dot(hn, wup_ref[:, lo:lo + FFN_SLAB], preferred_element_type=F32)
        u = jnp.dot(hn, wup_ref[:, FFN_HIDDEN + lo:FFN_HIDDEN + lo + FFN_SLAB], preferred_element_type=F32)
        act_ref[:, lo:lo + FFN_SLAB] = (_silu(g) * u).astype(BF16)
    h2 = h + jnp.dot(act_ref[...], wdn_ref[...], preferred_element_type=F32)
    gate = _sigmoid(jnp.dot(_rms(h2, gp_ref[...]).astype(BF16), wg_ref[...], preferred_element_type=F32))
    h3 = h2 + jnp.dot(p_ref[...].astype(BF16), wp_ref[...], preferred_element_type=F32) * gate
    if final:
        h3 = _rms(h3, gl_ref[...])
    o_ref[...] = h3


def _ffn_call(h2d, p3d, layer, weights, final):
    T, D = h2d.shape
    TM = FFN_TM
    return pl.pallas_call(
        functools.partial(_ffn_kernel, final=final),
        grid=(T // TM,),
        in_specs=[pl.BlockSpec((TM, D), lambda i: (i, 0)),
                  pl.BlockSpec((None, TM, PLE_DIM), lambda i: (layer, i, 0))]
        + [_layer_spec(w, layer, 1) for w in weights],
        out_specs=pl.BlockSpec((TM, D), lambda i: (i, 0)),
        out_shape=jax.ShapeDtypeStruct((T, D), F32),
        scratch_shapes=[pltpu.VMEM((TM, FFN_HIDDEN), BF16)],
        compiler_params=pltpu.CompilerParams(dimension_semantics=("arbitrary",), vmem_limit_bytes=VMEM_LIMIT),
        name="ffn_ple",
    )(h2d, p3d, *weights)


def _row(v):
    return v[:, None, :]


def _mixer_params(norm_mix, w_in, w_out, gla_w_a2, gla_b_a, gla_norm, ret_norm, gdn_conv, gdn_a_log,
                  gdn_dt_bias, gdn_norm, s5_w_glu, s5_b_glu):
    depth = w_in.shape[0]

    def cols(seg):
        return w_in[:, :, seg[0]:seg[1]]

    small_pad = jnp.zeros((depth, D_MODEL, 128 - GLA_RANK - 2 * NUM_HEADS), F32)
    w_z = jnp.concatenate([cols(_AQ), cols(_AK), cols(_AV), cols(_AR), cols(_RQ), cols(_RK), cols(_RV),
                           cols(_RG), cols(_DQ), cols(_DK), cols(_DV), cols(_DG), cols(_ALOW), cols(_DB),
                           cols(_DA), small_pad], axis=2).astype(BF16)
    w_a2 = jnp.pad(gla_w_a2, ((0, 0), (0, 128 - GLA_RANK), (0, 0)))
    lane_pad = ((0, 0), (SM_A, 128 - SM_A - NUM_HEADS))
    return dict(norm_mix=_row(norm_mix), w_z=w_z, w_out=w_out.astype(BF16), w_a2=w_a2, b_a=_row(gla_b_a),
                gla_g=_row(jnp.tile(gla_norm, (1, NUM_HEADS))), ret_g=_row(ret_norm),
                gdn_g=_row(jnp.tile(gdn_norm, (1, NUM_HEADS))), conv_w=gdn_conv,
                alog_v=_row(jnp.pad(gdn_a_log, lane_pad)), dtb_v=_row(jnp.pad(gdn_dt_bias, lane_pad)),
                w_glu=s5_w_glu.astype(BF16), b_glu=_row(s5_b_glu))


def kernel(x, p, positions, norm_mix, w_in, w_out, gla_w_a2, gla_b_a, gla_norm, s5_lam_re, s5_lam_im, s5_log_dt,
           s5_b_re, s5_b_im, s5_c_re, s5_c_im, s5_d, s5_w_glu, s5_b_glu, ret_norm, gdn_conv, gdn_a_log,
           gdn_dt_bias, gdn_norm, norm_ffn, w_ffn_up, w_ffn_down, norm_ple, w_ple_gate, w_ple_proj, norm_final):
    B, L, D = x.shape
    depth = w_in.shape[0]
    assert D == D_MODEL and L % max(MIX_TL, S5_SUB * S5_ROWS) == 0 and (B * L) % FFN_TM == 0
    consts = {k: jnp.asarray(v) for k, v in _mixer_consts().items()}
    cos_t, sin_t = _rope_tables(positions)
    prep = _s5_prepare(s5_lam_re, s5_lam_im, s5_log_dt, s5_b_re, s5_b_im, s5_c_re, s5_c_im, s5_d)
    w_su = w_in[:, :, _SU[0]:_SU[1]].astype(BF16)
    params = _mixer_params(norm_mix, w_in, w_out, gla_w_a2, gla_b_a, gla_norm, ret_norm, gdn_conv,
                           gdn_a_log, gdn_dt_bias, gdn_norm, s5_w_glu, s5_b_glu)
    ffn_w = [_row(norm_ffn), w_ffn_up.astype(BF16), w_ffn_down.astype(BF16), _row(norm_ple),
             w_ple_gate.astype(BF16), w_ple_proj.astype(BF16),
             jnp.broadcast_to(norm_final[None, None, :], (depth, 1, D))]
    p3d = p.reshape(depth, B * L, PLE_DIM)
    h = x
    for i in range(depth):
        ys5 = _s5_call(h, i, _row(norm_mix), w_su, prep)
        h = _mixer_call(h, ys5, cos_t, sin_t, consts, params, i)
        h = _ffn_call(h.reshape(B * L, D), p3d, i, ffn_w, final=(i == depth - 1)).reshape(B, L, D)
    return h
```

```python
import functools
import math

import numpy as np
import jax
import jax.numpy as jnp
from jax import lax
from jax.experimental import pallas as pl
from jax.experimental.pallas import tpu as pltpu

F32 = jnp.float32
BF16 = jnp.bfloat16
HI = lax.Precision.HIGHEST

D_MODEL = 1024
PLE_DIM = 256
GROUP_WIDTH = 256
NUM_HEADS = 4
CHUNK = 64
EPS = 1e-6

GLA_DK = 32
GLA_RANK = 16
GLA_GATE_NORM = 16.0
S5_CH = 16
S5_GROUPS = 16
S5_STATE = 64
S5_SUB = 8
HEAD_DIM = 64
ROPE_BASE = 10000.0
GDN_CONV = 4
FFN_HIDDEN = 2816

MIX_TL = 256
S5_ROWS = 128
FFN_TM = 512
FFN_SLAB = 256

VMEM_LIMIT = 56 * 1024 * 1024

_IN_OFFS = np.cumsum([0, 128, 128, 256, 16, 256, 256, 256, 256, 256, 256, 256, 256, 256, 4, 4, 256])
(_AQ, _AK, _AV, _ALOW, _AR, _SU, _RQ, _RK, _RV, _RG, _DQ, _DK, _DV, _DB, _DA, _DG) = [
    (int(_IN_OFFS[i]), int(_IN_OFFS[i + 1])) for i in range(16)]
Z_GLA = 0
Z_RET = 768
Z_GDN = 1792
Z_SMALL = 2816
Z_WIDTH = 2944
SM_BETA = 16
SM_A = 20


def _dot(a, b):
    return jnp.dot(a.astype(BF16), b.astype(BF16), preferred_element_type=F32)


def _dot_nt(a, b):
    return lax.dot_general(a.astype(BF16), b.astype(BF16), (((1,), (1,)), ((), ())),
                           preferred_element_type=F32)


def _split(x):
    hi = x.astype(BF16)
    return hi, (x - hi.astype(F32)).astype(BF16)


def _dot_exact_lhs(a, b):
    hi, lo = _split(b)
    n = b.shape[1]
    if n <= 128:
        r = jnp.dot(a.astype(BF16), jnp.concatenate([hi, lo], axis=1), preferred_element_type=F32)
        return r[:, :n] + r[:, n:]
    return (jnp.dot(a.astype(BF16), hi, preferred_element_type=F32)
            + jnp.dot(a.astype(BF16), lo, preferred_element_type=F32))


def _dot_exact_rhs(a, b):
    hi, lo = _split(a)
    m = a.shape[0]
    r = jnp.dot(jnp.concatenate([hi, lo], axis=0), b.astype(BF16), preferred_element_type=F32)
    return r[:m] + r[m:]


def _sigmoid(x):
    return 1.0 / (1.0 + jnp.exp(-x))


def _silu(x):
    return x * _sigmoid(x)


def _softplus(x):
    return jnp.maximum(x, 0.0) + jnp.log1p(jnp.exp(-jnp.abs(x)))


def _rms(x, g):
    return x * lax.rsqrt(jnp.mean(x * x, axis=-1, keepdims=True) + EPS) * g


def _stack4(x):
    return jnp.concatenate([x, x, x, x], axis=0)


def _rope_kernel(pos_ref, invf_ref, sign_ref, cos_ref, sin_ref):
    ang = pos_ref[...].astype(F32) * invf_ref[...]
    cos_ref[...] = jnp.cos(ang)
    sin_ref[...] = jnp.sin(ang) * sign_ref[...]


def _rope_tables(positions):
    B, L = positions.shape
    inv_freq = ROPE_BASE ** (-jnp.linspace(0.0, 1.0, HEAD_DIM // 2, dtype=F32))
    invf = jnp.tile(inv_freq, 4)[None, :]
    sign = np.where((np.arange(128) % HEAD_DIM) < HEAD_DIM // 2, -1.0, 1.0).astype(np.float32)[None, :]
    spec = pl.BlockSpec((None, L, 128), lambda b: (b, 0, 0))
    vec = pl.BlockSpec((1, 128), lambda b: (0, 0))
    return pl.pallas_call(
        _rope_kernel,
        grid=(B,),
        in_specs=[pl.BlockSpec((None, L, 1), lambda b: (b, 0, 0)), vec, vec],
        out_specs=[spec, spec],
        out_shape=[jax.ShapeDtypeStruct((B, L, 128), F32)] * 2,
        name="rope_tables",
    )(positions[:, :, None], invf, jnp.asarray(sign))


def _s5_kernel(h_ref, g_ref, wsu_ref, krev_ref, bbig_ref, cbig_ref, pwr_ref, pwi_ref, d8_ref,
               y_ref, cr_ref, ci_ref, su_scr, y_scr):
    n_state = S5_GROUPS * S5_STATE
    W = GROUP_WIDTH
    n_slab = W // 128

    @pl.when(pl.program_id(1) == 0)
    def _():
        cr_ref[...] = jnp.zeros_like(cr_ref)
        ci_ref[...] = jnp.zeros_like(ci_ref)

    su = _dot(_rms(h_ref[...], g_ref[...]), wsu_ref[...])
    for s in range(n_slab):
        su_scr[s] = su[:, s * 128:(s + 1) * 128]
    u8 = jnp.concatenate([su_scr[s, pl.ds(j, S5_ROWS, stride=S5_SUB), :]
                          for j in range(S5_SUB) for s in range(n_slab)], axis=1)
    u8b = u8.astype(BF16)
    y = jnp.concatenate(
        [jnp.dot(u8b[:, 0:(t + 1) * W], krev_ref[(S5_SUB - 1 - t) * W:S5_SUB * W, :], preferred_element_type=F32)
         for t in range(S5_SUB)], axis=1) + d8_ref[...] * u8
    inc = jnp.dot(u8b, bbig_ref[...], preferred_element_type=F32)
    xr = inc[:, :n_state]
    xi = inc[:, n_state:]

    rows = lax.broadcasted_iota(jnp.int32, (S5_ROWS, n_state), 0)
    ar0 = pwr_ref[0:1, :]
    ai0 = pwi_ref[0:1, :]
    cr = cr_ref[...]
    ci = ci_ref[...]
    first = rows == 0
    xr = xr + jnp.where(first, ar0 * cr - ai0 * ci, 0.0)
    xi = xi + jnp.where(first, ar0 * ci + ai0 * cr, 0.0)
    shift = 1
    lvl = 0
    while shift < S5_ROWS:
        ar = pwr_ref[lvl:lvl + 1, :]
        ai = pwi_ref[lvl:lvl + 1, :]
        keep = rows >= shift
        sr = jnp.where(keep, pltpu.roll(xr, shift, 0), 0.0)
        si = jnp.where(keep, pltpu.roll(xi, shift, 0), 0.0)
        xr, xi = xr + ar * sr - ai * si, xi + ar * si + ai * sr
        shift *= 2
        lvl += 1
    pr = jnp.where(first, cr, pltpu.roll(xr, 1, 0))
    pi = jnp.where(first, ci, pltpu.roll(xi, 1, 0))
    cr_ref[...] = xr[S5_ROWS - 1:S5_ROWS, :]
    ci_ref[...] = xi[S5_ROWS - 1:S5_ROWS, :]
    xprev = jnp.concatenate([pr, pi], axis=1).astype(BF16)
    y = y + jnp.dot(xprev, cbig_ref[...], preferred_element_type=F32)
    for j in range(S5_SUB):
        for s in range(n_slab):
            y_scr[s, pl.ds(j, S5_ROWS, stride=S5_SUB), :] = y[:, j * W + s * 128:j * W + (s + 1) * 128]
    y_ref[...] = jnp.concatenate([y_scr[s] for s in range(n_slab)], axis=1)


def _s5_prepare(lam_re, lam_im, log_dt, b_re, b_im, c_re, c_im, d_skip):
    G, P, H, M = S5_GROUPS, S5_STATE, S5_CH, S5_SUB
    depth = lam_re.shape[0]
    lr = jnp.minimum(lam_re.astype(F32), -1e-4)
    li = lam_im.astype(F32)
    dt = jnp.exp(log_dt.astype(F32))[:, :, None]

    def apow(t):
        tt = jnp.asarray(t, F32)[None, :, None, None]
        mag = jnp.exp((lr * dt)[:, None] * tt)
        ang = (li * dt)[:, None] * tt
        return mag * jnp.cos(ang), mag * jnp.sin(ang)

    n_lvl = int(math.log2(S5_ROWS))
    exps = list(range(M + 1)) + list(range(M - 1, -1, -1)) + [M * 2 ** s for s in range(n_lvl)]
    pw_r, pw_i = apow(np.asarray(exps, np.float32))
    rev_r, rev_i = pw_r[:, M + 1:2 * M + 1], pw_i[:, M + 1:2 * M + 1]
    ar, ai = pw_r[:, 1], pw_i[:, 1]
    nr, ni = ar - 1.0, ai
    den = lr * lr + li * li
    fr = ((nr * lr + ni * li) / den)[..., None]
    fi = ((ni * lr - nr * li) / den)[..., None]
    bbr = fr * b_re - fi * b_im
    bbi = fr * b_im + fi * b_re

    def embed(narrow, tile, row_div, col_div):
        wide = jnp.einsum('dnk,km->dnm', narrow.astype(BF16), jnp.asarray(tile, BF16),
                          preferred_element_type=F32)
        rg = (lax.broadcasted_iota(jnp.int32, wide.shape, 1) // row_div) % G
        cg = (lax.broadcasted_iota(jnp.int32, wide.shape, 2) // col_div) % G
        return jnp.where(rg == cg, wide, 0.0).astype(BF16)

    pr, pi = rev_r[:, :, :, None, :], rev_i[:, :, :, None, :]
    car = c_re[:, None] * pr - c_im[:, None] * pi
    cai = c_re[:, None] * pi + c_im[:, None] * pr
    bbr_t = bbr.transpose(0, 1, 3, 2)[:, None, :, :, None, :]
    bbi_t = bbi.transpose(0, 1, 3, 2)[:, None, :, :, None, :]
    kt = jnp.sum(car[:, :, :, None] * bbr_t - cai[:, :, :, None] * bbi_t, axis=-1)
    k_narrow = kt.reshape(depth, M * G * H, H)
    k_tile = np.tile(np.eye(H, dtype=np.float32), (1, G))
    krev = embed(k_narrow, k_tile, H, H)

    pr, pi = rev_r[..., None], rev_i[..., None]
    b_narrow = jnp.stack([pr * bbr[:, None] - pi * bbi[:, None], pr * bbi[:, None] + pi * bbr[:, None]],
                         axis=1)
    b_narrow = b_narrow.transpose(0, 2, 3, 5, 1, 4).reshape(depth, M * G * H, 2 * P)
    cp = np.arange(2 * P)
    col = np.arange(2 * G * P)
    b_tile = ((cp[:, None] // P == (col // (G * P))[None, :]) & (cp[:, None] % P == (col % P)[None, :]))
    bbig = embed(b_narrow, b_tile.astype(np.float32), H, P)

    pr, pi = pw_r[:, 1:M + 1, :, None, :], pw_i[:, 1:M + 1, :, None, :]
    qr = c_re[:, None] * pr - c_im[:, None] * pi
    qi = -(c_re[:, None] * pi + c_im[:, None] * pr)
    c_narrow = jnp.stack([qr, qi], axis=1).transpose(0, 1, 3, 5, 2, 4).reshape(depth, 2 * G * P, M * H)
    th = np.arange(M * H)
    col = np.arange(M * G * H)
    c_tile = ((th[:, None] // H == (col // (G * H))[None, :]) & (th[:, None] % H == (col % H)[None, :]))
    cbig = embed(c_narrow, c_tile.astype(np.float32), P, H)

    pad = (-n_lvl) % 8
    pwr = jnp.pad(pw_r[:, 2 * M + 1:].reshape(depth, n_lvl, G * P), ((0, 0), (0, pad), (0, 0)))
    pwi = jnp.pad(pw_i[:, 2 * M + 1:].reshape(depth, n_lvl, G * P), ((0, 0), (0, pad), (0, 0)))
    d8 = jnp.tile(d_skip.astype(F32), (1, M))[:, None, :]
    return krev, bbig, cbig, pwr, pwi, d8


def _layer_spec(arr, layer, n_grid):
    nd = arr.ndim - 1
    zeros = (0,) * nd
    if n_grid == 1:
        return pl.BlockSpec((None,) + arr.shape[1:], lambda i: (layer,) + zeros, pipeline_mode=pl.Buffered(1))
    return pl.BlockSpec((None,) + arr.shape[1:], lambda b, i: (layer,) + zeros, pipeline_mode=pl.Buffered(1))


def _s5_call(h, layer, norm_g, w_su, prep):
    B, L, D = h.shape
    krev, bbig, cbig, pwr, pwi, d8 = prep
    tokens = S5_SUB * S5_ROWS
    n_state = S5_GROUPS * S5_STATE
    n_slab = GROUP_WIDTH // 128
    params = [norm_g, w_su, krev, bbig, cbig, pwr, pwi, d8]

    return pl.pallas_call(
        _s5_kernel,
        grid=(B, L // tokens),
        in_specs=[pl.BlockSpec((None, tokens, D), lambda b, i: (b, i, 0))]
        + [_layer_spec(a, layer, 2) for a in params],
        out_specs=pl.BlockSpec((None, tokens, GROUP_WIDTH), lambda b, i: (b, i, 0)),
        out_shape=jax.ShapeDtypeStruct((B, L, GROUP_WIDTH), F32),
        scratch_shapes=[pltpu.VMEM((1, n_state), F32), pltpu.VMEM((1, n_state), F32),
                        pltpu.VMEM((n_slab, tokens, 128), F32), pltpu.VMEM((n_slab, tokens, 128), F32)],
        compiler_params=pltpu.CompilerParams(dimension_semantics=("arbitrary", "arbitrary"),
                                             vmem_limit_bytes=VMEM_LIMIT),
        name="s5_mixer",
    )(h, *params)


def _mixer_consts():
    TL, C, H = MIX_TL, CHUNK, NUM_HEADS
    t = np.arange(TL)
    ltri = ((t[:, None] // C == t[None, :] // C) & (t[None, :] <= t[:, None])).astype(np.float32)
    lane256 = np.arange(256)
    lane128 = np.arange(128)
    hm128 = (lane128[None, :] // GLA_DK == np.arange(H)[:, None]).astype(np.float32)
    hm256 = (lane256[None, :] // HEAD_DIM == np.arange(H)[:, None]).astype(np.float32)
    bm = (lane256[:, None] // HEAD_DIM == lane256[None, :] // HEAD_DIM).astype(np.float32)
    bmt = (lane256[:, None] // HEAD_DIM == lane128[None, :] // GLA_DK).astype(np.float32)
    c = np.arange(C)
    caus4 = np.tile((c[:, None] >= c[None, :]).astype(np.float32), (H, 1))
    s_lane = lane128 % HEAD_DIM
    bm2 = bm[0:128, 0:128]
    incl = (c[:, None] >= s_lane[None, :]).astype(np.float32)
    strict = (c[:, None] > s_lane[None, :]).astype(np.float32)
    idiag = (c[:, None] == s_lane[None, :]).astype(np.float32)
    log_gamma = np.log1p(-(2.0 ** (-5.0 - np.arange(H, dtype=np.float32)))).astype(np.float32)
    idx = np.arange(TL, dtype=np.float32)
    rel = idx[:, None] - idx[None, :]
    dmask = np.where(rel >= 0, np.exp(np.maximum(rel, 0.0)[None] * log_gamma[:, None, None]), 0.0)
    dmask = dmask.reshape(H * TL, TL).astype(np.float32)
    lg_lane = log_gamma[lane256 // HEAD_DIM]
    xi = np.exp((idx[:, None] + 1.0) * lg_lane[None, :]).astype(np.float32)
    zeta = np.exp((TL - 1.0 - idx[:, None]) * lg_lane[None, :]).astype(np.float32)
    gchunk = np.exp(TL * lg_lane)[None, :].astype(np.float32)
    e_beta = np.zeros((128, 256), np.float32)
    e_a = np.zeros((128, 256), np.float32)
    for h in range(H):
        e_beta[SM_BETA + h, h * HEAD_DIM:(h + 1) * HEAD_DIM] = 1.0
        e_a[SM_A + h, h * HEAD_DIM:(h + 1) * HEAD_DIM] = 1.0
    return dict(ltri=ltri, hm128=hm128, hm256=hm256, bm=bm, bmt=bmt, caus4=caus4, incl=incl, strict=strict,
                idiag=idiag, bm2=bm2, dmask=dmask, xi=xi, zeta=zeta, gchunk=gchunk, e_beta=e_beta, e_a=e_a,
                m64=bm / HEAD_DIM)


_CONST_ORDER = ("ltri", "hm128", "hm256", "bm", "bmt", "caus4", "incl", "strict", "idiag", "bm2", "dmask", "xi",
                "zeta", "gchunk", "e_beta", "e_a", "m64")
_PARAM_ORDER = ("norm_mix", "w_z", "w_out", "w_a2", "b_a", "gla_g", "ret_g", "gdn_g", "conv_w", "alog_v",
                "dtb_v", "w_glu", "b_glu")


def _mixer_kernel(*refs):
    n_c, n_p = len(_CONST_ORDER), len(_PARAM_ORDER)
    h_ref, ys5_ref, cos_ref, sin_ref = refs[:4]
    cst = dict(zip(_CONST_ORDER, refs[4:4 + n_c]))
    prm = dict(zip(_PARAM_ORDER, refs[4 + n_c:4 + n_c + n_p]))
    o_ref = refs[4 + n_c + n_p]
    st_gla, s_ret, s_gdn, xpad = refs[4 + n_c + n_p + 1:]
    TL, C, H = MIX_TL, CHUNK, NUM_HEADS

    @pl.when(pl.program_id(1) == 0)
    def _():
        st_gla[...] = jnp.zeros_like(st_gla)
        s_ret[...] = jnp.zeros_like(s_ret)
        s_gdn[...] = jnp.zeros_like(s_gdn)
        xpad[0:8, :] = jnp.zeros((8, 3 * GROUP_WIDTH), F32)

    h = h_ref[...]
    hn = _rms(h, prm["norm_mix"][...]).astype(BF16)
    w_z = prm["w_z"]

    def proj(lo, hi):
        return jnp.dot(hn, w_z[:, lo:hi], preferred_element_type=F32)

    hm128 = cst["hm128"][...]
    hm256 = cst["hm256"][...]
    bm = cst["bm"][...]
    m64 = cst["m64"][...]
    ltri = cst["ltri"][...]
    sm = proj(Z_SMALL, Z_WIDTH)

    def take_heads(stacked, rows):
        out = stacked[0:rows] * hm256[0:1]
        for hh in range(1, H):
            out = out + stacked[hh * rows:(hh + 1) * rows] * hm256[hh:hh + 1]
        return out

    out = {}

    def gdn_stages():
        zd = proj(Z_GDN, Z_GDN + 1024)
        xpad[8:8 + TL, :] = zd[:, 0:768]
        cw = prm["conv_w"][...]
        xc = cw[GDN_CONV - 1:GDN_CONV, :] * zd[:, 0:768]
        for j in range(GDN_CONV - 1):
            xc = xc + cw[j:j + 1, :] * xpad[5 + j:5 + j + TL, :]
        xpad[0:8, :] = zd[TL - 8:TL, 0:768]
        xc = _silu(xc)
        dq, dk, dv = xc[:, 0:256], xc[:, 256:512], xc[:, 512:768]
        yield
        dq = dq * lax.rsqrt(_dot(dq * dq, bm) + EPS) * (HEAD_DIM ** -0.5)
        dk = dk * lax.rsqrt(_dot(dk * dk, bm) + EPS)
        beta = _dot_exact_rhs(_sigmoid(sm), cst["e_beta"][...])
        g_log = _dot_exact_rhs(-jnp.exp(prm["alog_v"][...]) * _softplus(sm + prm["dtb_v"][...]),
                               cst["e_a"][...])
        yield
        gcum = _dot_exact_lhs(ltri, g_log)
        yield
        incl = cst["incl"][...]
        strict = cst["strict"][...]
        idiag = cst["idiag"][...]
        bm2 = cst["bm2"][...]
        bm2b = bm2.astype(BF16)
        n_pair = NUM_HEADS // 2
        probs = [(c, p) for c in range(TL // C) for p in range(n_pair)]

        def blk(x, c, p):
            return x[c * C:(c + 1) * C, p * 128:(p + 1) * 128]

        def bd2(x):
            xb = x.astype(BF16)
            return jnp.concatenate([xb, xb], axis=0) * bm2b

        def bdot(a, b):
            return jnp.dot(a.astype(BF16), b, preferred_element_type=F32)

        gk_, gq_, gv_, gb_, gg_, p_, t_, aqk_ = {}, {}, {}, {}, {}, {}, {}, {}
        for cp in probs:
            k_c, q_c, g_c = blk(dk, *cp), blk(dq, *cp), blk(gcum, *cp)
            gk_[cp], gq_[cp], gv_[cp], gb_[cp], gg_[cp] = k_c, q_c, blk(dv, *cp), blk(beta, *cp), g_c
            g_row = jnp.sum(g_c * idiag, axis=0, keepdims=True)
            dec = jnp.where(incl > 0.0, jnp.exp(jnp.where(incl > 0.0, g_c - g_row, 0.0)), 0.0)
            kq = lax.dot_general(jnp.concatenate([k_c, q_c], axis=0).astype(BF16), bd2(k_c),
                                 (((1,), (1,)), ((), ())), preferred_element_type=F32)
            aqk_[cp] = kq[C:2 * C] * dec
            p_[cp] = -(kq[0:C] * dec * strict * gb_[cp])
            t_[cp] = idiag + p_[cp]
        yield
        for cp in probs:
            p_[cp] = bdot(p_[cp], bd2(p_[cp]))
        yield
        for _ in range(int(math.log2(C)) - 2):
            for cp in probs:
                pt = bdot(jnp.concatenate([p_[cp], t_[cp]], axis=0), bd2(p_[cp]))
                p_[cp] = pt[0:C]
                t_[cp] = t_[cp] + pt[C:2 * C]
            yield
        for cp in probs:
            t_[cp] = t_[cp] + bdot(t_[cp], bd2(p_[cp]))
        yield
        uw_, eg_ = {}, {}
        for cp in probs:
            eg_[cp] = jnp.exp(gg_[cp])
            b_c = gb_[cp]
            uw_[cp] = bdot(t_[cp], jnp.concatenate([bd2(gv_[cp] * b_c), bd2(gk_[cp] * b_c * eg_[cp])],
                                                   axis=1))
        yield
        mm_, nn_, dl_, qp_, op_ = {}, {}, {}, {}, {}
        for cp in probs:
            k_c, g_c = gk_[cp], gg_[cp]
            g_last = g_c[C - 1:C, :]
            u_c, w_c = uw_[cp][:, 0:128], uw_[cp][:, 128:256]
            k_end_t = (k_c * jnp.exp(g_last - g_c)).T
            mn = bdot(k_end_t, jnp.concatenate([w_c, u_c], axis=1).astype(BF16))
            mm_[cp] = bm2 * mn[:, 0:128]
            nn_[cp] = bm2 * mn[:, 128:256]
            dl_[cp] = jnp.exp(g_last)
            qo = bdot(aqk_[cp], jnp.concatenate([bd2(w_c), bd2(u_c)], axis=1))
            qp_[cp] = gq_[cp] * eg_[cp] - qo[:, 0:128]
            op_[cp] = qo[:, 128:256]
        yield
        o_rows = []
        sgp = [s_gdn[p] for p in range(n_pair)]
        for c in range(TL // C):
            o_pair = []
            for p in range(n_pair):
                cp = (c, p)
                s_b = sgp[p].astype(BF16)
                o_pair.append(bdot(qp_[cp], s_b) + op_[cp])
                sgp[p] = sgp[p] * dl_[cp] - bdot(mm_[cp], s_b) + nn_[cp]
            o_rows.append(jnp.concatenate(o_pair, axis=1))
            yield
        for p in range(n_pair):
            s_gdn[p] = sgp[p]
        o_d = jnp.concatenate(o_rows, axis=0)
        out["d"] = o_d * lax.rsqrt(_dot(o_d * o_d, m64) + EPS) * prm["gdn_g"][...] * _silu(zd[:, 768:1024])

    def gla_stages():
        zg = proj(Z_GLA, Z_GLA + 768)
        q, k, v, r = zg[:, 0:128], zg[:, 128:256], zg[:, 256:512], zg[:, 512:768]
        gk = _dot(sm, prm["w_a2"][...]) + prm["b_a"][...]
        gk = (jnp.minimum(gk, 0.0) - jnp.log1p(jnp.exp(-jnp.abs(gk)))) / GLA_GATE_NORM
        yield
        bcum = _dot_exact_lhs(ltri, gk)
        caus4 = cst["caus4"][...]
        bmt = cst["bmt"][...]
        st = st_gla[...]
        yield
        o_chunks = []
        for c in range(TL // C):
            sl = slice(c * C, (c + 1) * C)
            b_c = bcum[sl]
            b_last = b_c[C - 1:C, :]
            q_t = q[sl] * jnp.exp(b_c) * (GLA_DK ** -0.5)
            k_t = k[sl] * jnp.exp(-b_c)
            k_end = k[sl] * jnp.exp(b_last - b_c)
            v_c = v[sl]
            qs = jnp.concatenate([q_t * hm128[hh:hh + 1] for hh in range(H)], axis=0)
            att = _dot_nt(qs, k_t) * caus4
            cross = _dot_nt(q_t, st)
            st = st * jnp.exp(b_last) + bmt * _dot(v_c.T, k_end)
            yield
            o_chunks.append(take_heads(_dot(att, v_c), C) + cross)
            yield
        st_gla[...] = st
        o_a = jnp.concatenate(o_chunks, axis=0)
        out["a"] = o_a * lax.rsqrt(_dot(o_a * o_a, m64) + EPS) * prm["gla_g"][...] * _silu(r)

    def ret_stages():
        zr = proj(Z_RET, Z_RET + 1024)
        cos2 = jnp.concatenate([cos_ref[...], cos_ref[...]], axis=1)
        sin2 = jnp.concatenate([sin_ref[...], sin_ref[...]], axis=1)
        first_half = (lax.broadcasted_iota(jnp.int32, (TL, 256), 1) % HEAD_DIM) < HEAD_DIM // 2

        def rope(x):
            rot = jnp.where(first_half, pltpu.roll(x, 256 - HEAD_DIM // 2, 1),
                            pltpu.roll(x, HEAD_DIM // 2, 1))
            return x * cos2 + rot * sin2

        rq = rope(zr[:, 0:256])
        rk = rope(zr[:, 256:512]) * (HEAD_DIM ** -0.5)
        rv = zr[:, 512:768]
        rg = zr[:, 768:1024]
        yield
        qs = jnp.concatenate([rq * hm256[hh:hh + 1] for hh in range(H)], axis=0)
        sc = _dot_nt(qs, rk) * cst["dmask"][...]
        s_prev = s_ret[...]
        cross = _dot(rq, s_prev) * cst["xi"][...]
        s_ret[...] = s_prev * cst["gchunk"][...] + bm * _dot((rk * cst["zeta"][...]).T, rv)
        yield
        o_c = take_heads(_dot(sc, rv), TL) + cross
        yield
        o_c = o_c - _dot_exact_rhs(o_c, m64)
        yield
        out["c"] = o_c * lax.rsqrt(_dot(o_c * o_c, m64) + EPS) * prm["ret_g"][...] * _silu(rg)

    def s5_stages():
        y = ys5_ref[...]
        y = 0.5 * y * (1.0 + jnp.tanh(math.sqrt(2.0 / math.pi) * (y + 0.044715 * (y * y * y))))
        yield
        out["b"] = y * _sigmoid(_dot(y, prm["w_glu"][...]) + prm["b_glu"][...])

    running = [gdn_stages(), gla_stages(), ret_stages(), s5_stages()]
    while running:
        for gen in list(running):
            if next(gen, "done") == "done":
                running.remove(gen)

    mix = jnp.concatenate([out["a"], out["b"], out["c"], out["d"]], axis=1).astype(BF16)
    o_ref[...] = h + jnp.dot(mix, prm["w_out"][...], preferred_element_type=F32)


def _mixer_call(h, ys5, cos_t, sin_t, consts, params, layer):
    B, L, D = h.shape
    TL = MIX_TL

    def const(arr):
        nd = arr.ndim
        return pl.BlockSpec(arr.shape, lambda b, i: (0,) * nd, pipeline_mode=pl.Buffered(1))

    def tile(width):
        return pl.BlockSpec((None, TL, width), lambda b, i: (b, i, 0))

    c_list = [consts[n] for n in _CONST_ORDER]
    p_list = [params[n] for n in _PARAM_ORDER]
    return pl.pallas_call(
        _mixer_kernel,
        grid=(B, L // TL),
        in_specs=[tile(D), tile(GROUP_WIDTH), tile(128), tile(128)] + [const(a) for a in c_list]
        + [_layer_spec(a, layer, 2) for a in p_list],
        out_specs=tile(D),
        out_shape=jax.ShapeDtypeStruct((B, L, D), F32),
        scratch_shapes=[pltpu.VMEM((256, 128), F32), pltpu.VMEM((256, 256), F32), pltpu.VMEM((2, 128, 128), F32),
                        pltpu.VMEM((TL + 8, 3 * GROUP_WIDTH), F32)],
        compiler_params=pltpu.CompilerParams(dimension_semantics=("arbitrary", "arbitrary"),
                                             vmem_limit_bytes=VMEM_LIMIT),
        name="mixer",
    )(h, ys5, cos_t, sin_t, *c_list, *p_list)


def _ffn_kernel(h_ref, p_ref, gf_ref, wup_ref, wdn_ref, gp_ref, wg_ref, wp_ref, gl_ref, o_ref, act_ref, *,
                final):
    h = h_ref[...]
    hn = _rms(h, gf_ref[...]).astype(BF16)
    for s in range(FFN_HIDDEN // FFN_SLAB):
        lo = s * FFN_SLAB
        g = jnp.dot(hn, wup_ref[:, lo:lo + FFN_SLAB], preferred_element_type=F32)
        u = jnp.dot(hn, wup_ref[:, FFN_HIDDEN + lo:FFN_HIDDEN + lo + FFN_SLAB], preferred_element_type=F32)
        act_ref[:, lo:lo + FFN_SLAB] = (_silu(g) * u).astype(BF16)
    h2 = h + jnp.dot(act_ref[...], wdn_ref[...], preferred_element_type=F32)
    gate = _sigmoid(jnp.dot(_rms(h2, gp_ref[...]).astype(BF16), wg_ref[...], preferred_element_type=F32))
    h3 = h2 + jnp.dot(p_ref[...].astype(BF16), wp_ref[...], preferred_element_type=F32) * gate
    if final:
        h3 = _rms(h3, gl_ref[...])
    o_ref[...] = h3


def _ffn_call(h2d, p3d, layer, weights, final):
    T, D = h2d.shape
    TM = FFN_TM
    return pl.pallas_call(
        functools.partial(_ffn_kernel, final=final),
        grid=(T // TM,),
        in_specs=[pl.BlockSpec((TM, D), lambda i: (i, 0)),
                  pl.BlockSpec((None, TM, PLE_DIM), lambda i: (layer, i, 0))]
        + [_layer_spec(w, layer, 1) for w in weights],
        out_specs=pl.BlockSpec((TM, D), lambda i: (i, 0)),
        out_shape=jax.ShapeDtypeStruct((T, D), F32),
        scratch_shapes=[pltpu.VMEM((TM, FFN_HIDDEN), BF16)],
        compiler_params=pltpu.CompilerParams(dimension_semantics=("arbitrary",), vmem_limit_bytes=VMEM_LIMIT),
        name="ffn_ple",
    )(h2d, p3d, *weights)


def _row(v):
    return v[:, None, :]


def _mixer_params(norm_mix, w_in, w_out, gla_w_a2, gla_b_a, gla_norm, ret_norm, gdn_conv, gdn_a_log,
                  gdn_dt_bias, gdn_norm, s5_w_glu, s5_b_glu):
    depth = w_in.shape[0]

    def cols(seg):
        return w_in[:, :, seg[0]:seg[1]]

    small_pad = jnp.zeros((depth, D_MODEL, 128 - GLA_RANK - 2 * NUM_HEADS), F32)
    w_z = jnp.concatenate([cols(_AQ), cols(_AK), cols(_AV), cols(_AR), cols(_RQ), cols(_RK), cols(_RV),
                           cols(_RG), cols(_DQ), cols(_DK), cols(_DV), cols(_DG), cols(_ALOW), cols(_DB),
                           cols(_DA), small_pad], axis=2).astype(BF16)
    w_a2 = jnp.pad(gla_w_a2, ((0, 0), (0, 128 - GLA_RANK), (0, 0)))
    lane_pad = ((0, 0), (SM_A, 128 - SM_A - NUM_HEADS))
    return dict(norm_mix=_row(norm_mix), w_z=w_z, w_out=w_out.astype(BF16), w_a2=w_a2, b_a=_row(gla_b_a),
                gla_g=_row(jnp.tile(gla_norm, (1, NUM_HEADS))), ret_g=_row(ret_norm),
                gdn_g=_row(jnp.tile(gdn_norm, (1, NUM_HEADS))), conv_w=gdn_conv,
                alog_v=_row(jnp.pad(gdn_a_log, lane_pad)), dtb_v=_row(jnp.pad(gdn_dt_bias, lane_pad)),
                w_glu=s5_w_glu.astype(BF16), b_glu=_row(s5_b_glu))


def kernel(x, p, positions, norm_mix, w_in, w_out, gla_w_a2, gla_b_a, gla_norm, s5_lam_re, s5_lam_im, s5_log_dt,
           s5_b_re, s5_b_im, s5_c_re, s5_c_im, s5_d, s5_w_glu, s5_b_glu, ret_norm, gdn_conv, gdn_a_log,
           gdn_dt_bias, gdn_norm, norm_ffn, w_ffn_up, w_ffn_down, norm_ple, w_ple_gate, w_ple_proj, norm_final):
    B, L, D = x.shape
    depth = w_in.shape[0]
    assert D == D_MODEL and L % max(MIX_TL, S5_SUB * S5_ROWS) == 0 and (B * L) % FFN_TM == 0
    consts = {k: jnp.asarray(v) for k, v in _mixer_consts().items()}
    cos_t, sin_t = _rope_tables(positions)
    prep = _s5_prepare(s5_lam_re, s5_lam_im, s5_log_dt, s5_b_re, s5_b_im, s5_c_re, s5_c_im, s5_d)
    w_su = w_in[:, :, _SU[0]:_SU[1]].astype(BF16)
    params = _mixer_params(norm_mix, w_in, w_out, gla_w_a2, gla_b_a, gla_norm, ret_norm, gdn_conv,
                           gdn_a_log, gdn_dt_bias, gdn_norm, s5_w_glu, s5_b_glu)
    ffn_w = [_row(norm_ffn), w_ffn_up.astype(BF16), w_ffn_down.astype(BF16), _row(norm_ple),
             w_ple_gate.astype(BF16), w_ple_proj.astype(BF16),
             jnp.broadcast_to(norm_final[None, None, :], (depth, 1, D))]
    p3d = p.reshape(depth, B * L, PLE_DIM)
    h = x
    for i in range(depth):
        ys5 = _s5_call(h, i, _row(norm_mix), w_su, prep)
        h = _mixer_call(h, ys5, cos_t, sin_t, consts, params, i)
        h = _ffn_call(h.reshape(B * L, D), p3d, i, ffn_w, final=(i == depth - 1)).reshape(B, L, D)
    return h
```

```python
import functools
import math

import numpy as np
import jax
import jax.numpy as jnp
from jax import lax
from jax.experimental import pallas as pl
from jax.experimental.pallas import tpu as pltpu

F32 = jnp.float32
BF16 = jnp.bfloat16
HI = lax.Precision.HIGHEST

D_MODEL = 1024
PLE_DIM = 256
GROUP_WIDTH = 256
NUM_HEADS = 4
CHUNK = 64
EPS = 1e-6

GLA_DK = 32
GLA_RANK = 16
GLA_GATE_NORM = 16.0
S5_CH = 16
S5_GROUPS = 16
S5_STATE = 64
S5_SUB = 8
HEAD_DIM = 64
ROPE_BASE = 10000.0
GDN_CONV = 4
FFN_HIDDEN = 2816

MIX_TL = 256
S5_ROWS = 256
S5_PARTS = 2
FFN_TM = 512
FFN_SLAB = 256

VMEM_LIMIT = 56 * 1024 * 1024

_IN_OFFS = np.cumsum([0, 128, 128, 256, 16, 256, 256, 256, 256, 256, 256, 256, 256, 256, 4, 4, 256])
(_AQ, _AK, _AV, _ALOW, _AR, _SU, _RQ, _RK, _RV, _RG, _DQ, _DK, _DV, _DB, _DA, _DG) = [
    (int(_IN_OFFS[i]), int(_IN_OFFS[i + 1])) for i in range(16)]
Z_GLA = 0
Z_RET = 768
Z_GDN = 1792
Z_SMALL = 2816
Z_WIDTH = 2944
SM_BETA = 16
SM_A = 20


def _dot(a, b):
    return jnp.dot(a.astype(BF16), b.astype(BF16), preferred_element_type=F32)


def _dot_nt(a, b):
    return lax.dot_general(a.astype(BF16), b.astype(BF16), (((1,), (1,)), ((), ())),
                           preferred_element_type=F32)


def _split(x):
    hi = x.astype(BF16)
    return hi, (x - hi.astype(F32)).astype(BF16)


def _dot_exact_lhs(a, b):
    hi, lo = _split(b)
    n = b.shape[1]
    if n <= 128:
        r = jnp.dot(a.astype(BF16), jnp.concatenate([hi, lo], axis=1), preferred_element_type=F32)
        return r[:, :n] + r[:, n:]
    return (jnp.dot(a.astype(BF16), hi, preferred_element_type=F32)
            + jnp.dot(a.astype(BF16), lo, preferred_element_type=F32))


def _dot_exact_rhs(a, b):
    hi, lo = _split(a)
    m = a.shape[0]
    r = jnp.dot(jnp.concatenate([hi, lo], axis=0), b.astype(BF16), preferred_element_type=F32)
    return r[:m] + r[m:]


def _sigmoid(x):
    return 1.0 / (1.0 + jnp.exp(-x))


def _silu(x):
    return x * _sigmoid(x)


def _softplus(x):
    return jnp.maximum(x, 0.0) + jnp.log1p(jnp.exp(-jnp.abs(x)))


def _rms(x, g):
    return x * lax.rsqrt(jnp.mean(x * x, axis=-1, keepdims=True) + EPS) * g


def _stack4(x):
    return jnp.concatenate([x, x, x, x], axis=0)


def _rope_kernel(pos_ref, invf_ref, sign_ref, cos_ref, sin_ref):
    ang = pos_ref[...].astype(F32) * invf_ref[...]
    cos_ref[...] = jnp.cos(ang)
    sin_ref[...] = jnp.sin(ang) * sign_ref[...]


def _rope_tables(positions):
    B, L = positions.shape
    inv_freq = ROPE_BASE ** (-jnp.linspace(0.0, 1.0, HEAD_DIM // 2, dtype=F32))
    invf = jnp.tile(inv_freq, 4)[None, :]
    sign = np.where((np.arange(128) % HEAD_DIM) < HEAD_DIM // 2, -1.0, 1.0).astype(np.float32)[None, :]
    spec = pl.BlockSpec((None, L, 128), lambda b: (b, 0, 0))
    vec = pl.BlockSpec((1, 128), lambda b: (0, 0))
    return pl.pallas_call(
        _rope_kernel,
        grid=(B,),
        in_specs=[pl.BlockSpec((None, L, 1), lambda b: (b, 0, 0)), vec, vec],
        out_specs=[spec, spec],
        out_shape=[jax.ShapeDtypeStruct((B, L, 128), F32)] * 2,
        name="rope_tables",
    )(positions[:, :, None], invf, jnp.asarray(sign))


def _s5_kernel(h_ref, g_ref, wsu_ref, krev_ref, bbig_ref, cbig_ref, pwr_ref, pwi_ref, d8_ref,
               y_ref, cr_ref, ci_ref, su_scr, y_scr):
    n_state = S5_GROUPS * S5_STATE
    W = GROUP_WIDTH
    n_slab = W // 128

    @pl.when(pl.program_id(1) == 0)
    def _():
        cr_ref[...] = jnp.zeros_like(cr_ref)
        ci_ref[...] = jnp.zeros_like(ci_ref)

    R = S5_ROWS // S5_PARTS
    tok = R * S5_SUB
    n_sub = S5_SUB
    carry = {"r": cr_ref[...], "i": ci_ref[...], "parts_done": 0}
    rows = lax.broadcasted_iota(jnp.int32, (R, n_state), 0)
    first = rows == 0

    def shifted(x, shift):
        if shift % 8 == 0:
            return jnp.concatenate([jnp.zeros((shift, x.shape[1]), F32), x[:R - shift]], axis=0)
        return jnp.where(rows >= shift, pltpu.roll(x, shift, 0), 0.0)

    def part_stages(part):
        base = part * tok
        blk = tok // n_sub
        for q in range(n_sub):
            lo = base + q * blk
            su = _dot(_rms(h_ref[lo:lo + blk, :], g_ref[...]), wsu_ref[...])
            for s in range(n_slab):
                su_scr[s, lo:lo + blk, :] = su[:, s * 128:(s + 1) * 128]
            yield
        u8 = jnp.concatenate([su_scr[s, pl.ds(base + j, R, stride=S5_SUB), :]
                              for j in range(S5_SUB) for s in range(n_slab)], axis=1)
        u8b = u8.astype(BF16)
        y_loc, inc = [], []
        n_col = 2 * n_state // n_sub
        for t in range(n_sub):
            y_loc.append(jnp.dot(u8b[:, 0:(t + 1) * W], krev_ref[(S5_SUB - 1 - t) * W:S5_SUB * W, :],
                                 preferred_element_type=F32)
                         + d8_ref[:, t * W:(t + 1) * W] * u8[:, t * W:(t + 1) * W])
            inc.append(jnp.dot(u8b, bbig_ref[:, t * n_col:(t + 1) * n_col], preferred_element_type=F32))
            yield
        inc = jnp.concatenate(inc, axis=1)
        xr, xi = inc[:, :n_state], inc[:, n_state:]
        assert carry["parts_done"] == part
        cr, ci = carry["r"], carry["i"]
        ar0, ai0 = pwr_ref[0:1, :], pwi_ref[0:1, :]
        xr = xr + jnp.where(first, ar0 * cr - ai0 * ci, 0.0)
        xi = xi + jnp.where(first, ar0 * ci + ai0 * cr, 0.0)
        shift, lvl = 1, 0
        while shift < R:
            ar, ai = pwr_ref[lvl:lvl + 1, :], pwi_ref[lvl:lvl + 1, :]
            sr, si = shifted(xr, shift), shifted(xi, shift)
            xr, xi = xr + ar * sr - ai * si, xi + ar * si + ai * sr
            shift, lvl = shift * 2, lvl + 1
            yield
        pr = jnp.where(first, cr, pltpu.roll(xr, 1, 0))
        pi = jnp.where(first, ci, pltpu.roll(xi, 1, 0))
        carry.update(r=xr[R - 1:R, :], i=xi[R - 1:R, :], parts_done=part + 1)
        xprev = jnp.concatenate([pr, pi], axis=1).astype(BF16)
        yield
        for j in range(S5_SUB):
            yj = y_loc[j] + jnp.dot(xprev, cbig_ref[:, j * W:(j + 1) * W], preferred_element_type=F32)
            for s in range(n_slab):
                y_scr[s, pl.ds(base + j, R, stride=S5_SUB), :] = yj[:, s * 128:(s + 1) * 128]
            yield
        y_ref[base:base + tok, :] = jnp.concatenate([y_scr[s, base:base + tok, :] for s in range(n_slab)],
                                                    axis=1)

    gens = [part_stages(part) for part in range(S5_PARTS)]
    for lead in range(S5_PARTS):
        for _ in range(n_sub):
            for gen in gens[:lead + 1]:
                next(gen, None)
    live = list(gens)
    while live:
        for gen in list(live):
            if next(gen, "done") == "done":
                live.remove(gen)
    cr_ref[...] = carry["r"]
    ci_ref[...] = carry["i"]


def _s5_prepare(lam_re, lam_im, log_dt, b_re, b_im, c_re, c_im, d_skip):
    G, P, H, M = S5_GROUPS, S5_STATE, S5_CH, S5_SUB
    depth = lam_re.shape[0]
    lr = jnp.minimum(lam_re.astype(F32), -1e-4)
    li = lam_im.astype(F32)
    dt = jnp.exp(log_dt.astype(F32))[:, :, None]

    def apow(t):
        tt = jnp.asarray(t, F32)[None, :, None, None]
        mag = jnp.exp((lr * dt)[:, None] * tt)
        ang = (li * dt)[:, None] * tt
        return mag * jnp.cos(ang), mag * jnp.sin(ang)

    n_lvl = int(math.log2(S5_ROWS))
    exps = list(range(M + 1)) + list(range(M - 1, -1, -1)) + [M * 2 ** s for s in range(n_lvl)]
    pw_r, pw_i = apow(np.asarray(exps, np.float32))
    rev_r, rev_i = pw_r[:, M + 1:2 * M + 1], pw_i[:, M + 1:2 * M + 1]
    ar, ai = pw_r[:, 1], pw_i[:, 1]
    nr, ni = ar - 1.0, ai
    den = lr * lr + li * li
    fr = ((nr * lr + ni * li) / den)[..., None]
    fi = ((ni * lr - nr * li) / den)[..., None]
    bbr = fr * b_re - fi * b_im
    bbi = fr * b_im + fi * b_re

    def embed(narrow, tile, row_div, col_div):
        wide = jnp.einsum('dnk,km->dnm', narrow.astype(BF16), jnp.asarray(tile, BF16),
                          preferred_element_type=F32)
        rg = (lax.broadcasted_iota(jnp.int32, wide.shape, 1) // row_div) % G
        cg = (lax.broadcasted_iota(jnp.int32, wide.shape, 2) // col_div) % G
        return jnp.where(rg == cg, wide, 0.0).astype(BF16)

    pr, pi = rev_r[:, :, :, None, :], rev_i[:, :, :, None, :]
    car = c_re[:, None] * pr - c_im[:, None] * pi
    cai = c_re[:, None] * pi + c_im[:, None] * pr
    bbr_t = bbr.transpose(0, 1, 3, 2)[:, None, :, :, None, :]
    bbi_t = bbi.transpose(0, 1, 3, 2)[:, None, :, :, None, :]
    kt = jnp.sum(car[:, :, :, None] * bbr_t - cai[:, :, :, None] * bbi_t, axis=-1)
    k_narrow = kt.reshape(depth, M * G * H, H)
    k_tile = np.tile(np.eye(H, dtype=np.float32), (1, G))
    krev = embed(k_narrow, k_tile, H, H)

    pr, pi = rev_r[..., None], rev_i[..., None]
    b_narrow = jnp.stack([pr * bbr[:, None] - pi * bbi[:, None], pr * bbi[:, None] + pi * bbr[:, None]],
                         axis=1)
    b_narrow = b_narrow.transpose(0, 2, 3, 5, 1, 4).reshape(depth, M * G * H, 2 * P)
    cp = np.arange(2 * P)
    col = np.arange(2 * G * P)
    b_tile = ((cp[:, None] // P == (col // (G * P))[None, :]) & (cp[:, None] % P == (col % P)[None, :]))
    bbig = embed(b_narrow, b_tile.astype(np.float32), H, P)

    pr, pi = pw_r[:, 1:M + 1, :, None, :], pw_i[:, 1:M + 1, :, None, :]
    qr = c_re[:, None] * pr - c_im[:, None] * pi
    qi = -(c_re[:, None] * pi + c_im[:, None] * pr)
    c_narrow = jnp.stack([qr, qi], axis=1).transpose(0, 1, 3, 5, 2, 4).reshape(depth, 2 * G * P, M * H)
    th = np.arange(M * H)
    col = np.arange(M * G * H)
    c_tile = ((th[:, None] // H == (col // (G * H))[None, :]) & (th[:, None] % H == (col % H)[None, :]))
    cbig = embed(c_narrow, c_tile.astype(np.float32), P, H)

    pad = (-n_lvl) % 8
    pwr = jnp.pad(pw_r[:, 2 * M + 1:].reshape(depth, n_lvl, G * P), ((0, 0), (0, pad), (0, 0)))
    pwi = jnp.pad(pw_i[:, 2 * M + 1:].reshape(depth, n_lvl, G * P), ((0, 0), (0, pad), (0, 0)))
    d8 = jnp.tile(d_skip.astype(F32), (1, M))[:, None, :]
    return krev, bbig, cbig, pwr, pwi, d8


def _layer_spec(arr, layer, n_grid):
    nd = arr.ndim - 1
    zeros = (0,) * nd
    if n_grid == 1:
        return pl.BlockSpec((None,) + arr.shape[1:], lambda i: (layer,) + zeros, pipeline_mode=pl.Buffered(1))
    return pl.BlockSpec((None,) + arr.shape[1:], lambda b, i: (layer,) + zeros, pipeline_mode=pl.Buffered(1))


def _s5_call(h, layer, norm_g, w_su, prep):
    B, L, D = h.shape
    krev, bbig, cbig, pwr, pwi, d8 = prep
    tokens = S5_SUB * S5_ROWS
    n_state = S5_GROUPS * S5_STATE
    n_slab = GROUP_WIDTH // 128
    params = [norm_g, w_su, krev, bbig, cbig, pwr, pwi, d8]

    return pl.pallas_call(
        _s5_kernel,
        grid=(B, L // tokens),
        in_specs=[pl.BlockSpec((None, tokens, D), lambda b, i: (b, i, 0))]
        + [_layer_spec(a, layer, 2) for a in params],
        out_specs=pl.BlockSpec((None, tokens, GROUP_WIDTH), lambda b, i: (b, i, 0)),
        out_shape=jax.ShapeDtypeStruct((B, L, GROUP_WIDTH), F32),
        scratch_shapes=[pltpu.VMEM((1, n_state), F32), pltpu.VMEM((1, n_state), F32),
                        pltpu.VMEM((n_slab, tokens, 128), F32), pltpu.VMEM((n_slab, tokens, 128), F32)],
        compiler_params=pltpu.CompilerParams(dimension_semantics=("arbitrary", "arbitrary"),
                                             vmem_limit_bytes=VMEM_LIMIT),
        name="s5_mixer",
    )(h, *params)


def _mixer_consts():
    TL, C, H = MIX_TL, CHUNK, NUM_HEADS
    t = np.arange(TL)
    ltri = ((t[:, None] // C == t[None, :] // C) & (t[None, :] <= t[:, None])).astype(np.float32)
    lane256 = np.arange(256)
    lane128 = np.arange(128)
    hm128 = (lane128[None, :] // GLA_DK == np.arange(H)[:, None]).astype(np.float32)
    hm256 = (lane256[None, :] // HEAD_DIM == np.arange(H)[:, None]).astype(np.float32)
    bm = (lane256[:, None] // HEAD_DIM == lane256[None, :] // HEAD_DIM).astype(np.float32)
    bmt = (lane256[:, None] // HEAD_DIM == lane128[None, :] // GLA_DK).astype(np.float32)
    c = np.arange(C)
    caus4 = np.tile((c[:, None] >= c[None, :]).astype(np.float32), (H, 1))
    s_lane = lane128 % HEAD_DIM
    bm2 = bm[0:128, 0:128]
    incl = (c[:, None] >= s_lane[None, :]).astype(np.float32)
    strict = (c[:, None] > s_lane[None, :]).astype(np.float32)
    idiag = (c[:, None] == s_lane[None, :]).astype(np.float32)
    log_gamma = np.log1p(-(2.0 ** (-5.0 - np.arange(H, dtype=np.float32)))).astype(np.float32)
    idx = np.arange(TL, dtype=np.float32)
    rel = idx[:, None] - idx[None, :]
    dmask = np.where(rel >= 0, np.exp(np.maximum(rel, 0.0)[None] * log_gamma[:, None, None]), 0.0)
    dmask = dmask.reshape(H * TL, TL).astype(np.float32)
    lg_lane = log_gamma[lane256 // HEAD_DIM]
    xi = np.exp((idx[:, None] + 1.0) * lg_lane[None, :]).astype(np.float32)
    zeta = np.exp((TL - 1.0 - idx[:, None]) * lg_lane[None, :]).astype(np.float32)
    gchunk = np.exp(TL * lg_lane)[None, :].astype(np.float32)
    e_beta = np.zeros((128, 256), np.float32)
    e_a = np.zeros((128, 256), np.float32)
    for h in range(H):
        e_beta[SM_BETA + h, h * HEAD_DIM:(h + 1) * HEAD_DIM] = 1.0
        e_a[SM_A + h, h * HEAD_DIM:(h + 1) * HEAD_DIM] = 1.0
    return dict(ltri=ltri, hm128=hm128, hm256=hm256, bm=bm, bmt=bmt, caus4=caus4, incl=incl, strict=strict,
                idiag=idiag, bm2=bm2, dmask=dmask, xi=xi, zeta=zeta, gchunk=gchunk, e_beta=e_beta, e_a=e_a,
                m64=bm / HEAD_DIM)


_CONST_ORDER = ("ltri", "hm128", "hm256", "bm", "bmt", "caus4", "incl", "strict", "idiag", "bm2", "dmask", "xi",
                "zeta", "gchunk", "e_beta", "e_a", "m64")
_PARAM_ORDER = ("norm_mix", "w_z", "w_out", "w_a2", "b_a", "gla_g", "ret_g", "gdn_g", "conv_w", "alog_v",
                "dtb_v", "w_glu", "b_glu")


def _mixer_kernel(*refs):
    n_c, n_p = len(_CONST_ORDER), len(_PARAM_ORDER)
    h_ref, ys5_ref, cos_ref, sin_ref = refs[:4]
    cst = dict(zip(_CONST_ORDER, refs[4:4 + n_c]))
    prm = dict(zip(_PARAM_ORDER, refs[4 + n_c:4 + n_c + n_p]))
    o_ref = refs[4 + n_c + n_p]
    st_gla, s_ret, s_gdn, xpad = refs[4 + n_c + n_p + 1:]
    TL, C, H = MIX_TL, CHUNK, NUM_HEADS

    @pl.when(pl.program_id(1) == 0)
    def _():
        st_gla[...] = jnp.zeros_like(st_gla)
        s_ret[...] = jnp.zeros_like(s_ret)
        s_gdn[...] = jnp.zeros_like(s_gdn)
        xpad[0:8, :] = jnp.zeros((8, 3 * GROUP_WIDTH), F32)

    h = h_ref[...]
    hn = _rms(h, prm["norm_mix"][...]).astype(BF16)
    w_z = prm["w_z"]

    def proj(lo, hi):
        return jnp.dot(hn, w_z[:, lo:hi], preferred_element_type=F32)

    hm128 = cst["hm128"][...]
    hm256 = cst["hm256"][...]
    bm = cst["bm"][...]
    m64 = cst["m64"][...]
    ltri = cst["ltri"][...]
    sm = proj(Z_SMALL, Z_WIDTH)

    def take_heads(stacked, rows):
        out = stacked[0:rows] * hm256[0:1]
        for hh in range(1, H):
            out = out + stacked[hh * rows:(hh + 1) * rows] * hm256[hh:hh + 1]
        return out

    out = {}

    def gdn_stages():
        zd = proj(Z_GDN, Z_GDN + 1024)
        xpad[8:8 + TL, :] = zd[:, 0:768]
        cw = prm["conv_w"][...]
        xc = cw[GDN_CONV - 1:GDN_CONV, :] * zd[:, 0:768]
        for j in range(GDN_CONV - 1):
            xc = xc + cw[j:j + 1, :] * xpad[5 + j:5 + j + TL, :]
        xpad[0:8, :] = zd[TL - 8:TL, 0:768]
        xc = _silu(xc)
        dq, dk, dv = xc[:, 0:256], xc[:, 256:512], xc[:, 512:768]
        yield
        dq = dq * lax.rsqrt(_dot(dq * dq, bm) + EPS) * (HEAD_DIM ** -0.5)
        dk = dk * lax.rsqrt(_dot(dk * dk, bm) + EPS)
        beta = _dot_exact_rhs(_sigmoid(sm), cst["e_beta"][...])
        g_log = _dot_exact_rhs(-jnp.exp(prm["alog_v"][...]) * _softplus(sm + prm["dtb_v"][...]),
                               cst["e_a"][...])
        yield
        gcum = _dot_exact_lhs(ltri, g_log)
        yield
        incl = cst["incl"][...]
        strict = cst["strict"][...]
        idiag = cst["idiag"][...]
        bm2 = cst["bm2"][...]
        bm2b = bm2.astype(BF16)
        n_pair = NUM_HEADS // 2
        probs = [(c, p) for c in range(TL // C) for p in range(n_pair)]

        def blk(x, c, p):
            return x[c * C:(c + 1) * C, p * 128:(p + 1) * 128]

        def bd2(x):
            xb = x.astype(BF16)
            return jnp.concatenate([xb, xb], axis=0) * bm2b

        def bdot(a, b):
            return jnp.dot(a.astype(BF16), b, preferred_element_type=F32)

        gk_, gq_, gv_, gb_, gg_, p_, t_, aqk_ = {}, {}, {}, {}, {}, {}, {}, {}
        for cp in probs:
            k_c, q_c, g_c = blk(dk, *cp), blk(dq, *cp), blk(gcum, *cp)
            gk_[cp], gq_[cp], gv_[cp], gb_[cp], gg_[cp] = k_c, q_c, blk(dv, *cp), blk(beta, *cp), g_c
            g_row = jnp.sum(g_c * idiag, axis=0, keepdims=True)
            dec = jnp.where(incl > 0.0, jnp.exp(jnp.where(incl > 0.0, g_c - g_row, 0.0)), 0.0)
            kq = lax.dot_general(jnp.concatenate([k_c, q_c], axis=0).astype(BF16), bd2(k_c),
                                 (((1,), (1,)), ((), ())), preferred_element_type=F32)
            aqk_[cp] = kq[C:2 * C] * dec
            p_[cp] = -(kq[0:C] * dec * strict * gb_[cp])
            t_[cp] = idiag + p_[cp]
        yield
        for cp in probs:
            p_[cp] = bdot(p_[cp], bd2(p_[cp]))
        yield
        for _ in range(int(math.log2(C)) - 2):
            for cp in probs:
                pt = bdot(jnp.concatenate([p_[cp], t_[cp]], axis=0), bd2(p_[cp]))
                p_[cp] = pt[0:C]
                t_[cp] = t_[cp] + pt[C:2 * C]
            yield
        for cp in probs:
            t_[cp] = t_[cp] + bdot(t_[cp], bd2(p_[cp]))
        yield
        uw_, eg_ = {}, {}
        for cp in probs:
            eg_[cp] = jnp.exp(gg_[cp])
            b_c = gb_[cp]
            uw_[cp] = bdot(t_[cp], jnp.concatenate([bd2(gv_[cp] * b_c), bd2(gk_[cp] * b_c * eg_[cp])],
                                                   axis=1))
        yield
        mm_, nn_, dl_, qp_, op_ = {}, {}, {}, {}, {}
        for cp in probs:
            k_c, g_c = gk_[cp], gg_[cp]
            g_last = g_c[C - 1:C, :]
            u_c, w_c = uw_[cp][:, 0:128], uw_[cp][:, 128:256]
            k_end_t = (k_c * jnp.exp(g_last - g_c)).T
            mn = bdot(k_end_t, jnp.concatenate([w_c, u_c], axis=1).astype(BF16))
            mm_[cp] = bm2 * mn[:, 0:128]
            nn_[cp] = bm2 * mn[:, 128:256]
            dl_[cp] = jnp.exp(g_last)
            qo = bdot(aqk_[cp], jnp.concatenate([bd2(w_c), bd2(u_c)], axis=1))
            qp_[cp] = gq_[cp] * eg_[cp] - qo[:, 0:128]
            op_[cp] = qo[:, 128:256]
        yield
        o_rows = []
        sgp = [s_gdn[p] for p in range(n_pair)]
        for c in range(TL // C):
            o_pair = []
            for p in range(n_pair):
                cp = (c, p)
                s_b = sgp[p].astype(BF16)
                o_pair.append(bdot(qp_[cp], s_b) + op_[cp])
                sgp[p] = sgp[p] * dl_[cp] - bdot(mm_[cp], s_b) + nn_[cp]
            o_rows.append(jnp.concatenate(o_pair, axis=1))
            yield
        for p in range(n_pair):
            s_gdn[p] = sgp[p]
        o_d = jnp.concatenate(o_rows, axis=0)
        out["d"] = o_d * lax.rsqrt(_dot(o_d * o_d, m64) + EPS) * prm["gdn_g"][...] * _silu(zd[:, 768:1024])

    def gla_stages():
        zg = proj(Z_GLA, Z_GLA + 768)
        q, k, v, r = zg[:, 0:128], zg[:, 128:256], zg[:, 256:512], zg[:, 512:768]
        gk = _dot(sm, prm["w_a2"][...]) + prm["b_a"][...]
        gk = (jnp.minimum(gk, 0.0) - jnp.log1p(jnp.exp(-jnp.abs(gk)))) / GLA_GATE_NORM
        yield
        bcum = _dot_exact_lhs(ltri, gk)
        caus4 = cst["caus4"][...]
        bmt = cst["bmt"][...]
        st = st_gla[...]
        yield
        o_chunks = []
        for c in range(TL // C):
            sl = slice(c * C, (c + 1) * C)
            b_c = bcum[sl]
            b_last = b_c[C - 1:C, :]
            q_t = q[sl] * jnp.exp(b_c) * (GLA_DK ** -0.5)
            k_t = k[sl] * jnp.exp(-b_c)
            k_end = k[sl] * jnp.exp(b_last - b_c)
            v_c = v[sl]
            qs = jnp.concatenate([q_t * hm128[hh:hh + 1] for hh in range(H)], axis=0)
            att = _dot_nt(qs, k_t) * caus4
            cross = _dot_nt(q_t, st)
            st = st * jnp.exp(b_last) + bmt * _dot(v_c.T, k_end)
            yield
            o_chunks.append(take_heads(_dot(att, v_c), C) + cross)
            yield
        st_gla[...] = st
        o_a = jnp.concatenate(o_chunks, axis=0)
        out["a"] = o_a * lax.rsqrt(_dot(o_a * o_a, m64) + EPS) * prm["gla_g"][...] * _silu(r)

    def ret_stages():
        zr = proj(Z_RET, Z_RET + 1024)
        cos2 = jnp.concatenate([cos_ref[...], cos_ref[...]], axis=1)
        sin2 = jnp.concatenate([sin_ref[...], sin_ref[...]], axis=1)
        first_half = (lax.broadcasted_iota(jnp.int32, (TL, 256), 1) % HEAD_DIM) < HEAD_DIM // 2

        def rope(x):
            rot = jnp.where(first_half, pltpu.roll(x, 256 - HEAD_DIM // 2, 1),
                            pltpu.roll(x, HEAD_DIM // 2, 1))
            return x * cos2 + rot * sin2

        rq = rope(zr[:, 0:256])
        rk = rope(zr[:, 256:512]) * (HEAD_DIM ** -0.5)
        rv = zr[:, 512:768]
        rg = zr[:, 768:1024]
        yield
        qs = jnp.concatenate([rq * hm256[hh:hh + 1] for hh in range(H)], axis=0)
        sc = _dot_nt(qs, rk) * cst["dmask"][...]
        s_prev = s_ret[...]
        cross = _dot(rq, s_prev) * cst["xi"][...]
        s_ret[...] = s_prev * cst["gchunk"][...] + bm * _dot((rk * cst["zeta"][...]).T, rv)
        yield
        o_c = take_heads(_dot(sc, rv), TL) + cross
        yield
        o_c = o_c - _dot_exact_rhs(o_c, m64)
        yield
        out["c"] = o_c * lax.rsqrt(_dot(o_c * o_c, m64) + EPS) * prm["ret_g"][...] * _silu(rg)

    def s5_stages():
        y = ys5_ref[...]
        y = 0.5 * y * (1.0 + jnp.tanh(math.sqrt(2.0 / math.pi) * (y + 0.044715 * (y * y * y))))
        yield
        out["b"] = y * _sigmoid(_dot(y, prm["w_glu"][...]) + prm["b_glu"][...])

    running = [gdn_stages(), gla_stages(), ret_stages(), s5_stages()]
    while running:
        for gen in list(running):
            if next(gen, "done") == "done":
                running.remove(gen)

    mix = jnp.concatenate([out["a"], out["b"], out["c"], out["d"]], axis=1).astype(BF16)
    o_ref[...] = h + jnp.dot(mix, prm["w_out"][...], preferred_element_type=F32)


def _mixer_call(h, ys5, cos_t, sin_t, consts, params, layer):
    B, L, D = h.shape
    TL = MIX_TL

    def const(arr):
        nd = arr.ndim
        return pl.BlockSpec(arr.shape, lambda b, i: (0,) * nd, pipeline_mode=pl.Buffered(1))

    def tile(width):
        return pl.BlockSpec((None, TL, width), lambda b, i: (b, i, 0))

    c_list = [consts[n] for n in _CONST_ORDER]
    p_list = [params[n] for n in _PARAM_ORDER]
    return pl.pallas_call(
        _mixer_kernel,
        grid=(B, L // TL),
        in_specs=[tile(D), tile(GROUP_WIDTH), tile(128), tile(128)] + [const(a) for a in c_list]
        + [_layer_spec(a, layer, 2) for a in p_list],
        out_specs=tile(D),
        out_shape=jax.ShapeDtypeStruct((B, L, D), F32),
        scratch_shapes=[pltpu.VMEM((256, 128), F32), pltpu.VMEM((256, 256), F32), pltpu.VMEM((2, 128, 128), F32),
                        pltpu.VMEM((TL + 8, 3 * GROUP_WIDTH), F32)],
        compiler_params=pltpu.CompilerParams(dimension_semantics=("arbitrary", "arbitrary"),
                                             vmem_limit_bytes=VMEM_LIMIT),
        name="mixer",
    )(h, ys5, cos_t, sin_t, *c_list, *p_list)


def _ffn_kernel(h_ref, p_ref, gf_ref, wup_ref, wdn_ref, gp_ref, wg_ref, wp_ref, gl_ref, o_ref, act_ref, *,
                final):
    h = h_ref[...]
    hn = _rms(h, gf_ref[...]).astype(BF16)
    for s in range(FFN_HIDDEN // FFN_SLAB):
        lo = s * FFN_SLAB
        g = jnp.dot(hn, wup_ref[:, lo:lo + FFN_SLAB], preferred_element_type=F32)
        u = jnp.dot(hn, wup_ref[:, FFN_HIDDEN + lo:FFN_HIDDEN + lo + FFN_SLAB], preferred_element_type=F32)
        act_ref[:, lo:lo + FFN_SLAB] = (_silu(g) * u).astype(BF16)
    h2 = h + jnp.dot(act_ref[...], wdn_ref[...], preferred_element_type=F32)
    gate = _sigmoid(jnp.dot(_rms(h2, gp_ref[...]).astype(BF16), wg_ref[...], preferred_element_type=F32))
    h3 = h2 + jnp.dot(p_ref[...].astype(BF16), wp_ref[...], preferred_element_type=F32) * gate
    if final:
        h3 = _rms(h3, gl_ref[...])
    o_ref[...] = h3


def _ffn_call(h2d, p3d, layer, weights, final):
    T, D = h2d.shape
    TM = FFN_TM
    return pl.pallas_call(
        functools.partial(_ffn_kernel, final=final),
        grid=(T // TM,),
        in_specs=[pl.BlockSpec((TM, D), lambda i: (i, 0)),
                  pl.BlockSpec((None, TM, PLE_DIM), lambda i: (layer, i, 0))]
        + [_layer_spec(w, layer, 1) for w in weights],
        out_specs=pl.BlockSpec((TM, D), lambda i: (i, 0)),
        out_shape=jax.ShapeDtypeStruct((T, D), F32),
        scratch_shapes=[pltpu.VMEM((TM, FFN_HIDDEN), BF16)],
        compiler_params=pltpu.CompilerParams(dimension_semantics=("arbitrary",), vmem_limit_bytes=VMEM_LIMIT),
        name="ffn_ple",
    )(h2d, p3d, *weights)


def _row(v):
    return v[:, None, :]


def _mixer_params(norm_mix, w_in, w_out, gla_w_a2, gla_b_a, gla_norm, ret_norm, gdn_conv, gdn_a_log,
                  gdn_dt_bias, gdn_norm, s5_w_glu, s5_b_glu):
    depth = w_in.shape[0]

    def cols(seg):
        return w_in[:, :, seg[0]:seg[1]]

    small_pad = jnp.zeros((depth, D_MODEL, 128 - GLA_RANK - 2 * NUM_HEADS), F32)
    w_z = jnp.concatenate([cols(_AQ), cols(_AK), cols(_AV), cols(_AR), cols(_RQ), cols(_RK), cols(_RV),
                           cols(_RG), cols(_DQ), cols(_DK), cols(_DV), cols(_DG), cols(_ALOW), cols(_DB),
                           cols(_DA), small_pad], axis=2).astype(BF16)
    w_a2 = jnp.pad(gla_w_a2, ((0, 0), (0, 128 - GLA_RANK), (0, 0)))
    lane_pad = ((0, 0), (SM_A, 128 - SM_A - NUM_HEADS))
    return dict(norm_mix=_row(norm_mix), w_z=w_z, w_out=w_out.astype(BF16), w_a2=w_a2, b_a=_row(gla_b_a),
                gla_g=_row(jnp.tile(gla_norm, (1, NUM_HEADS))), ret_g=_row(ret_norm),
                gdn_g=_row(jnp.tile(gdn_norm, (1, NUM_HEADS))), conv_w=gdn_conv,
                alog_v=_row(jnp.pad(gdn_a_log, lane_pad)), dtb_v=_row(jnp.pad(gdn_dt_bias, lane_pad)),
                w_glu=s5_w_glu.astype(BF16), b_glu=_row(s5_b_glu))


def kernel(x, p, positions, norm_mix, w_in, w_out, gla_w_a2, gla_b_a, gla_norm, s5_lam_re, s5_lam_im, s5_log_dt,
           s5_b_re, s5_b_im, s5_c_re, s5_c_im, s5_d, s5_w_glu, s5_b_glu, ret_norm, gdn_conv, gdn_a_log,
           gdn_dt_bias, gdn_norm, norm_ffn, w_ffn_up, w_ffn_down, norm_ple, w_ple_gate, w_ple_proj, norm_final):
    B, L, D = x.shape
    depth = w_in.shape[0]
    assert D == D_MODEL and L % max(MIX_TL, S5_SUB * S5_ROWS) == 0 and (B * L) % FFN_TM == 0
    consts = {k: jnp.asarray(v) for k, v in _mixer_consts().items()}
    cos_t, sin_t = _rope_tables(positions)
    prep = _s5_prepare(s5_lam_re, s5_lam_im, s5_log_dt, s5_b_re, s5_b_im, s5_c_re, s5_c_im, s5_d)
    w_su = w_in[:, :, _SU[0]:_SU[1]].astype(BF16)
    params = _mixer_params(norm_mix, w_in, w_out, gla_w_a2, gla_b_a, gla_norm, ret_norm, gdn_conv,
                           gdn_a_log, gdn_dt_bias, gdn_norm, s5_w_glu, s5_b_glu)
    ffn_w = [_row(norm_ffn), w_ffn_up.astype(BF16), w_ffn_down.astype(BF16), _row(norm_ple),
             w_ple_gate.astype(BF16), w_ple_proj.astype(BF16),
             jnp.broadcast_to(norm_final[None, None, :], (depth, 1, D))]
    p3d = p.reshape(depth, B * L, PLE_DIM)
    h = x
    for i in range(depth):
        ys5 = _s5_call(h, i, _row(norm_mix), w_su, prep)
        h = _mixer_call(h, ys5, cos_t, sin_t, consts, params, i)
        h = _ffn_call(h.reshape(B * L, D), p3d, i, ffn_w, final=(i == depth - 1)).reshape(B, L, D)
    return h
```

```python
import functools
import math

import numpy as np
import jax
import jax.numpy as jnp
from jax import lax
from jax.experimental import pallas as pl
from jax.experimental.pallas import tpu as pltpu

F32 = jnp.float32
BF16 = jnp.bfloat16
HI = lax.Precision.HIGHEST

D_MODEL = 1024
PLE_DIM = 256
GROUP_WIDTH = 256
NUM_HEADS = 4
CHUNK = 64
EPS = 1e-6

GLA_DK = 32
GLA_RANK = 16
GLA_GATE_NORM = 16.0
S5_CH = 16
S5_GROUPS = 16
S5_STATE = 64
S5_SUB = 8
HEAD_DIM = 64
ROPE_BASE = 10000.0
GDN_CONV = 4
FFN_HIDDEN = 2816

MIX_TL = 256
S5_ROWS = 256
S5_PARTS = 2
FFN_TM = 1024
FFN_SLAB = 256

VMEM_LIMIT = 56 * 1024 * 1024

_IN_OFFS = np.cumsum([0, 128, 128, 256, 16, 256, 256, 256, 256, 256, 256, 256, 256, 256, 4, 4, 256])
(_AQ, _AK, _AV, _ALOW, _AR, _SU, _RQ, _RK, _RV, _RG, _DQ, _DK, _DV, _DB, _DA, _DG) = [
    (int(_IN_OFFS[i]), int(_IN_OFFS[i + 1])) for i in range(16)]
Z_GLA = 0
Z_RET = 768
Z_GDN = 1792
Z_SMALL = 2816
Z_WIDTH = 2944
SM_BETA = 16
SM_A = 20


def _dot(a, b):
    return jnp.dot(a.astype(BF16), b.astype(BF16), preferred_element_type=F32)


def _dot_nt(a, b):
    return lax.dot_general(a.astype(BF16), b.astype(BF16), (((1,), (1,)), ((), ())),
                           preferred_element_type=F32)


def _split(x):
    hi = x.astype(BF16)
    return hi, (x - hi.astype(F32)).astype(BF16)


def _dot_exact_lhs(a, b):
    hi, lo = _split(b)
    n = b.shape[1]
    if n <= 128:
        r = jnp.dot(a.astype(BF16), jnp.concatenate([hi, lo], axis=1), preferred_element_type=F32)
        return r[:, :n] + r[:, n:]
    return (jnp.dot(a.astype(BF16), hi, preferred_element_type=F32)
            + jnp.dot(a.astype(BF16), lo, preferred_element_type=F32))


def _dot_exact_rhs(a, b):
    hi, lo = _split(a)
    m = a.shape[0]
    r = jnp.dot(jnp.concatenate([hi, lo], axis=0), b.astype(BF16), preferred_element_type=F32)
    return r[:m] + r[m:]


def _sigmoid(x):
    return 1.0 / (1.0 + jnp.exp(-x))


def _silu(x):
    return x * _sigmoid(x)


def _softplus(x):
    return jnp.maximum(x, 0.0) + jnp.log1p(jnp.exp(-jnp.abs(x)))


def _rms(x, g):
    return x * lax.rsqrt(jnp.mean(x * x, axis=-1, keepdims=True) + EPS) * g


def _stack4(x):
    return jnp.concatenate([x, x, x, x], axis=0)


def _rope_kernel(pos_ref, invf_ref, cos_ref, sin_ref):
    ang = invf_ref[...] * pos_ref[...].astype(F32)
    c = jnp.cos(ang)
    s = jnp.sin(ang)
    cos_ref[...] = jnp.concatenate([c, c, c, c], axis=0).T
    sin_ref[...] = jnp.concatenate([-s, s, -s, s], axis=0).T


def _rope_tables(positions):
    B, L = positions.shape
    inv_freq = ROPE_BASE ** (-jnp.linspace(0.0, 1.0, HEAD_DIM // 2, dtype=F32))
    spec = pl.BlockSpec((None, L, 128), lambda b: (b, 0, 0))
    return pl.pallas_call(
        _rope_kernel,
        grid=(B,),
        in_specs=[pl.BlockSpec((None, 1, L), lambda b: (b, 0, 0)),
                  pl.BlockSpec((HEAD_DIM // 2, 1), lambda b: (0, 0))],
        out_specs=[spec, spec],
        out_shape=[jax.ShapeDtypeStruct((B, L, 128), F32)] * 2,
        name="rope_tables",
    )(positions[:, None, :], inv_freq[:, None])


def _s5_kernel(h_ref, g_ref, wsu_ref, krev_ref, bbig_ref, cbig_ref, pwr_ref, pwi_ref, d8_ref,
               y_ref, cr_ref, ci_ref, su_scr, y_scr):
    n_state = S5_GROUPS * S5_STATE
    W = GROUP_WIDTH
    n_slab = W // 128

    @pl.when(pl.program_id(1) == 0)
    def _():
        cr_ref[...] = jnp.zeros_like(cr_ref)
        ci_ref[...] = jnp.zeros_like(ci_ref)

    R = S5_ROWS // S5_PARTS
    tok = R * S5_SUB
    n_sub = S5_SUB
    carry = {"r": cr_ref[...], "i": ci_ref[...], "parts_done": 0}
    rows = lax.broadcasted_iota(jnp.int32, (R, n_state), 0)
    first = rows == 0

    def shifted(x, shift):
        if shift % 8 == 0:
            return jnp.concatenate([jnp.zeros((shift, x.shape[1]), F32), x[:R - shift]], axis=0)
        return jnp.where(rows >= shift, pltpu.roll(x, shift, 0), 0.0)

    def part_stages(part):
        base = part * tok
        blk = tok // n_sub
        for q in range(n_sub):
            lo = base + q * blk
            su = _dot(_rms(h_ref[lo:lo + blk, :], g_ref[...]), wsu_ref[...])
            for s in range(n_slab):
                su_scr[s, lo:lo + blk, :] = su[:, s * 128:(s + 1) * 128]
            yield
        u8 = jnp.concatenate([su_scr[s, pl.ds(base + j, R, stride=S5_SUB), :]
                              for j in range(S5_SUB) for s in range(n_slab)], axis=1)
        u8b = u8.astype(BF16)
        y_loc, inc = [], []
        n_col = 2 * n_state // n_sub
        for t in range(n_sub):
            y_loc.append(jnp.dot(u8b[:, 0:(t + 1) * W], krev_ref[(S5_SUB - 1 - t) * W:S5_SUB * W, :],
                                 preferred_element_type=F32)
                         + d8_ref[:, t * W:(t + 1) * W] * u8[:, t * W:(t + 1) * W])
            inc.append(jnp.dot(u8b, bbig_ref[:, t * n_col:(t + 1) * n_col], preferred_element_type=F32))
            yield
        inc = jnp.concatenate(inc, axis=1)
        xr, xi = inc[:, :n_state], inc[:, n_state:]
        assert carry["parts_done"] == part
        cr, ci = carry["r"], carry["i"]
        ar0, ai0 = pwr_ref[0:1, :], pwi_ref[0:1, :]
        xr = xr + jnp.where(first, ar0 * cr - ai0 * ci, 0.0)
        xi = xi + jnp.where(first, ar0 * ci + ai0 * cr, 0.0)
        shift, lvl = 1, 0
        while shift < R:
            ar, ai = pwr_ref[lvl:lvl + 1, :], pwi_ref[lvl:lvl + 1, :]
            sr, si = shifted(xr, shift), shifted(xi, shift)
            xr, xi = xr + ar * sr - ai * si, xi + ar * si + ai * sr
            shift, lvl = shift * 2, lvl + 1
            yield
        pr = jnp.where(first, cr, pltpu.roll(xr, 1, 0))
        pi = jnp.where(first, ci, pltpu.roll(xi, 1, 0))
        carry.update(r=xr[R - 1:R, :], i=xi[R - 1:R, :], parts_done=part + 1)
        xprev = jnp.concatenate([pr, pi], axis=1).astype(BF16)
        yield
        for j in range(S5_SUB):
            yj = y_loc[j] + jnp.dot(xprev, cbig_ref[:, j * W:(j + 1) * W], preferred_element_type=F32)
            for s in range(n_slab):
                y_scr[s, pl.ds(base + j, R, stride=S5_SUB), :] = yj[:, s * 128:(s + 1) * 128]
            yield
        y_ref[base:base + tok, :] = jnp.concatenate([y_scr[s, base:base + tok, :] for s in range(n_slab)],
                                                    axis=1)

    gens = [part_stages(part) for part in range(S5_PARTS)]
    for lead in range(S5_PARTS):
        for _ in range(n_sub):
            for gen in gens[:lead + 1]:
                next(gen, None)
    live = list(gens)
    while live:
        for gen in list(live):
            if next(gen, "done") == "done":
                live.remove(gen)
    cr_ref[...] = carry["r"]
    ci_ref[...] = carry["i"]


def _s5_prepare(lam_re, lam_im, log_dt, b_re, b_im, c_re, c_im, d_skip):
    G, P, H, M = S5_GROUPS, S5_STATE, S5_CH, S5_SUB
    depth = lam_re.shape[0]
    lr = jnp.minimum(lam_re.astype(F32), -1e-4)
    li = lam_im.astype(F32)
    dt = jnp.exp(log_dt.astype(F32))[:, :, None]

    def apow(t):
        tt = jnp.asarray(t, F32)[None, :, None, None]
        mag = jnp.exp((lr * dt)[:, None] * tt)
        ang = (li * dt)[:, None] * tt
        return mag * jnp.cos(ang), mag * jnp.sin(ang)

    n_lvl = int(math.log2(S5_ROWS))
    exps = list(range(M + 1)) + list(range(M - 1, -1, -1)) + [M * 2 ** s for s in range(n_lvl)]
    pw_r, pw_i = apow(np.asarray(exps, np.float32))
    rev_r, rev_i = pw_r[:, M + 1:2 * M + 1], pw_i[:, M + 1:2 * M + 1]
    ar, ai = pw_r[:, 1], pw_i[:, 1]
    nr, ni = ar - 1.0, ai
    den = lr * lr + li * li
    fr = ((nr * lr + ni * li) / den)[..., None]
    fi = ((ni * lr - nr * li) / den)[..., None]
    bbr = fr * b_re - fi * b_im
    bbi = fr * b_im + fi * b_re

    def embed(narrow, tile, row_div, col_div):
        wide = jnp.einsum('dnk,km->dnm', narrow.astype(BF16), jnp.asarray(tile, BF16),
                          preferred_element_type=F32)
        rg = (lax.broadcasted_iota(jnp.int32, wide.shape, 1) // row_div) % G
        cg = (lax.broadcasted_iota(jnp.int32, wide.shape, 2) // col_div) % G
        return jnp.where(rg == cg, wide, 0.0).astype(BF16)

    pr, pi = rev_r[:, :, :, None, :], rev_i[:, :, :, None, :]
    car = c_re[:, None] * pr - c_im[:, None] * pi
    cai = c_re[:, None] * pi + c_im[:, None] * pr
    bbr_t = bbr.transpose(0, 1, 3, 2)[:, None, :, :, None, :]
    bbi_t = bbi.transpose(0, 1, 3, 2)[:, None, :, :, None, :]
    kt = jnp.sum(car[:, :, :, None] * bbr_t - cai[:, :, :, None] * bbi_t, axis=-1)
    k_narrow = kt.reshape(depth, M * G * H, H)
    k_tile = np.tile(np.eye(H, dtype=np.float32), (1, G))
    krev = embed(k_narrow, k_tile, H, H)

    pr, pi = rev_r[..., None], rev_i[..., None]
    b_narrow = jnp.stack([pr * bbr[:, None] - pi * bbi[:, None], pr * bbi[:, None] + pi * bbr[:, None]],
                         axis=1)
    b_narrow = b_narrow.transpose(0, 2, 3, 5, 1, 4).reshape(depth, M * G * H, 2 * P)
    cp = np.arange(2 * P)
    col = np.arange(2 * G * P)
    b_tile = ((cp[:, None] // P == (col // (G * P))[None, :]) & (cp[:, None] % P == (col % P)[None, :]))
    bbig = embed(b_narrow, b_tile.astype(np.float32), H, P)

    pr, pi = pw_r[:, 1:M + 1, :, None, :], pw_i[:, 1:M + 1, :, None, :]
    qr = c_re[:, None] * pr - c_im[:, None] * pi
    qi = -(c_re[:, None] * pi + c_im[:, None] * pr)
    c_narrow = jnp.stack([qr, qi], axis=1).transpose(0, 1, 3, 5, 2, 4).reshape(depth, 2 * G * P, M * H)
    th = np.arange(M * H)
    col = np.arange(M * G * H)
    c_tile = ((th[:, None] // H == (col // (G * H))[None, :]) & (th[:, None] % H == (col % H)[None, :]))
    cbig = embed(c_narrow, c_tile.astype(np.float32), P, H)

    pad = (-n_lvl) % 8
    pwr = jnp.pad(pw_r[:, 2 * M + 1:].reshape(depth, n_lvl, G * P), ((0, 0), (0, pad), (0, 0)))
    pwi = jnp.pad(pw_i[:, 2 * M + 1:].reshape(depth, n_lvl, G * P), ((0, 0), (0, pad), (0, 0)))
    d8 = jnp.tile(d_skip.astype(F32), (1, M))[:, None, :]
    return krev, bbig, cbig, pwr, pwi, d8


def _layer_spec(arr, layer, n_grid):
    nd = arr.ndim - 1
    zeros = (0,) * nd
    if n_grid == 1:
        return pl.BlockSpec((None,) + arr.shape[1:], lambda i: (layer,) + zeros, pipeline_mode=pl.Buffered(1))
    return pl.BlockSpec((None,) + arr.shape[1:], lambda b, i: (layer,) + zeros, pipeline_mode=pl.Buffered(1))


def _s5_call(h, layer, norm_g, w_su, prep):
    B, L, D = h.shape
    krev, bbig, cbig, pwr, pwi, d8 = prep
    tokens = S5_SUB * S5_ROWS
    n_state = S5_GROUPS * S5_STATE
    n_slab = GROUP_WIDTH // 128
    params = [norm_g, w_su, krev, bbig, cbig, pwr, pwi, d8]

    return pl.pallas_call(
        _s5_kernel,
        grid=(B, L // tokens),
        in_specs=[pl.BlockSpec((None, tokens, D), lambda b, i: (b, i, 0))]
        + [_layer_spec(a, layer, 2) for a in params],
        out_specs=pl.BlockSpec((None, tokens, GROUP_WIDTH), lambda b, i: (b, i, 0)),
        out_shape=jax.ShapeDtypeStruct((B, L, GROUP_WIDTH), F32),
        scratch_shapes=[pltpu.VMEM((1, n_state), F32), pltpu.VMEM((1, n_state), F32),
                        pltpu.VMEM((n_slab, tokens, 128), F32), pltpu.VMEM((n_slab, tokens, 128), F32)],
        compiler_params=pltpu.CompilerParams(dimension_semantics=("arbitrary", "arbitrary"),
                                             vmem_limit_bytes=VMEM_LIMIT),
        name="s5_mixer",
    )(h, *params)


def _mixer_consts():
    TL, C, H = MIX_TL, CHUNK, NUM_HEADS
    t = np.arange(TL)
    ltri = ((t[:, None] // C == t[None, :] // C) & (t[None, :] <= t[:, None])).astype(np.float32)
    lane256 = np.arange(256)
    lane128 = np.arange(128)
    hm128 = (lane128[None, :] // GLA_DK == np.arange(H)[:, None]).astype(np.float32)
    hm256 = (lane256[None, :] // HEAD_DIM == np.arange(H)[:, None]).astype(np.float32)
    bm = (lane256[:, None] // HEAD_DIM == lane256[None, :] // HEAD_DIM).astype(np.float32)
    bmt = (lane256[:, None] // HEAD_DIM == lane128[None, :] // GLA_DK).astype(np.float32)
    c = np.arange(C)
    caus4 = np.tile((c[:, None] >= c[None, :]).astype(np.float32), (H, 1))
    s_lane = lane128 % HEAD_DIM
    bm2 = bm[0:128, 0:128]
    incl = (c[:, None] >= s_lane[None, :]).astype(np.float32)
    strict = (c[:, None] > s_lane[None, :]).astype(np.float32)
    idiag = (c[:, None] == s_lane[None, :]).astype(np.float32)
    log_gamma = np.log1p(-(2.0 ** (-5.0 - np.arange(H, dtype=np.float32)))).astype(np.float32)
    idx = np.arange(TL, dtype=np.float32)
    rel = idx[:, None] - idx[None, :]
    dmask = np.where(rel >= 0, np.exp(np.maximum(rel, 0.0)[None] * log_gamma[:, None, None]), 0.0)
    dmask = dmask.reshape(H * TL, TL).astype(np.float32)
    lg_lane = log_gamma[lane256 // HEAD_DIM]
    xi = np.exp((idx[:, None] + 1.0) * lg_lane[None, :]).astype(np.float32)
    zeta = np.exp((TL - 1.0 - idx[:, None]) * lg_lane[None, :]).astype(np.float32)
    gchunk = np.exp(TL * lg_lane)[None, :].astype(np.float32)
    e_beta = np.zeros((128, 256), np.float32)
    e_a = np.zeros((128, 256), np.float32)
    for h in range(H):
        e_beta[SM_BETA + h, h * HEAD_DIM:(h + 1) * HEAD_DIM] = 1.0
        e_a[SM_A + h, h * HEAD_DIM:(h + 1) * HEAD_DIM] = 1.0
    return dict(ltri=ltri, hm128=hm128, hm256=hm256, bm=bm, bmt=bmt, caus4=caus4, incl=incl, strict=strict,
                idiag=idiag, bm2=bm2, dmask=dmask, xi=xi, zeta=zeta, gchunk=gchunk, e_beta=e_beta, e_a=e_a,
                m64=bm / HEAD_DIM)


_CONST_ORDER = ("ltri", "hm128", "hm256", "bm", "bmt", "caus4", "incl", "strict", "idiag", "bm2", "dmask", "xi",
                "zeta", "gchunk", "e_beta", "e_a", "m64")
_PARAM_ORDER = ("norm_mix", "w_z", "w_out", "w_a2", "b_a", "gla_g", "ret_g", "gdn_g", "conv_w", "alog_v",
                "dtb_v", "w_glu", "b_glu")


def _mixer_kernel(*refs):
    n_c, n_p = len(_CONST_ORDER), len(_PARAM_ORDER)
    h_ref, ys5_ref, cos_ref, sin_ref = refs[:4]
    cst = dict(zip(_CONST_ORDER, refs[4:4 + n_c]))
    prm = dict(zip(_PARAM_ORDER, refs[4 + n_c:4 + n_c + n_p]))
    o_ref = refs[4 + n_c + n_p]
    st_gla, s_ret, s_gdn, xpad = refs[4 + n_c + n_p + 1:]
    TL, C, H = MIX_TL, CHUNK, NUM_HEADS

    @pl.when(pl.program_id(1) == 0)
    def _():
        st_gla[...] = jnp.zeros_like(st_gla)
        s_ret[...] = jnp.zeros_like(s_ret)
        s_gdn[...] = jnp.zeros_like(s_gdn)
        xpad[0:8, :] = jnp.zeros((8, 3 * GROUP_WIDTH), F32)

    h = h_ref[...]
    hn = _rms(h, prm["norm_mix"][...]).astype(BF16)
    w_z = prm["w_z"]

    def proj(lo, hi):
        return jnp.dot(hn, w_z[:, lo:hi], preferred_element_type=F32)

    hm128 = cst["hm128"][...]
    hm256 = cst["hm256"][...]
    bm = cst["bm"][...]
    m64 = cst["m64"][...]
    ltri = cst["ltri"][...]
    sm = proj(Z_SMALL, Z_WIDTH)

    def take_heads(stacked, rows):
        out = stacked[0:rows] * hm256[0:1]
        for hh in range(1, H):
            out = out + stacked[hh * rows:(hh + 1) * rows] * hm256[hh:hh + 1]
        return out

    out = {}

    def gdn_stages():
        zd = proj(Z_GDN, Z_GDN + 1024)
        xpad[8:8 + TL, :] = zd[:, 0:768]
        cw = prm["conv_w"][...]
        xc = cw[GDN_CONV - 1:GDN_CONV, :] * zd[:, 0:768]
        for j in range(GDN_CONV - 1):
            xc = xc + cw[j:j + 1, :] * xpad[5 + j:5 + j + TL, :]
        xpad[0:8, :] = zd[TL - 8:TL, 0:768]
        xc = _silu(xc)
        dq, dk, dv = xc[:, 0:256], xc[:, 256:512], xc[:, 512:768]
        yield
        dq = dq * lax.rsqrt(_dot(dq * dq, bm) + EPS) * (HEAD_DIM ** -0.5)
        dk = dk * lax.rsqrt(_dot(dk * dk, bm) + EPS)
        beta = _dot_exact_rhs(_sigmoid(sm), cst["e_beta"][...])
        g_log = _dot_exact_rhs(-jnp.exp(prm["alog_v"][...]) * _softplus(sm + prm["dtb_v"][...]),
                               cst["e_a"][...])
        yield
        gcum = _dot_exact_lhs(ltri, g_log)
        yield
        incl = cst["incl"][...]
        strict = cst["strict"][...]
        idiag = cst["idiag"][...]
        bm2 = cst["bm2"][...]
        bm2b = bm2.astype(BF16)
        n_pair = NUM_HEADS // 2
        probs = [(c, p) for c in range(TL // C) for p in range(n_pair)]

        def blk(x, c, p):
            return x[c * C:(c + 1) * C, p * 128:(p + 1) * 128]

        def bd2(x):
            xb = x.astype(BF16)
            return jnp.concatenate([xb, xb], axis=0) * bm2b

        def bdot(a, b):
            return jnp.dot(a.astype(BF16), b, preferred_element_type=F32)

        gk_, gq_, gv_, gb_, gg_, p_, t_, aqk_ = {}, {}, {}, {}, {}, {}, {}, {}
        for cp in probs:
            k_c, q_c, g_c = blk(dk, *cp), blk(dq, *cp), blk(gcum, *cp)
            gk_[cp], gq_[cp], gv_[cp], gb_[cp], gg_[cp] = k_c, q_c, blk(dv, *cp), blk(beta, *cp), g_c
            g_row = jnp.sum(g_c * idiag, axis=0, keepdims=True)
            dec = jnp.where(incl > 0.0, jnp.exp(jnp.where(incl > 0.0, g_c - g_row, 0.0)), 0.0)
            kq = lax.dot_general(jnp.concatenate([k_c, q_c], axis=0).astype(BF16), bd2(k_c),
                                 (((1,), (1,)), ((), ())), preferred_element_type=F32)
            aqk_[cp] = kq[C:2 * C] * dec
            p_[cp] = -(kq[0:C] * dec * strict * gb_[cp])
            t_[cp] = idiag + p_[cp]
        yield
        for cp in probs:
            p_[cp] = bdot(p_[cp], bd2(p_[cp]))
        yield
        for _ in range(int(math.log2(C)) - 2):
            for cp in probs:
                pt = bdot(jnp.concatenate([p_[cp], t_[cp]], axis=0), bd2(p_[cp]))
                p_[cp] = pt[0:C]
                t_[cp] = t_[cp] + pt[C:2 * C]
            yield
        for cp in probs:
            t_[cp] = t_[cp] + bdot(t_[cp], bd2(p_[cp]))
        yield
        uw_, eg_ = {}, {}
        for cp in probs:
            eg_[cp] = jnp.exp(gg_[cp])
            b_c = gb_[cp]
            uw_[cp] = bdot(t_[cp], jnp.concatenate([bd2(gv_[cp] * b_c), bd2(gk_[cp] * b_c * eg_[cp])],
                                                   axis=1))
        yield
        mm_, nn_, dl_, qp_, op_ = {}, {}, {}, {}, {}
        for cp in probs:
            k_c, g_c = gk_[cp], gg_[cp]
            g_last = g_c[C - 1:C, :]
            u_c, w_c = uw_[cp][:, 0:128], uw_[cp][:, 128:256]
            k_end_t = (k_c * jnp.exp(g_last - g_c)).T
            mn = bdot(k_end_t, jnp.concatenate([w_c, u_c], axis=1).astype(BF16))
            mm_[cp] = bm2 * mn[:, 0:128]
            nn_[cp] = bm2 * mn[:, 128:256]
            dl_[cp] = jnp.exp(g_last)
            qo = bdot(aqk_[cp], jnp.concatenate([bd2(w_c), bd2(u_c)], axis=1))
            qp_[cp] = gq_[cp] * eg_[cp] - qo[:, 0:128]
            op_[cp] = qo[:, 128:256]
        yield
        o_rows = []
        sgp = [s_gdn[p] for p in range(n_pair)]
        for c in range(TL // C):
            o_pair = []
            for p in range(n_pair):
                cp = (c, p)
                s_b = sgp[p].astype(BF16)
                o_pair.append(bdot(qp_[cp], s_b) + op_[cp])
                sgp[p] = sgp[p] * dl_[cp] - bdot(mm_[cp], s_b) + nn_[cp]
            o_rows.append(jnp.concatenate(o_pair, axis=1))
            yield
        for p in range(n_pair):
            s_gdn[p] = sgp[p]
        o_d = jnp.concatenate(o_rows, axis=0)
        out["d"] = o_d * lax.rsqrt(_dot(o_d * o_d, m64) + EPS) * prm["gdn_g"][...] * _silu(zd[:, 768:1024])

    def gla_stages():
        zg = proj(Z_GLA, Z_GLA + 768)
        q, k, v, r = zg[:, 0:128], zg[:, 128:256], zg[:, 256:512], zg[:, 512:768]
        gk = _dot(sm, prm["w_a2"][...]) + prm["b_a"][...]
        gk = (jnp.minimum(gk, 0.0) - jnp.log1p(jnp.exp(-jnp.abs(gk)))) / GLA_GATE_NORM
        yield
        bcum = _dot_exact_lhs(ltri, gk)
        caus4 = cst["caus4"][...]
        bmt = cst["bmt"][...]
        st = st_gla[...]
        yield
        o_chunks = []
        for c in range(TL // C):
            sl = slice(c * C, (c + 1) * C)
            b_c = bcum[sl]
            b_last = b_c[C - 1:C, :]
            q_t = q[sl] * jnp.exp(b_c) * (GLA_DK ** -0.5)
            k_t = k[sl] * jnp.exp(-b_c)
            k_end = k[sl] * jnp.exp(b_last - b_c)
            v_c = v[sl]
            qs = jnp.concatenate([q_t * hm128[hh:hh + 1] for hh in range(H)], axis=0)
            att = _dot_nt(qs, k_t) * caus4
            cross = _dot_nt(q_t, st)
            st = st * jnp.exp(b_last) + bmt * _dot(v_c.T, k_end)
            yield
            o_chunks.append(take_heads(_dot(att, v_c), C) + cross)
            yield
        st_gla[...] = st
        o_a = jnp.concatenate(o_chunks, axis=0)
        out["a"] = o_a * lax.rsqrt(_dot(o_a * o_a, m64) + EPS) * prm["gla_g"][...] * _silu(r)

    def ret_stages():
        zr = proj(Z_RET, Z_RET + 1024)
        cos2 = jnp.concatenate([cos_ref[...], cos_ref[...]], axis=1)
        sin2 = jnp.concatenate([sin_ref[...], sin_ref[...]], axis=1)
        first_half = (lax.broadcasted_iota(jnp.int32, (TL, 256), 1) % HEAD_DIM) < HEAD_DIM // 2

        def rope(x):
            rot = jnp.where(first_half, pltpu.roll(x, 256 - HEAD_DIM // 2, 1),
                            pltpu.roll(x, HEAD_DIM // 2, 1))
            return x * cos2 + rot * sin2

        rq = rope(zr[:, 0:256])
        rk = rope(zr[:, 256:512]) * (HEAD_DIM ** -0.5)
        rv = zr[:, 512:768]
        rg = zr[:, 768:1024]
        yield
        qs = jnp.concatenate([rq * hm256[hh:hh + 1] for hh in range(H)], axis=0)
        sc = _dot_nt(qs, rk) * cst["dmask"][...]
        s_prev = s_ret[...]
        cross = _dot(rq, s_prev) * cst["xi"][...]
        s_ret[...] = s_prev * cst["gchunk"][...] + bm * _dot((rk * cst["zeta"][...]).T, rv)
        yield
        o_c = take_heads(_dot(sc, rv), TL) + cross
        yield
        o_c = o_c - _dot_exact_rhs(o_c, m64)
        yield
        out["c"] = o_c * lax.rsqrt(_dot(o_c * o_c, m64) + EPS) * prm["ret_g"][...] * _silu(rg)

    def s5_stages():
        y = ys5_ref[...]
        y = 0.5 * y * (1.0 + jnp.tanh(math.sqrt(2.0 / math.pi) * (y + 0.044715 * (y * y * y))))
        yield
        out["b"] = y * _sigmoid(_dot(y, prm["w_glu"][...]) + prm["b_glu"][...])

    running = [gdn_stages(), gla_stages(), ret_stages(), s5_stages()]
    while running:
        for gen in list(running):
            if next(gen, "done") == "done":
                running.remove(gen)

    mix = jnp.concatenate([out["a"], out["b"], out["c"], out["d"]], axis=1).astype(BF16)
    o_ref[...] = h + jnp.dot(mix, prm["w_out"][...], preferred_element_type=F32)


def _mixer_call(h, ys5, cos_t, sin_t, consts, params, layer):
    B, L, D = h.shape
    TL = MIX_TL

    def const(arr):
        nd = arr.ndim
        return pl.BlockSpec(arr.shape, lambda b, i: (0,) * nd, pipeline_mode=pl.Buffered(1))

    def tile(width):
        return pl.BlockSpec((None, TL, width), lambda b, i: (b, i, 0))

    c_list = [consts[n] for n in _CONST_ORDER]
    p_list = [params[n] for n in _PARAM_ORDER]
    return pl.pallas_call(
        _mixer_kernel,
        grid=(B, L // TL),
        in_specs=[tile(D), tile(GROUP_WIDTH), tile(128), tile(128)] + [const(a) for a in c_list]
        + [_layer_spec(a, layer, 2) for a in p_list],
        out_specs=tile(D),
        out_shape=jax.ShapeDtypeStruct((B, L, D), F32),
        scratch_shapes=[pltpu.VMEM((256, 128), F32), pltpu.VMEM((256, 256), F32), pltpu.VMEM((2, 128, 128), F32),
                        pltpu.VMEM((TL + 8, 3 * GROUP_WIDTH), F32)],
        compiler_params=pltpu.CompilerParams(dimension_semantics=("arbitrary", "arbitrary"),
                                             vmem_limit_bytes=VMEM_LIMIT),
        name="mixer",
    )(h, ys5, cos_t, sin_t, *c_list, *p_list)


def _ffn_kernel(h_ref, p_ref, gf_ref, wup_ref, wdn_ref, gp_ref, wg_ref, wp_ref, gl_ref, o_ref, act_ref, *,
                final):
    h = h_ref[...]
    hn = _rms(h, gf_ref[...]).astype(BF16)
    for s in range(FFN_HIDDEN // FFN_SLAB):
        lo = s * FFN_SLAB
        g = jnp.dot(hn, wup_ref[:, lo:lo + FFN_SLAB], preferred_element_type=F32)
        u = jnp.dot(hn, wup_ref[:, FFN_HIDDEN + lo:FFN_HIDDEN + lo + FFN_SLAB], preferred_element_type=F32)
        act_ref[:, lo:lo + FFN_SLAB] = (_silu(g) * u).astype(BF16)
    h2 = h + jnp.dot(act_ref[...], wdn_ref[...], preferred_element_type=F32)
    gate = _sigmoid(jnp.dot(_rms(h2, gp_ref[...]).astype(BF16), wg_ref[...], preferred_element_type=F32))
    h3 = h2 + jnp.dot(p_ref[...].astype(BF16), wp_ref[...], preferred_element_type=F32) * gate
    if final:
        h3 = _rms(h3, gl_ref[...])
    o_ref[...] = h3


def _ffn_call(h2d, p3d, layer, weights, final):
    T, D = h2d.shape
    TM = FFN_TM
    return pl.pallas_call(
        functools.partial(_ffn_kernel, final=final),
        grid=(T // TM,),
        in_specs=[pl.BlockSpec((TM, D), lambda i: (i, 0)),
                  pl.BlockSpec((None, TM, PLE_DIM), lambda i: (layer, i, 0))]
        + [_layer_spec(w, layer, 1) for w in weights],
        out_specs=pl.BlockSpec((TM, D), lambda i: (i, 0)),
        out_shape=jax.ShapeDtypeStruct((T, D), F32),
        scratch_shapes=[pltpu.VMEM((TM, FFN_HIDDEN), BF16)],
        compiler_params=pltpu.CompilerParams(dimension_semantics=("arbitrary",), vmem_limit_bytes=VMEM_LIMIT),
        name="ffn_ple",
    )(h2d, p3d, *weights)


def _row(v):
    return v[:, None, :]


def _cast_kernel(x_ref, o_ref):
    o_ref[...] = x_ref[...].astype(BF16)


def _to_bf16(w):
    depth, K, N = w.shape
    rows = depth * K
    blk = 512 if rows % 512 == 0 else 256
    out = pl.pallas_call(
        _cast_kernel,
        grid=(rows // blk,),
        in_specs=[pl.BlockSpec((blk, N), lambda i: (i, 0))],
        out_specs=pl.BlockSpec((blk, N), lambda i: (i, 0)),
        out_shape=jax.ShapeDtypeStruct((rows, N), BF16),
        name="weight_to_bf16",
    )(w.reshape(rows, N))
    return out.reshape(depth, K, N)


_WZ_SEGMENTS = (_AQ, _AK, _AV, _AR, _RQ, _RK, _RV, _RG, _DQ, _DK, _DV, _DG, _ALOW, _DB, _DA)


def _regroup_kernel(w_ref, wz_ref, wsu_ref):
    w = w_ref[...]
    pieces = [w[:, lo:hi] for lo, hi in _WZ_SEGMENTS]
    pieces.append(jnp.zeros((w.shape[0], 128 - GLA_RANK - 2 * NUM_HEADS), F32))
    wz_ref[...] = jnp.concatenate(pieces, axis=1).astype(BF16)
    wsu_ref[...] = w[:, _SU[0]:_SU[1]].astype(BF16)


def _regroup_w_in(w_in):
    depth, K, N = w_in.shape
    rows = depth * K
    blk = 256
    w_z, w_su = pl.pallas_call(
        _regroup_kernel,
        grid=(rows // blk,),
        in_specs=[pl.BlockSpec((blk, N), lambda i: (i, 0))],
        out_specs=[pl.BlockSpec((blk, Z_WIDTH), lambda i: (i, 0)),
                   pl.BlockSpec((blk, GROUP_WIDTH), lambda i: (i, 0))],
        out_shape=[jax.ShapeDtypeStruct((rows, Z_WIDTH), BF16), jax.ShapeDtypeStruct((rows, GROUP_WIDTH), BF16)],
        name="regroup_w_in",
    )(w_in.reshape(rows, N))
    return w_z.reshape(depth, K, Z_WIDTH), w_su.reshape(depth, K, GROUP_WIDTH)


def _mixer_params(norm_mix, w_z, w_out, gla_w_a2, gla_b_a, gla_norm, ret_norm, gdn_conv, gdn_a_log,
                  gdn_dt_bias, gdn_norm, s5_w_glu, s5_b_glu):
    w_a2 = jnp.pad(gla_w_a2, ((0, 0), (0, 128 - GLA_RANK), (0, 0)))
    lane_pad = ((0, 0), (SM_A, 128 - SM_A - NUM_HEADS))
    return dict(norm_mix=_row(norm_mix), w_z=w_z, w_out=_to_bf16(w_out), w_a2=w_a2, b_a=_row(gla_b_a),
                gla_g=_row(jnp.tile(gla_norm, (1, NUM_HEADS))), ret_g=_row(ret_norm),
                gdn_g=_row(jnp.tile(gdn_norm, (1, NUM_HEADS))), conv_w=gdn_conv,
                alog_v=_row(jnp.pad(gdn_a_log, lane_pad)), dtb_v=_row(jnp.pad(gdn_dt_bias, lane_pad)),
                w_glu=_to_bf16(s5_w_glu), b_glu=_row(s5_b_glu))


def kernel(x, p, positions, norm_mix, w_in, w_out, gla_w_a2, gla_b_a, gla_norm, s5_lam_re, s5_lam_im, s5_log_dt,
           s5_b_re, s5_b_im, s5_c_re, s5_c_im, s5_d, s5_w_glu, s5_b_glu, ret_norm, gdn_conv, gdn_a_log,
           gdn_dt_bias, gdn_norm, norm_ffn, w_ffn_up, w_ffn_down, norm_ple, w_ple_gate, w_ple_proj, norm_final):
    B, L, D = x.shape
    depth = w_in.shape[0]
    assert D == D_MODEL and L % max(MIX_TL, S5_SUB * S5_ROWS) == 0 and (B * L) % FFN_TM == 0
    consts = {k: jnp.asarray(v) for k, v in _mixer_consts().items()}
    cos_t, sin_t = _rope_tables(positions)
    prep = _s5_prepare(s5_lam_re, s5_lam_im, s5_log_dt, s5_b_re, s5_b_im, s5_c_re, s5_c_im, s5_d)
    w_z, w_su = _regroup_w_in(w_in)
    params = _mixer_params(norm_mix, w_z, w_out, gla_w_a2, gla_b_a, gla_norm, ret_norm, gdn_conv,
                           gdn_a_log, gdn_dt_bias, gdn_norm, s5_w_glu, s5_b_glu)
    ffn_w = [_row(norm_ffn), _to_bf16(w_ffn_up), _to_bf16(w_ffn_down), _row(norm_ple),
             _to_bf16(w_ple_gate), _to_bf16(w_ple_proj),
             jnp.broadcast_to(norm_final[None, None, :], (depth, 1, D))]
    p3d = p.reshape(depth, B * L, PLE_DIM)
    h = x
    for i in range(depth):
        ys5 = _s5_call(h, i, _row(norm_mix), w_su, prep)
        h = _mixer_call(h, ys5, cos_t, sin_t, consts, params, i)
        h = _ffn_call(h.reshape(B * L, D), p3d, i, ffn_w, final=(i == depth - 1)).reshape(B, L, D)
    return h
```

```python
import functools
import math

import numpy as np
import jax
import jax.numpy as jnp
from jax import lax
from jax.experimental import pallas as pl
from jax.experimental.pallas import tpu as pltpu

F32 = jnp.float32
BF16 = jnp.bfloat16
HI = lax.Precision.HIGHEST

D_MODEL = 1024
PLE_DIM = 256
GROUP_WIDTH = 256
NUM_HEADS = 4
CHUNK = 64
EPS = 1e-6

GLA_DK = 32
GLA_RANK = 16
GLA_GATE_NORM = 16.0
S5_CH = 16
S5_GROUPS = 16
S5_STATE = 64
S5_SUB = 8
HEAD_DIM = 64
ROPE_BASE = 10000.0
GDN_CONV = 4
FFN_HIDDEN = 2816

MIX_TL = 256
S5_ROWS = 256
S5_PARTS = 2
FFN_TM = 1024
FFN_SLAB = 256

VMEM_LIMIT = 56 * 1024 * 1024

_IN_OFFS = np.cumsum([0, 128, 128, 256, 16, 256, 256, 256, 256, 256, 256, 256, 256, 256, 4, 4, 256])
(_AQ, _AK, _AV, _ALOW, _AR, _SU, _RQ, _RK, _RV, _RG, _DQ, _DK, _DV, _DB, _DA, _DG) = [
    (int(_IN_OFFS[i]), int(_IN_OFFS[i + 1])) for i in range(16)]
Z_GLA = 0
Z_RET = 768
Z_GDN = 1792
Z_SMALL = 2816
Z_WIDTH = 2944
SM_BETA = 16
SM_A = 20


def _dot(a, b):
    return jnp.dot(a.astype(BF16), b.astype(BF16), preferred_element_type=F32)


def _dot_nt(a, b):
    return lax.dot_general(a.astype(BF16), b.astype(BF16), (((1,), (1,)), ((), ())),
                           preferred_element_type=F32)


def _split(x):
    hi = x.astype(BF16)
    return hi, (x - hi.astype(F32)).astype(BF16)


def _dot_exact_lhs(a, b):
    hi, lo = _split(b)
    n = b.shape[1]
    if n <= 128:
        r = jnp.dot(a.astype(BF16), jnp.concatenate([hi, lo], axis=1), preferred_element_type=F32)
        return r[:, :n] + r[:, n:]
    return (jnp.dot(a.astype(BF16), hi, preferred_element_type=F32)
            + jnp.dot(a.astype(BF16), lo, preferred_element_type=F32))


def _dot_exact_rhs(a, b):
    hi, lo = _split(a)
    m = a.shape[0]
    r = jnp.dot(jnp.concatenate([hi, lo], axis=0), b.astype(BF16), preferred_element_type=F32)
    return r[:m] + r[m:]


def _sigmoid(x):
    return 0.5 * jnp.tanh(0.5 * x) + 0.5


def _silu(x):
    return x * _sigmoid(x)


def _softplus(x):
    return jnp.maximum(x, 0.0) + jnp.log1p(jnp.exp(-jnp.abs(x)))


def _rms(x, g):
    return x * lax.rsqrt(jnp.mean(x * x, axis=-1, keepdims=True) + EPS) * g


def _stack4(x):
    return jnp.concatenate([x, x, x, x], axis=0)


def _rope_kernel(pos_ref, invf_ref, cos_ref, sin_ref):
    ang = invf_ref[...] * pos_ref[...].astype(F32)
    c = jnp.cos(ang)
    s = jnp.sin(ang)
    cos_ref[...] = jnp.concatenate([c, c, c, c], axis=0).T
    sin_ref[...] = jnp.concatenate([-s, s, -s, s], axis=0).T


def _rope_tables(positions):
    B, L = positions.shape
    inv_freq = ROPE_BASE ** (-jnp.linspace(0.0, 1.0, HEAD_DIM // 2, dtype=F32))
    spec = pl.BlockSpec((None, L, 128), lambda b: (b, 0, 0))
    return pl.pallas_call(
        _rope_kernel,
        grid=(B,),
        in_specs=[pl.BlockSpec((None, 1, L), lambda b: (b, 0, 0)),
                  pl.BlockSpec((HEAD_DIM // 2, 1), lambda b: (0, 0))],
        out_specs=[spec, spec],
        out_shape=[jax.ShapeDtypeStruct((B, L, 128), F32)] * 2,
        name="rope_tables",
    )(positions[:, None, :], inv_freq[:, None])


def _s5_kernel(h_ref, g_ref, wsu_ref, krev_ref, bbig_ref, cbig_ref, pwr_ref, pwi_ref, d8_ref,
               y_ref, cr_ref, ci_ref, su_scr, y_scr):
    n_state = S5_GROUPS * S5_STATE
    W = GROUP_WIDTH
    n_slab = W // 128

    @pl.when(pl.program_id(1) == 0)
    def _():
        cr_ref[...] = jnp.zeros_like(cr_ref)
        ci_ref[...] = jnp.zeros_like(ci_ref)

    R = S5_ROWS // S5_PARTS
    tok = R * S5_SUB
    n_sub = S5_SUB
    carry = {"r": cr_ref[...], "i": ci_ref[...], "parts_done": 0}
    rows = lax.broadcasted_iota(jnp.int32, (R, n_state), 0)
    first = rows == 0

    def shifted(x, shift):
        if shift % 8 == 0:
            return jnp.concatenate([jnp.zeros((shift, x.shape[1]), F32), x[:R - shift]], axis=0)
        return jnp.where(rows >= shift, pltpu.roll(x, shift, 0), 0.0)

    def part_stages(part):
        base = part * tok
        blk = tok // n_sub
        for q in range(n_sub):
            lo = base + q * blk
            su = _dot(_rms(h_ref[lo:lo + blk, :], g_ref[...]), wsu_ref[...])
            for s in range(n_slab):
                su_scr[s, lo:lo + blk, :] = su[:, s * 128:(s + 1) * 128]
            yield
        u8 = jnp.concatenate([su_scr[s, pl.ds(base + j, R, stride=S5_SUB), :]
                              for j in range(S5_SUB) for s in range(n_slab)], axis=1)
        u8b = u8.astype(BF16)
        y_loc, inc = [], []
        n_col = 2 * n_state // n_sub
        for t in range(n_sub):
            y_loc.append(jnp.dot(u8b[:, 0:(t + 1) * W], krev_ref[(S5_SUB - 1 - t) * W:S5_SUB * W, :],
                                 preferred_element_type=F32)
                         + d8_ref[:, t * W:(t + 1) * W] * u8[:, t * W:(t + 1) * W])
            inc.append(jnp.dot(u8b, bbig_ref[:, t * n_col:(t + 1) * n_col], preferred_element_type=F32))
            yield
        inc = jnp.concatenate(inc, axis=1)
        xr, xi = inc[:, :n_state], inc[:, n_state:]
        assert carry["parts_done"] == part
        cr, ci = carry["r"], carry["i"]
        ar0, ai0 = pwr_ref[0:1, :], pwi_ref[0:1, :]
        xr = xr + jnp.where(first, ar0 * cr - ai0 * ci, 0.0)
        xi = xi + jnp.where(first, ar0 * ci + ai0 * cr, 0.0)
        shift, lvl = 1, 0
        while shift < R:
            ar, ai = pwr_ref[lvl:lvl + 1, :], pwi_ref[lvl:lvl + 1, :]
            sr, si = shifted(xr, shift), shifted(xi, shift)
            xr, xi = xr + ar * sr - ai * si, xi + ar * si + ai * sr
            shift, lvl = shift * 2, lvl + 1
            yield
        pr = jnp.where(first, cr, pltpu.roll(xr, 1, 0))
        pi = jnp.where(first, ci, pltpu.roll(xi, 1, 0))
        carry.update(r=xr[R - 1:R, :], i=xi[R - 1:R, :], parts_done=part + 1)
        xprev = jnp.concatenate([pr, pi], axis=1).astype(BF16)
        yield
        for j in range(S5_SUB):
            yj = y_loc[j] + jnp.dot(xprev, cbig_ref[:, j * W:(j + 1) * W], preferred_element_type=F32)
            for s in range(n_slab):
                y_scr[s, pl.ds(base + j, R, stride=S5_SUB), :] = yj[:, s * 128:(s + 1) * 128]
            yield
        y_ref[base:base + tok, :] = jnp.concatenate([y_scr[s, base:base + tok, :] for s in range(n_slab)],
                                                    axis=1)

    gens = [part_stages(part) for part in range(S5_PARTS)]
    for lead in range(S5_PARTS):
        for _ in range(n_sub):
            for gen in gens[:lead + 1]:
                next(gen, None)
    live = list(gens)
    while live:
        for gen in list(live):
            if next(gen, "done") == "done":
                live.remove(gen)
    cr_ref[...] = carry["r"]
    ci_ref[...] = carry["i"]


def _s5_prepare(lam_re, lam_im, log_dt, b_re, b_im, c_re, c_im, d_skip):
    G, P, H, M = S5_GROUPS, S5_STATE, S5_CH, S5_SUB
    depth = lam_re.shape[0]
    lr = jnp.minimum(lam_re.astype(F32), -1e-4)
    li = lam_im.astype(F32)
    dt = jnp.exp(log_dt.astype(F32))[:, :, None]

    def apow(t):
        tt = jnp.asarray(t, F32)[None, :, None, None]
        mag = jnp.exp((lr * dt)[:, None] * tt)
        ang = (li * dt)[:, None] * tt
        return mag * jnp.cos(ang), mag * jnp.sin(ang)

    n_lvl = int(math.log2(S5_ROWS))
    exps = list(range(M + 1)) + list(range(M - 1, -1, -1)) + [M * 2 ** s for s in range(n_lvl)]
    pw_r, pw_i = apow(np.asarray(exps, np.float32))
    rev_r, rev_i = pw_r[:, M + 1:2 * M + 1], pw_i[:, M + 1:2 * M + 1]
    ar, ai = pw_r[:, 1], pw_i[:, 1]
    nr, ni = ar - 1.0, ai
    den = lr * lr + li * li
    fr = ((nr * lr + ni * li) / den)[..., None]
    fi = ((ni * lr - nr * li) / den)[..., None]
    bbr = fr * b_re - fi * b_im
    bbi = fr * b_im + fi * b_re

    def embed(narrow, tile, row_div, col_div):
        wide = jnp.einsum('dnk,km->dnm', narrow.astype(BF16), jnp.asarray(tile, BF16),
                          preferred_element_type=F32)
        rg = (lax.broadcasted_iota(jnp.int32, wide.shape, 1) // row_div) % G
        cg = (lax.broadcasted_iota(jnp.int32, wide.shape, 2) // col_div) % G
        return jnp.where(rg == cg, wide, 0.0).astype(BF16)

    pr, pi = rev_r[:, :, :, None, :], rev_i[:, :, :, None, :]
    car = c_re[:, None] * pr - c_im[:, None] * pi
    cai = c_re[:, None] * pi + c_im[:, None] * pr
    bbr_t = bbr.transpose(0, 1, 3, 2)[:, None, :, :, None, :]
    bbi_t = bbi.transpose(0, 1, 3, 2)[:, None, :, :, None, :]
    kt = jnp.sum(car[:, :, :, None] * bbr_t - cai[:, :, :, None] * bbi_t, axis=-1)
    k_narrow = kt.reshape(depth, M * G * H, H)
    k_tile = np.tile(np.eye(H, dtype=np.float32), (1, G))
    krev = embed(k_narrow, k_tile, H, H)

    pr, pi = rev_r[..., None], rev_i[..., None]
    b_narrow = jnp.stack([pr * bbr[:, None] - pi * bbi[:, None], pr * bbi[:, None] + pi * bbr[:, None]],
                         axis=1)
    b_narrow = b_narrow.transpose(0, 2, 3, 5, 1, 4).reshape(depth, M * G * H, 2 * P)
    cp = np.arange(2 * P)
    col = np.arange(2 * G * P)
    b_tile = ((cp[:, None] // P == (col // (G * P))[None, :]) & (cp[:, None] % P == (col % P)[None, :]))
    bbig = embed(b_narrow, b_tile.astype(np.float32), H, P)

    pr, pi = pw_r[:, 1:M + 1, :, None, :], pw_i[:, 1:M + 1, :, None, :]
    qr = c_re[:, None] * pr - c_im[:, None] * pi
    qi = -(c_re[:, None] * pi + c_im[:, None] * pr)
    c_narrow = jnp.stack([qr, qi], axis=1).transpose(0, 1, 3, 5, 2, 4).reshape(depth, 2 * G * P, M * H)
    th = np.arange(M * H)
    col = np.arange(M * G * H)
    c_tile = ((th[:, None] // H == (col // (G * H))[None, :]) & (th[:, None] % H == (col % H)[None, :]))
    cbig = embed(c_narrow, c_tile.astype(np.float32), P, H)

    pad = (-n_lvl) % 8
    pwr = jnp.pad(pw_r[:, 2 * M + 1:].reshape(depth, n_lvl, G * P), ((0, 0), (0, pad), (0, 0)))
    pwi = jnp.pad(pw_i[:, 2 * M + 1:].reshape(depth, n_lvl, G * P), ((0, 0), (0, pad), (0, 0)))
    d8 = jnp.tile(d_skip.astype(F32), (1, M))[:, None, :]
    return krev, bbig, cbig, pwr, pwi, d8


def _layer_spec(arr, layer, n_grid):
    nd = arr.ndim - 1
    zeros = (0,) * nd
    if n_grid == 1:
        return pl.BlockSpec((None,) + arr.shape[1:], lambda i: (layer,) + zeros, pipeline_mode=pl.Buffered(1))
    return pl.BlockSpec((None,) + arr.shape[1:], lambda b, i: (layer,) + zeros, pipeline_mode=pl.Buffered(1))


def _s5_call(h, layer, norm_g, w_su, prep):
    B, L, D = h.shape
    krev, bbig, cbig, pwr, pwi, d8 = prep
    tokens = S5_SUB * S5_ROWS
    n_state = S5_GROUPS * S5_STATE
    n_slab = GROUP_WIDTH // 128
    params = [norm_g, w_su, krev, bbig, cbig, pwr, pwi, d8]

    return pl.pallas_call(
        _s5_kernel,
        grid=(B, L // tokens),
        in_specs=[pl.BlockSpec((None, tokens, D), lambda b, i: (b, i, 0))]
        + [_layer_spec(a, layer, 2) for a in params],
        out_specs=pl.BlockSpec((None, tokens, GROUP_WIDTH), lambda b, i: (b, i, 0)),
        out_shape=jax.ShapeDtypeStruct((B, L, GROUP_WIDTH), F32),
        scratch_shapes=[pltpu.VMEM((1, n_state), F32), pltpu.VMEM((1, n_state), F32),
                        pltpu.VMEM((n_slab, tokens, 128), F32), pltpu.VMEM((n_slab, tokens, 128), F32)],
        compiler_params=pltpu.CompilerParams(dimension_semantics=("arbitrary", "arbitrary"),
                                             vmem_limit_bytes=VMEM_LIMIT),
        name="s5_mixer",
    )(h, *params)


def _mixer_consts():
    TL, C, H = MIX_TL, CHUNK, NUM_HEADS
    t = np.arange(TL)
    ltri = ((t[:, None] // C == t[None, :] // C) & (t[None, :] <= t[:, None])).astype(np.float32)
    lane256 = np.arange(256)
    lane128 = np.arange(128)
    hm128 = (lane128[None, :] // GLA_DK == np.arange(H)[:, None]).astype(np.float32)
    hm256 = (lane256[None, :] // HEAD_DIM == np.arange(H)[:, None]).astype(np.float32)
    bm = (lane256[:, None] // HEAD_DIM == lane256[None, :] // HEAD_DIM).astype(np.float32)
    bmt = (lane256[:, None] // HEAD_DIM == lane128[None, :] // GLA_DK).astype(np.float32)
    c = np.arange(C)
    caus4 = (c[:, None] >= (lane256 % HEAD_DIM)[None, :]).astype(np.float32)
    bm4 = np.repeat(hm256, TL, axis=0)
    s_lane = lane128 % HEAD_DIM
    bm2 = bm[0:128, 0:128]
    incl = (c[:, None] >= s_lane[None, :]).astype(np.float32)
    strict = (c[:, None] > s_lane[None, :]).astype(np.float32)
    idiag = (c[:, None] == s_lane[None, :]).astype(np.float32)
    log_gamma = np.log1p(-(2.0 ** (-5.0 - np.arange(H, dtype=np.float32)))).astype(np.float32)
    idx = np.arange(TL, dtype=np.float32)
    rel = idx[:, None] - idx[None, :]
    dmask = np.where(rel >= 0, np.exp(np.maximum(rel, 0.0)[None] * log_gamma[:, None, None]), 0.0)
    dmask = dmask.reshape(H * TL, TL).astype(np.float32)
    lg_lane = log_gamma[lane256 // HEAD_DIM]
    xi = np.exp((idx[:, None] + 1.0) * lg_lane[None, :]).astype(np.float32)
    zeta = np.exp((TL - 1.0 - idx[:, None]) * lg_lane[None, :]).astype(np.float32)
    gchunk = np.exp(TL * lg_lane)[None, :].astype(np.float32)
    e_beta = np.zeros((128, 256), np.float32)
    e_a = np.zeros((128, 256), np.float32)
    for h in range(H):
        e_beta[SM_BETA + h, h * HEAD_DIM:(h + 1) * HEAD_DIM] = 1.0
        e_a[SM_A + h, h * HEAD_DIM:(h + 1) * HEAD_DIM] = 1.0
    return dict(ltri=ltri, hm256=hm256, bm=bm, bmt=bmt, caus4=caus4, bm4=bm4, incl=incl, strict=strict,
                idiag=idiag, bm2=bm2, dmask=dmask, xi=xi, zeta=zeta, gchunk=gchunk, e_beta=e_beta, e_a=e_a,
                m64=bm / HEAD_DIM)


_CONST_ORDER = ("ltri", "hm256", "bm", "bmt", "caus4", "bm4", "incl", "strict", "idiag", "bm2", "dmask", "xi",
                "zeta", "gchunk", "e_beta", "e_a", "m64")
_PARAM_ORDER = ("norm_mix", "w_z", "w_out", "w_a2", "b_a", "gla_g", "ret_g", "gdn_g", "conv_w", "alog_v",
                "dtb_v", "w_glu", "b_glu")


def _mixer_kernel(*refs):
    n_c, n_p = len(_CONST_ORDER), len(_PARAM_ORDER)
    h_ref, ys5_ref, cos_ref, sin_ref = refs[:4]
    cst = dict(zip(_CONST_ORDER, refs[4:4 + n_c]))
    prm = dict(zip(_PARAM_ORDER, refs[4 + n_c:4 + n_c + n_p]))
    o_ref = refs[4 + n_c + n_p]
    st_gla, s_ret, s_gdn, xpad = refs[4 + n_c + n_p + 1:]
    TL, C, H = MIX_TL, CHUNK, NUM_HEADS

    @pl.when(pl.program_id(1) == 0)
    def _():
        st_gla[...] = jnp.zeros_like(st_gla)
        s_ret[...] = jnp.zeros_like(s_ret)
        s_gdn[...] = jnp.zeros_like(s_gdn)
        xpad[0:8, :] = jnp.zeros((8, 3 * GROUP_WIDTH), F32)

    h = h_ref[...]
    hn = _rms(h, prm["norm_mix"][...]).astype(BF16)
    w_z = prm["w_z"]

    def proj(lo, hi):
        return jnp.dot(hn, w_z[:, lo:hi], preferred_element_type=F32)

    hm256b = cst["hm256"][...].astype(BF16)
    bmb = cst["bm"][...].astype(BF16)
    bm = cst["bm"][...]
    m64 = cst["m64"][...]
    ltri = cst["ltri"][...]
    sm = proj(Z_SMALL, Z_WIDTH)

    def stack4b(x):
        xb = x.astype(BF16)
        return jnp.concatenate([xb, xb, xb, xb], axis=0)

    out = {}

    def gdn_stages():
        zd = proj(Z_GDN, Z_GDN + 1024)
        xpad[8:8 + TL, :] = zd[:, 0:768]
        cw = prm["conv_w"][...]
        xc = cw[GDN_CONV - 1:GDN_CONV, :] * zd[:, 0:768]
        for j in range(GDN_CONV - 1):
            xc = xc + cw[j:j + 1, :] * xpad[5 + j:5 + j + TL, :]
        xpad[0:8, :] = zd[TL - 8:TL, 0:768]
        xc = _silu(xc)
        dq, dk, dv = xc[:, 0:256], xc[:, 256:512], xc[:, 512:768]
        yield
        dq = dq * lax.rsqrt(_dot(dq * dq, bm) + EPS) * (HEAD_DIM ** -0.5)
        dk = dk * lax.rsqrt(_dot(dk * dk, bm) + EPS)
        beta = _dot_exact_rhs(_sigmoid(sm), cst["e_beta"][...])
        g_log = _dot_exact_rhs(-jnp.exp(prm["alog_v"][...]) * _softplus(sm + prm["dtb_v"][...]),
                               cst["e_a"][...])
        yield
        gcum = _dot_exact_lhs(ltri, g_log)
        yield
        incl = cst["incl"][...]
        strict = cst["strict"][...]
        idiag = cst["idiag"][...]
        bm2 = cst["bm2"][...]
        bm2b = bm2.astype(BF16)
        n_pair = NUM_HEADS // 2
        probs = [(c, p) for c in range(TL // C) for p in range(n_pair)]

        def blk(x, c, p):
            return x[c * C:(c + 1) * C, p * 128:(p + 1) * 128]

        def bd2(x):
            xb = x.astype(BF16)
            return jnp.concatenate([xb, xb], axis=0) * bm2b

        def bdot(a, b):
            return jnp.dot(a.astype(BF16), b, preferred_element_type=F32)

        gk_, gq_, gv_, gb_, gg_, p_, t_, aqk_ = {}, {}, {}, {}, {}, {}, {}, {}
        for cp in probs:
            k_c, q_c, g_c = blk(dk, *cp), blk(dq, *cp), blk(gcum, *cp)
            gk_[cp], gq_[cp], gv_[cp], gb_[cp], gg_[cp] = k_c, q_c, blk(dv, *cp), blk(beta, *cp), g_c
            g_row = jnp.sum(g_c * idiag, axis=0, keepdims=True)
            dec = jnp.where(incl > 0.0, jnp.exp(jnp.where(incl > 0.0, g_c - g_row, 0.0)), 0.0)
            kq = lax.dot_general(jnp.concatenate([k_c, q_c], axis=0).astype(BF16), bd2(k_c),
                                 (((1,), (1,)), ((), ())), preferred_element_type=F32)
            aqk_[cp] = kq[C:2 * C] * dec
            p_[cp] = -(kq[0:C] * dec * strict * gb_[cp])
            t_[cp] = idiag + p_[cp]
        yield
        for cp in probs:
            p_[cp] = bdot(p_[cp], bd2(p_[cp]))
        yield
        for _ in range(int(math.log2(C)) - 2):
            for cp in probs:
                pt = bdot(jnp.concatenate([p_[cp], t_[cp]], axis=0), bd2(p_[cp]))
                p_[cp] = pt[0:C]
                t_[cp] = t_[cp] + pt[C:2 * C]
            yield
        for cp in probs:
            t_[cp] = t_[cp] + bdot(t_[cp], bd2(p_[cp]))
        yield
        uw_, eg_ = {}, {}
        for cp in probs:
            eg_[cp] = jnp.exp(gg_[cp])
            b_c = gb_[cp]
            uw_[cp] = bdot(t_[cp], jnp.concatenate([bd2(gv_[cp] * b_c), bd2(gk_[cp] * b_c * eg_[cp])],
                                                   axis=1))
        yield
        mm_, nn_, dl_, qp_, op_ = {}, {}, {}, {}, {}
        for cp in probs:
            k_c, g_c = gk_[cp], gg_[cp]
            g_last = g_c[C - 1:C, :]
            u_c, w_c = uw_[cp][:, 0:128], uw_[cp][:, 128:256]
            k_end_t = (k_c * jnp.exp(g_last - g_c)).T
            mn = bdot(k_end_t, jnp.concatenate([w_c, u_c], axis=1).astype(BF16))
            mm_[cp] = bm2 * mn[:, 0:128]
            nn_[cp] = bm2 * mn[:, 128:256]
            dl_[cp] = jnp.exp(g_last)
            qo = bdot(aqk_[cp], jnp.concatenate([bd2(w_c), bd2(u_c)], axis=1))
            qp_[cp] = gq_[cp] * eg_[cp] - qo[:, 0:128]
            op_[cp] = qo[:, 128:256]
        yield
        o_rows = []
        sgp = [s_gdn[p] for p in range(n_pair)]
        for c in range(TL // C):
            o_pair = []
            for p in range(n_pair):
                cp = (c, p)
                s_b = sgp[p].astype(BF16)
                o_pair.append(bdot(qp_[cp], s_b) + op_[cp])
                sgp[p] = sgp[p] * dl_[cp] - bdot(mm_[cp], s_b) + nn_[cp]
            o_rows.append(jnp.concatenate(o_pair, axis=1))
            yield
        for p in range(n_pair):
            s_gdn[p] = sgp[p]
        o_d = jnp.concatenate(o_rows, axis=0)
        out["d"] = o_d * lax.rsqrt(_dot(o_d * o_d, m64) + EPS) * prm["gdn_g"][...] * _silu(zd[:, 768:1024])

    def gla_stages():
        zg = proj(Z_GLA, Z_GLA + 768)
        q, k, v, r = zg[:, 0:128], zg[:, 128:256], zg[:, 256:512], zg[:, 512:768]
        gk = _dot(sm, prm["w_a2"][...]) + prm["b_a"][...]
        gk = (jnp.minimum(gk, 0.0) - jnp.log1p(jnp.exp(-jnp.abs(gk)))) / GLA_GATE_NORM
        yield
        bcum = _dot_exact_lhs(ltri, gk)
        caus4 = cst["caus4"][...]
        bmt = cst["bmt"][...]
        bmtb = bmt.astype(BF16)
        st = st_gla[...]
        yield
        o_chunks = []
        for c in range(TL // C):
            sl = slice(c * C, (c + 1) * C)
            b_c = bcum[sl]
            b_last = b_c[C - 1:C, :]
            q_t = q[sl] * jnp.exp(b_c) * (GLA_DK ** -0.5)
            k_t = k[sl] * jnp.exp(-b_c)
            k_end = k[sl] * jnp.exp(b_last - b_c)
            v_c = v[sl]
            q_b = q_t.astype(BF16)
            att = lax.dot_general(q_b, stack4b(k_t) * bmtb, (((1,), (1,)), ((), ())),
                                  preferred_element_type=F32) * caus4
            cross = lax.dot_general(q_b, st.astype(BF16), (((1,), (1,)), ((), ())),
                                    preferred_element_type=F32)
            st = st * jnp.exp(b_last) + bmt * _dot(v_c.T, k_end)
            yield
            o_chunks.append(jnp.dot(att.astype(BF16), stack4b(v_c) * bmb, preferred_element_type=F32)
                            + cross)
            yield
        st_gla[...] = st
        o_a = jnp.concatenate(o_chunks, axis=0)
        out["a"] = o_a * lax.rsqrt(_dot(o_a * o_a, m64) + EPS) * prm["gla_g"][...] * _silu(r)

    def ret_stages():
        zr = proj(Z_RET, Z_RET + 1024)
        cos2 = jnp.concatenate([cos_ref[...], cos_ref[...]], axis=1)
        sin2 = jnp.concatenate([sin_ref[...], sin_ref[...]], axis=1)
        first_half = (lax.broadcasted_iota(jnp.int32, (TL, 256), 1) % HEAD_DIM) < HEAD_DIM // 2

        def rope(x):
            rot = jnp.where(first_half, pltpu.roll(x, 256 - HEAD_DIM // 2, 1),
                            pltpu.roll(x, HEAD_DIM // 2, 1))
            return x * cos2 + rot * sin2

        rq = rope(zr[:, 0:256])
        rk = rope(zr[:, 256:512]) * (HEAD_DIM ** -0.5)
        rv = zr[:, 512:768]
        rg = zr[:, 768:1024]
        yield
        rq_b = rq.astype(BF16)
        qs = jnp.concatenate([rq_b * hm256b[hh:hh + 1] for hh in range(H)], axis=0)
        sc = lax.dot_general(qs, rk.astype(BF16), (((1,), (1,)), ((), ())),
                             preferred_element_type=F32) * cst["dmask"][...]
        s_prev = s_ret[...]
        cross = jnp.dot(rq_b, s_prev.astype(BF16), preferred_element_type=F32) * cst["xi"][...]
        s_ret[...] = s_prev * cst["gchunk"][...] + bm * _dot((rk * cst["zeta"][...]).T, rv)
        yield
        sc_wide = jnp.concatenate([sc[hh * TL:(hh + 1) * TL] for hh in range(H)], axis=1).astype(BF16)
        o_c = jnp.dot(sc_wide, stack4b(rv) * cst["bm4"][...].astype(BF16), preferred_element_type=F32) + cross
        yield
        o_c = o_c - _dot_exact_rhs(o_c, m64)
        yield
        out["c"] = o_c * lax.rsqrt(_dot(o_c * o_c, m64) + EPS) * prm["ret_g"][...] * _silu(rg)

    def s5_stages():
        y = ys5_ref[...]
        y = 0.5 * y * (1.0 + jnp.tanh(math.sqrt(2.0 / math.pi) * (y + 0.044715 * (y * y * y))))
        yield
        out["b"] = y * _sigmoid(_dot(y, prm["w_glu"][...]) + prm["b_glu"][...])

    running = [gdn_stages(), gla_stages(), ret_stages(), s5_stages()]
    while running:
        for gen in list(running):
            if next(gen, "done") == "done":
                running.remove(gen)

    mix = jnp.concatenate([out["a"], out["b"], out["c"], out["d"]], axis=1).astype(BF16)
    o_ref[...] = h + jnp.dot(mix, prm["w_out"][...], preferred_element_type=F32)


def _mixer_call(h, ys5, cos_t, sin_t, consts, params, layer):
    B, L, D = h.shape
    TL = MIX_TL

    def const(arr):
        nd = arr.ndim
        return pl.BlockSpec(arr.shape, lambda b, i: (0,) * nd, pipeline_mode=pl.Buffered(1))

    def tile(width):
        return pl.BlockSpec((None, TL, width), lambda b, i: (b, i, 0))

    c_list = [consts[n] for n in _CONST_ORDER]
    p_list = [params[n] for n in _PARAM_ORDER]
    return pl.pallas_call(
        _mixer_kernel,
        grid=(B, L // TL),
        in_specs=[tile(D), tile(GROUP_WIDTH), tile(128), tile(128)] + [const(a) for a in c_list]
        + [_layer_spec(a, layer, 2) for a in p_list],
        out_specs=tile(D),
        out_shape=jax.ShapeDtypeStruct((B, L, D), F32),
        scratch_shapes=[pltpu.VMEM((256, 128), F32), pltpu.VMEM((256, 256), F32), pltpu.VMEM((2, 128, 128), F32),
                        pltpu.VMEM((TL + 8, 3 * GROUP_WIDTH), F32)],
        compiler_params=pltpu.CompilerParams(dimension_semantics=("arbitrary", "arbitrary"),
                                             vmem_limit_bytes=VMEM_LIMIT),
        name="mixer",
    )(h, ys5, cos_t, sin_t, *c_list, *p_list)


def _ffn_kernel(h_ref, p_ref, gf_ref, wup_ref, wdn_ref, gp_ref, wg_ref, wp_ref, gl_ref, o_ref, act_ref, *,
                final):
    h = h_ref[...]
    hn = _rms(h, gf_ref[...]).astype(BF16)
    for s in range(FFN_HIDDEN // FFN_SLAB):
        lo = s * FFN_SLAB
        g = jnp.dot(hn, wup_ref[:, lo:lo + FFN_SLAB], preferred_element_type=F32)
        u = jnp.dot(hn, wup_ref[:, FFN_HIDDEN + lo:FFN_HIDDEN + lo + FFN_SLAB], preferred_element_type=F32)
        act_ref[:, lo:lo + FFN_SLAB] = (_silu(g) * u).astype(BF16)
    h2 = h + jnp.dot(act_ref[...], wdn_ref[...], preferred_element_type=F32)
    gate = _sigmoid(jnp.dot(_rms(h2, gp_ref[...]).astype(BF16), wg_ref[...], preferred_element_type=F32))
    h3 = h2 + jnp.dot(p_ref[...].astype(BF16), wp_ref[...], preferred_element_type=F32) * gate
    if final:
        h3 = _rms(h3, gl_ref[...])
    o_ref[...] = h3


def _ffn_call(h2d, p3d, layer, weights, final):
    T, D = h2d.shape
    TM = FFN_TM
    return pl.pallas_call(
        functools.partial(_ffn_kernel, final=final),
        grid=(T // TM,),
        in_specs=[pl.BlockSpec((TM, D), lambda i: (i, 0)),
                  pl.BlockSpec((None, TM, PLE_DIM), lambda i: (layer, i, 0))]
        + [_layer_spec(w, layer, 1) for w in weights],
        out_specs=pl.BlockSpec((TM, D), lambda i: (i, 0)),
        out_shape=jax.ShapeDtypeStruct((T, D), F32),
        scratch_shapes=[pltpu.VMEM((TM, FFN_HIDDEN), BF16)],
        compiler_params=pltpu.CompilerParams(dimension_semantics=("arbitrary",), vmem_limit_bytes=VMEM_LIMIT),
        name="ffn_ple",
    )(h2d, p3d, *weights)


def _row(v):
    return v[:, None, :]


def _cast_kernel(x_ref, o_ref):
    o_ref[...] = x_ref[...].astype(BF16)


def _to_bf16(w):
    depth, K, N = w.shape
    rows = depth * K
    blk = 512 if rows % 512 == 0 else 256
    out = pl.pallas_call(
        _cast_kernel,
        grid=(rows // blk,),
        in_specs=[pl.BlockSpec((blk, N), lambda i: (i, 0))],
        out_specs=pl.BlockSpec((blk, N), lambda i: (i, 0)),
        out_shape=jax.ShapeDtypeStruct((rows, N), BF16),
        name="weight_to_bf16",
    )(w.reshape(rows, N))
    return out.reshape(depth, K, N)


_WZ_SEGMENTS = (_AQ, _AK, _AV, _AR, _RQ, _RK, _RV, _RG, _DQ, _DK, _DV, _DG)


def _regroup_kernel(wt_ref, wz_ref, wsu_ref):
    col = 0
    for lo, hi in _WZ_SEGMENTS:
        wz_ref[:, col:col + hi - lo] = wt_ref[lo:hi, :].T.astype(BF16)
        col += hi - lo
    assert col == Z_SMALL and _DA[0] == _DB[1] and (_DA[1] - _DB[0]) % 8 == 0
    small = jnp.concatenate([wt_ref[_ALOW[0]:_ALOW[1], :], wt_ref[_DB[0]:_DA[1], :],
                             jnp.zeros((128 - GLA_RANK - 2 * NUM_HEADS, D_MODEL), F32)], axis=0)
    wz_ref[:, Z_SMALL:Z_WIDTH] = small.T.astype(BF16)
    wsu_ref[...] = wt_ref[_SU[0]:_SU[1], :].T.astype(BF16)


def _regroup_w_in(w_in):
    depth, K, N = w_in.shape
    return pl.pallas_call(
        _regroup_kernel,
        grid=(depth,),
        in_specs=[pl.BlockSpec((None, N, K), lambda d: (d, 0, 0))],
        out_specs=[pl.BlockSpec((None, K, Z_WIDTH), lambda d: (d, 0, 0)),
                   pl.BlockSpec((None, K, GROUP_WIDTH), lambda d: (d, 0, 0))],
        out_shape=[jax.ShapeDtypeStruct((depth, K, Z_WIDTH), BF16),
                   jax.ShapeDtypeStruct((depth, K, GROUP_WIDTH), BF16)],
        compiler_params=pltpu.CompilerParams(vmem_limit_bytes=VMEM_LIMIT),
        name="regroup_w_in",
    )(jnp.swapaxes(w_in, 1, 2))


def _mixer_params(norm_mix, w_z, w_out, gla_w_a2, gla_b_a, gla_norm, ret_norm, gdn_conv, gdn_a_log,
                  gdn_dt_bias, gdn_norm, s5_w_glu, s5_b_glu):
    w_a2 = jnp.pad(gla_w_a2, ((0, 0), (0, 128 - GLA_RANK), (0, 0)))
    lane_pad = ((0, 0), (SM_A, 128 - SM_A - NUM_HEADS))
    return dict(norm_mix=_row(norm_mix), w_z=w_z, w_out=_to_bf16(w_out), w_a2=w_a2, b_a=_row(gla_b_a),
                gla_g=_row(jnp.tile(gla_norm, (1, NUM_HEADS))), ret_g=_row(ret_norm),
                gdn_g=_row(jnp.tile(gdn_norm, (1, NUM_HEADS))), conv_w=gdn_conv,
                alog_v=_row(jnp.pad(gdn_a_log, lane_pad)), dtb_v=_row(jnp.pad(gdn_dt_bias, lane_pad)),
                w_glu=_to_bf16(s5_w_glu), b_glu=_row(s5_b_glu))


def kernel(x, p, positions, norm_mix, w_in, w_out, gla_w_a2, gla_b_a, gla_norm, s5_lam_re, s5_lam_im, s5_log_dt,
           s5_b_re, s5_b_im, s5_c_re, s5_c_im, s5_d, s5_w_glu, s5_b_glu, ret_norm, gdn_conv, gdn_a_log,
           gdn_dt_bias, gdn_norm, norm_ffn, w_ffn_up, w_ffn_down, norm_ple, w_ple_gate, w_ple_proj, norm_final):
    B, L, D = x.shape
    depth = w_in.shape[0]
    assert D == D_MODEL and L % max(MIX_TL, S5_SUB * S5_ROWS) == 0 and (B * L) % FFN_TM == 0
    consts = {k: jnp.asarray(v) for k, v in _mixer_consts().items()}
    cos_t, sin_t = _rope_tables(positions)
    prep = _s5_prepare(s5_lam_re, s5_lam_im, s5_log_dt, s5_b_re, s5_b_im, s5_c_re, s5_c_im, s5_d)
    w_z, w_su = _regroup_w_in(w_in)
    params = _mixer_params(norm_mix, w_z, w_out, gla_w_a2, gla_b_a, gla_norm, ret_norm, gdn_conv,
                           gdn_a_log, gdn_dt_bias, gdn_norm, s5_w_glu, s5_b_glu)
    ffn_w = [_row(norm_ffn), _to_bf16(w_ffn_up), _to_bf16(w_ffn_down), _row(norm_ple),
             _to_bf16(w_ple_gate), _to_bf16(w_ple_proj),
             jnp.broadcast_to(norm_final[None, None, :], (depth, 1, D))]
    p3d = p.reshape(depth, B * L, PLE_DIM)
    h = x
    for i in range(depth):
        ys5 = _s5_call(h, i, _row(norm_mix), w_su, prep)
        h = _mixer_call(h, ys5, cos_t, sin_t, consts, params, i)
        h = _ffn_call(h.reshape(B * L, D), p3d, i, ffn_w, final=(i == depth - 1)).reshape(B, L, D)
    return h
```

```python
import functools
import math

import numpy as np
import jax
import jax.numpy as jnp
from jax import lax
from jax.experimental import pallas as pl
from jax.experimental.pallas import tpu as pltpu

F32 = jnp.float32
BF16 = jnp.bfloat16
HI = lax.Precision.HIGHEST

D_MODEL = 1024
PLE_DIM = 256
GROUP_WIDTH = 256
NUM_HEADS = 4
CHUNK = 64
EPS = 1e-6

GLA_DK = 32
GLA_RANK = 16
GLA_GATE_NORM = 16.0
S5_CH = 16
S5_GROUPS = 16
S5_STATE = 64
S5_SUB = 8
HEAD_DIM = 64
ROPE_BASE = 10000.0
GDN_CONV = 4
FFN_HIDDEN = 2816

MIX_TL = 256
MIX_SLOTS = 2
S5_ROWS = 256
S5_PARTS = 2
FFN_TM = 1024
FFN_SLAB = 256

VMEM_LIMIT = 56 * 1024 * 1024

_IN_OFFS = np.cumsum([0, 128, 128, 256, 16, 256, 256, 256, 256, 256, 256, 256, 256, 256, 4, 4, 256])
(_AQ, _AK, _AV, _ALOW, _AR, _SU, _RQ, _RK, _RV, _RG, _DQ, _DK, _DV, _DB, _DA, _DG) = [
    (int(_IN_OFFS[i]), int(_IN_OFFS[i + 1])) for i in range(16)]
Z_GLA = 0
Z_RET = 768
Z_GDN = 1792
Z_SMALL = 2816
Z_WIDTH = 2944
SM_BETA = 16
SM_A = 20


def _dot(a, b):
    return jnp.dot(a.astype(BF16), b.astype(BF16), preferred_element_type=F32)


def _dot_nt(a, b):
    return lax.dot_general(a.astype(BF16), b.astype(BF16), (((1,), (1,)), ((), ())),
                           preferred_element_type=F32)


def _split(x):
    hi = x.astype(BF16)
    return hi, (x - hi.astype(F32)).astype(BF16)


def _dot_exact_lhs(a, b):
    hi, lo = _split(b)
    n = b.shape[1]
    if n <= 128:
        r = jnp.dot(a.astype(BF16), jnp.concatenate([hi, lo], axis=1), preferred_element_type=F32)
        return r[:, :n] + r[:, n:]
    return (jnp.dot(a.astype(BF16), hi, preferred_element_type=F32)
            + jnp.dot(a.astype(BF16), lo, preferred_element_type=F32))


def _dot_exact_rhs(a, b):
    hi, lo = _split(a)
    m = a.shape[0]
    r = jnp.dot(jnp.concatenate([hi, lo], axis=0), b.astype(BF16), preferred_element_type=F32)
    return r[:m] + r[m:]


def _sigmoid(x):
    return 0.5 * jnp.tanh(0.5 * x) + 0.5


def _silu(x):
    return x * _sigmoid(x)


def _softplus(x):
    return jnp.maximum(x, 0.0) + jnp.log1p(jnp.exp(-jnp.abs(x)))


def _rms(x, g):
    return x * lax.rsqrt(jnp.mean(x * x, axis=-1, keepdims=True) + EPS) * g


def _stack4(x):
    return jnp.concatenate([x, x, x, x], axis=0)


def _rope_kernel(pos_ref, invf_ref, cos_ref, sin_ref):
    ang = invf_ref[...] * pos_ref[...].astype(F32)
    c = jnp.cos(ang)
    s = jnp.sin(ang)
    cos_ref[...] = jnp.concatenate([c, c, c, c], axis=0).T
    sin_ref[...] = jnp.concatenate([-s, s, -s, s], axis=0).T


def _rope_tables(positions):
    B, L = positions.shape
    inv_freq = ROPE_BASE ** (-jnp.linspace(0.0, 1.0, HEAD_DIM // 2, dtype=F32))
    spec = pl.BlockSpec((None, L, 128), lambda b: (b, 0, 0))
    return pl.pallas_call(
        _rope_kernel,
        grid=(B,),
        in_specs=[pl.BlockSpec((None, 1, L), lambda b: (b, 0, 0)),
                  pl.BlockSpec((HEAD_DIM // 2, 1), lambda b: (0, 0))],
        out_specs=[spec, spec],
        out_shape=[jax.ShapeDtypeStruct((B, L, 128), F32)] * 2,
        name="rope_tables",
    )(positions[:, None, :], inv_freq[:, None])


def _s5_kernel(h_ref, g_ref, wsu_ref, krev_ref, bbig_ref, cbig_ref, pwr_ref, pwi_ref, d8_ref,
               y_ref, cr_ref, ci_ref, su_scr, y_scr):
    n_state = S5_GROUPS * S5_STATE
    W = GROUP_WIDTH
    n_slab = W // 128

    @pl.when(pl.program_id(1) == 0)
    def _():
        cr_ref[...] = jnp.zeros_like(cr_ref)
        ci_ref[...] = jnp.zeros_like(ci_ref)

    R = S5_ROWS // S5_PARTS
    tok = R * S5_SUB
    n_sub = S5_SUB
    carry = {"r": cr_ref[...], "i": ci_ref[...], "parts_done": 0}
    rows = lax.broadcasted_iota(jnp.int32, (R, n_state), 0)
    first = rows == 0

    def shifted(x, shift):
        if shift % 8 == 0:
            return jnp.concatenate([jnp.zeros((shift, x.shape[1]), F32), x[:R - shift]], axis=0)
        return jnp.where(rows >= shift, pltpu.roll(x, shift, 0), 0.0)

    def part_stages(part):
        base = part * tok
        blk = tok // n_sub
        for q in range(n_sub):
            lo = base + q * blk
            su = _dot(_rms(h_ref[lo:lo + blk, :], g_ref[...]), wsu_ref[...])
            for s in range(n_slab):
                su_scr[s, lo:lo + blk, :] = su[:, s * 128:(s + 1) * 128]
            yield
        u8 = jnp.concatenate([su_scr[s, pl.ds(base + j, R, stride=S5_SUB), :]
                              for j in range(S5_SUB) for s in range(n_slab)], axis=1)
        u8b = u8.astype(BF16)
        y_loc, inc = [], []
        n_col = 2 * n_state // n_sub
        for t in range(n_sub):
            y_loc.append(jnp.dot(u8b[:, 0:(t + 1) * W], krev_ref[(S5_SUB - 1 - t) * W:S5_SUB * W, :],
                                 preferred_element_type=F32)
                         + d8_ref[:, t * W:(t + 1) * W] * u8[:, t * W:(t + 1) * W])
            inc.append(jnp.dot(u8b, bbig_ref[:, t * n_col:(t + 1) * n_col], preferred_element_type=F32))
            yield
        inc = jnp.concatenate(inc, axis=1)
        xr, xi = inc[:, :n_state], inc[:, n_state:]
        assert carry["parts_done"] == part
        cr, ci = carry["r"], carry["i"]
        ar0, ai0 = pwr_ref[0:1, :], pwi_ref[0:1, :]
        xr = xr + jnp.where(first, ar0 * cr - ai0 * ci, 0.0)
        xi = xi + jnp.where(first, ar0 * ci + ai0 * cr, 0.0)
        shift, lvl = 1, 0
        while shift < R:
            ar, ai = pwr_ref[lvl:lvl + 1, :], pwi_ref[lvl:lvl + 1, :]
            sr, si = shifted(xr, shift), shifted(xi, shift)
            xr, xi = xr + ar * sr - ai * si, xi + ar * si + ai * sr
            shift, lvl = shift * 2, lvl + 1
            yield
        pr = jnp.where(first, cr, pltpu.roll(xr, 1, 0))
        pi = jnp.where(first, ci, pltpu.roll(xi, 1, 0))
        carry.update(r=xr[R - 1:R, :], i=xi[R - 1:R, :], parts_done=part + 1)
        xprev = jnp.concatenate([pr, pi], axis=1).astype(BF16)
        yield
        for j in range(S5_SUB):
            yj = y_loc[j] + jnp.dot(xprev, cbig_ref[:, j * W:(j + 1) * W], preferred_element_type=F32)
            for s in range(n_slab):
                y_scr[s, pl.ds(base + j, R, stride=S5_SUB), :] = yj[:, s * 128:(s + 1) * 128]
            yield
        y_ref[base:base + tok, :] = jnp.concatenate([y_scr[s, base:base + tok, :] for s in range(n_slab)],
                                                    axis=1)

    gens = [part_stages(part) for part in range(S5_PARTS)]
    for lead in range(S5_PARTS):
        for _ in range(n_sub):
            for gen in gens[:lead + 1]:
                next(gen, None)
    live = list(gens)
    while live:
        for gen in list(live):
            if next(gen, "done") == "done":
                live.remove(gen)
    cr_ref[...] = carry["r"]
    ci_ref[...] = carry["i"]


def _s5_prepare(lam_re, lam_im, log_dt, b_re, b_im, c_re, c_im, d_skip):
    G, P, H, M = S5_GROUPS, S5_STATE, S5_CH, S5_SUB
    depth = lam_re.shape[0]
    lr = jnp.minimum(lam_re.astype(F32), -1e-4)
    li = lam_im.astype(F32)
    dt = jnp.exp(log_dt.astype(F32))[:, :, None]

    def apow(t):
        tt = jnp.asarray(t, F32)[None, :, None, None]
        mag = jnp.exp((lr * dt)[:, None] * tt)
        ang = (li * dt)[:, None] * tt
        return mag * jnp.cos(ang), mag * jnp.sin(ang)

    n_lvl = int(math.log2(S5_ROWS))
    exps = list(range(M + 1)) + list(range(M - 1, -1, -1)) + [M * 2 ** s for s in range(n_lvl)]
    pw_r, pw_i = apow(np.asarray(exps, np.float32))
    rev_r, rev_i = pw_r[:, M + 1:2 * M + 1], pw_i[:, M + 1:2 * M + 1]
    ar, ai = pw_r[:, 1], pw_i[:, 1]
    nr, ni = ar - 1.0, ai
    den = lr * lr + li * li
    fr = ((nr * lr + ni * li) / den)[..., None]
    fi = ((ni * lr - nr * li) / den)[..., None]
    bbr = fr * b_re - fi * b_im
    bbi = fr * b_im + fi * b_re

    def embed(narrow, tile, row_div, col_div):
        wide = jnp.einsum('dnk,km->dnm', narrow.astype(BF16), jnp.asarray(tile, BF16),
                          preferred_element_type=F32)
        rg = (lax.broadcasted_iota(jnp.int32, wide.shape, 1) // row_div) % G
        cg = (lax.broadcasted_iota(jnp.int32, wide.shape, 2) // col_div) % G
        return jnp.where(rg == cg, wide, 0.0).astype(BF16)

    pr, pi = rev_r[:, :, :, None, :], rev_i[:, :, :, None, :]
    car = c_re[:, None] * pr - c_im[:, None] * pi
    cai = c_re[:, None] * pi + c_im[:, None] * pr
    bbr_t = bbr.transpose(0, 1, 3, 2)[:, None, :, :, None, :]
    bbi_t = bbi.transpose(0, 1, 3, 2)[:, None, :, :, None, :]
    kt = jnp.sum(car[:, :, :, None] * bbr_t - cai[:, :, :, None] * bbi_t, axis=-1)
    k_narrow = kt.reshape(depth, M * G * H, H)
    k_tile = np.tile(np.eye(H, dtype=np.float32), (1, G))
    krev = embed(k_narrow, k_tile, H, H)

    pr, pi = rev_r[..., None], rev_i[..., None]
    b_narrow = jnp.stack([pr * bbr[:, None] - pi * bbi[:, None], pr * bbi[:, None] + pi * bbr[:, None]],
                         axis=1)
    b_narrow = b_narrow.transpose(0, 2, 3, 5, 1, 4).reshape(depth, M * G * H, 2 * P)
    cp = np.arange(2 * P)
    col = np.arange(2 * G * P)
    b_tile = ((cp[:, None] // P == (col // (G * P))[None, :]) & (cp[:, None] % P == (col % P)[None, :]))
    bbig = embed(b_narrow, b_tile.astype(np.float32), H, P)

    pr, pi = pw_r[:, 1:M + 1, :, None, :], pw_i[:, 1:M + 1, :, None, :]
    qr = c_re[:, None] * pr - c_im[:, None] * pi
    qi = -(c_re[:, None] * pi + c_im[:, None] * pr)
    c_narrow = jnp.stack([qr, qi], axis=1).transpose(0, 1, 3, 5, 2, 4).reshape(depth, 2 * G * P, M * H)
    th = np.arange(M * H)
    col = np.arange(M * G * H)
    c_tile = ((th[:, None] // H == (col // (G * H))[None, :]) & (th[:, None] % H == (col % H)[None, :]))
    cbig = embed(c_narrow, c_tile.astype(np.float32), P, H)

    pad = (-n_lvl) % 8
    pwr = jnp.pad(pw_r[:, 2 * M + 1:].reshape(depth, n_lvl, G * P), ((0, 0), (0, pad), (0, 0)))
    pwi = jnp.pad(pw_i[:, 2 * M + 1:].reshape(depth, n_lvl, G * P), ((0, 0), (0, pad), (0, 0)))
    d8 = jnp.tile(d_skip.astype(F32), (1, M))[:, None, :]
    return krev, bbig, cbig, pwr, pwi, d8


def _layer_spec(arr, layer, n_grid):
    nd = arr.ndim - 1
    zeros = (0,) * nd
    if n_grid == 1:
        return pl.BlockSpec((None,) + arr.shape[1:], lambda i: (layer,) + zeros, pipeline_mode=pl.Buffered(1))
    return pl.BlockSpec((None,) + arr.shape[1:], lambda b, i: (layer,) + zeros, pipeline_mode=pl.Buffered(1))


def _s5_call(h, layer, norm_g, w_su, prep):
    B, L, D = h.shape
    krev, bbig, cbig, pwr, pwi, d8 = prep
    tokens = S5_SUB * S5_ROWS
    n_state = S5_GROUPS * S5_STATE
    n_slab = GROUP_WIDTH // 128
    params = [norm_g, w_su, krev, bbig, cbig, pwr, pwi, d8]

    return pl.pallas_call(
        _s5_kernel,
        grid=(B, L // tokens),
        in_specs=[pl.BlockSpec((None, tokens, D), lambda b, i: (b, i, 0))]
        + [_layer_spec(a, layer, 2) for a in params],
        out_specs=pl.BlockSpec((None, tokens, GROUP_WIDTH), lambda b, i: (b, i, 0)),
        out_shape=jax.ShapeDtypeStruct((B, L, GROUP_WIDTH), F32),
        scratch_shapes=[pltpu.VMEM((1, n_state), F32), pltpu.VMEM((1, n_state), F32),
                        pltpu.VMEM((n_slab, tokens, 128), F32), pltpu.VMEM((n_slab, tokens, 128), F32)],
        compiler_params=pltpu.CompilerParams(dimension_semantics=("arbitrary", "arbitrary"),
                                             vmem_limit_bytes=VMEM_LIMIT),
        name="s5_mixer",
    )(h, *params)


def _mixer_consts():
    TL, C, H = MIX_TL, CHUNK, NUM_HEADS
    t = np.arange(TL)
    ltri = ((t[:, None] // C == t[None, :] // C) & (t[None, :] <= t[:, None])).astype(np.float32)
    lane256 = np.arange(256)
    lane128 = np.arange(128)
    hm128 = (lane128[None, :] // GLA_DK == np.arange(H)[:, None]).astype(np.float32)
    hm256 = (lane256[None, :] // HEAD_DIM == np.arange(H)[:, None]).astype(np.float32)
    bm = (lane256[:, None] // HEAD_DIM == lane256[None, :] // HEAD_DIM).astype(np.float32)
    bmt = (lane256[:, None] // HEAD_DIM == lane128[None, :] // GLA_DK).astype(np.float32)
    c = np.arange(C)
    caus4 = (c[:, None] >= (lane256 % HEAD_DIM)[None, :]).astype(np.float32)
    bm4 = np.repeat(hm256, TL, axis=0)
    s_lane = lane128 % HEAD_DIM
    bm2 = bm[0:128, 0:128]
    incl = (c[:, None] >= s_lane[None, :]).astype(np.float32)
    strict = (c[:, None] > s_lane[None, :]).astype(np.float32)
    idiag = (c[:, None] == s_lane[None, :]).astype(np.float32)
    log_gamma = np.log1p(-(2.0 ** (-5.0 - np.arange(H, dtype=np.float32)))).astype(np.float32)
    idx = np.arange(TL, dtype=np.float32)
    rel = idx[:, None] - idx[None, :]
    dmask = np.where(rel >= 0, np.exp(np.maximum(rel, 0.0)[None] * log_gamma[:, None, None]), 0.0)
    dmask = dmask.reshape(H * TL, TL).astype(np.float32)
    lg_lane = log_gamma[lane256 // HEAD_DIM]
    xi = np.exp((idx[:, None] + 1.0) * lg_lane[None, :]).astype(np.float32)
    zeta = np.exp((TL - 1.0 - idx[:, None]) * lg_lane[None, :]).astype(np.float32)
    gchunk = np.exp(TL * lg_lane)[None, :].astype(np.float32)
    e_beta = np.zeros((128, 256), np.float32)
    e_a = np.zeros((128, 256), np.float32)
    for h in range(H):
        e_beta[SM_BETA + h, h * HEAD_DIM:(h + 1) * HEAD_DIM] = 1.0
        e_a[SM_A + h, h * HEAD_DIM:(h + 1) * HEAD_DIM] = 1.0
    return dict(ltri=ltri, hm256=hm256, bm=bm, bmt=bmt, caus4=caus4, bm4=bm4, incl=incl, strict=strict,
                idiag=idiag, bm2=bm2, dmask=dmask, xi=xi, zeta=zeta, gchunk=gchunk, e_beta=e_beta, e_a=e_a,
                m64=bm / HEAD_DIM)


_CONST_ORDER = ("ltri", "hm256", "bm", "bmt", "caus4", "bm4", "incl", "strict", "idiag", "bm2", "dmask", "xi",
                "zeta", "gchunk", "e_beta", "e_a", "m64")
_PARAM_ORDER = ("norm_mix", "w_z", "w_out", "w_a2", "b_a", "gla_g", "ret_g", "gdn_g", "conv_w", "alog_v",
                "dtb_v", "w_glu", "b_glu")


def _mixer_kernel(*refs):
    n_c, n_p = len(_CONST_ORDER), len(_PARAM_ORDER)
    h_ref, ys5_ref, cos_ref, sin_ref = refs[:4]
    cst = dict(zip(_CONST_ORDER, refs[4:4 + n_c]))
    prm = dict(zip(_PARAM_ORDER, refs[4 + n_c:4 + n_c + n_p]))
    o_ref = refs[4 + n_c + n_p]
    st_gla, s_ret, s_gdn, xpad = refs[4 + n_c + n_p + 1:]
    TL, C, H, NS = MIX_TL, CHUNK, NUM_HEADS, MIX_SLOTS

    @pl.when(pl.program_id(1) == 0)
    def _():
        st_gla[...] = jnp.zeros_like(st_gla)
        s_ret[...] = jnp.zeros_like(s_ret)
        s_gdn[...] = jnp.zeros_like(s_gdn)
        xpad[:, 0:8, :] = jnp.zeros((NS, 8, 3 * GROUP_WIDTH), F32)

    hn = _rms(jnp.concatenate([h_ref[s] for s in range(NS)], axis=0), prm["norm_mix"][...]).astype(BF16)
    w_z = prm["w_z"]
    z_cache = {}

    def proj(slot, lo, hi):
        if (lo, hi) not in z_cache:
            z_cache[lo, hi] = jnp.dot(hn, w_z[:, lo:hi], preferred_element_type=F32)
        return z_cache[lo, hi][slot * TL:(slot + 1) * TL]

    hm256b = cst["hm256"][...].astype(BF16)
    bmb = cst["bm"][...].astype(BF16)
    bm = cst["bm"][...]
    m64 = cst["m64"][...]
    ltri = cst["ltri"][...]

    def stack4b(x):
        xb = x.astype(BF16)
        return jnp.concatenate([xb, xb, xb, xb], axis=0)

    out = {}

    def gdn_stages(slot):
        zd = proj(slot, Z_GDN, Z_GDN + 1024)
        sm = proj(slot, Z_SMALL, Z_WIDTH)
        xp = xpad.at[slot]
        xp[8:8 + TL, :] = zd[:, 0:768]
        cw = prm["conv_w"][...]
        xc = cw[GDN_CONV - 1:GDN_CONV, :] * zd[:, 0:768]
        for j in range(GDN_CONV - 1):
            xc = xc + cw[j:j + 1, :] * xp[5 + j:5 + j + TL, :]
        xp[0:8, :] = zd[TL - 8:TL, 0:768]
        xc = _silu(xc)
        dq, dk, dv = xc[:, 0:256], xc[:, 256:512], xc[:, 512:768]
        yield
        dq = dq * lax.rsqrt(_dot(dq * dq, bm) + EPS) * (HEAD_DIM ** -0.5)
        dk = dk * lax.rsqrt(_dot(dk * dk, bm) + EPS)
        beta = _dot_exact_rhs(_sigmoid(sm), cst["e_beta"][...])
        g_log = _dot_exact_rhs(-jnp.exp(prm["alog_v"][...]) * _softplus(sm + prm["dtb_v"][...]),
                               cst["e_a"][...])
        yield
        gcum = _dot_exact_lhs(ltri, g_log)
        yield
        incl = cst["incl"][...]
        strict = cst["strict"][...]
        idiag = cst["idiag"][...]
        bm2 = cst["bm2"][...]
        bm2b = bm2.astype(BF16)
        n_pair = NUM_HEADS // 2
        probs = [(c, p) for c in range(TL // C) for p in range(n_pair)]

        def blk(x, c, p):
            return x[c * C:(c + 1) * C, p * 128:(p + 1) * 128]

        def bd2(x):
            xb = x.astype(BF16)
            return jnp.concatenate([xb, xb], axis=0) * bm2b

        def bdot(a, b):
            return jnp.dot(a.astype(BF16), b, preferred_element_type=F32)

        gk_, gq_, gv_, gb_, gg_, p_, t_, aqk_ = {}, {}, {}, {}, {}, {}, {}, {}
        for cp in probs:
            k_c, q_c, g_c = blk(dk, *cp), blk(dq, *cp), blk(gcum, *cp)
            gk_[cp], gq_[cp], gv_[cp], gb_[cp], gg_[cp] = k_c, q_c, blk(dv, *cp), blk(beta, *cp), g_c
            g_row = jnp.sum(g_c * idiag, axis=0, keepdims=True)
            dec = jnp.where(incl > 0.0, jnp.exp(jnp.where(incl > 0.0, g_c - g_row, 0.0)), 0.0)
            kq = lax.dot_general(jnp.concatenate([k_c, q_c], axis=0).astype(BF16), bd2(k_c),
                                 (((1,), (1,)), ((), ())), preferred_element_type=F32)
            aqk_[cp] = kq[C:2 * C] * dec
            p_[cp] = -(kq[0:C] * dec * strict * gb_[cp])
            t_[cp] = idiag + p_[cp]
        yield
        for cp in probs:
            p_[cp] = bdot(p_[cp], bd2(p_[cp]))
        yield
        for _ in range(int(math.log2(C)) - 2):
            for cp in probs:
                pt = bdot(jnp.concatenate([p_[cp], t_[cp]], axis=0), bd2(p_[cp]))
                p_[cp] = pt[0:C]
                t_[cp] = t_[cp] + pt[C:2 * C]
            yield
        for cp in probs:
            t_[cp] = t_[cp] + bdot(t_[cp], bd2(p_[cp]))
        yield
        uw_, eg_ = {}, {}
        for cp in probs:
            eg_[cp] = jnp.exp(gg_[cp])
            b_c = gb_[cp]
            uw_[cp] = bdot(t_[cp], jnp.concatenate([bd2(gv_[cp] * b_c), bd2(gk_[cp] * b_c * eg_[cp])],
                                                   axis=1))
        yield
        mm_, nn_, dl_, qp_, op_ = {}, {}, {}, {}, {}
        for cp in probs:
            k_c, g_c = gk_[cp], gg_[cp]
            g_last = g_c[C - 1:C, :]
            u_c, w_c = uw_[cp][:, 0:128], uw_[cp][:, 128:256]
            k_end_t = (k_c * jnp.exp(g_last - g_c)).T
            mn = bdot(k_end_t, jnp.concatenate([w_c, u_c], axis=1).astype(BF16))
            mm_[cp] = bm2 * mn[:, 0:128]
            nn_[cp] = bm2 * mn[:, 128:256]
            dl_[cp] = jnp.exp(g_last)
            qo = bdot(aqk_[cp], jnp.concatenate([bd2(w_c), bd2(u_c)], axis=1))
            qp_[cp] = gq_[cp] * eg_[cp] - qo[:, 0:128]
            op_[cp] = qo[:, 128:256]
        yield
        o_rows = []
        sgp = [s_gdn[slot, p] for p in range(n_pair)]
        for c in range(TL // C):
            o_pair = []
            for p in range(n_pair):
                cp = (c, p)
                s_b = sgp[p].astype(BF16)
                o_pair.append(bdot(qp_[cp], s_b) + op_[cp])
                sgp[p] = sgp[p] * dl_[cp] - bdot(mm_[cp], s_b) + nn_[cp]
            o_rows.append(jnp.concatenate(o_pair, axis=1))
            yield
        for p in range(n_pair):
            s_gdn[slot, p] = sgp[p]
        o_d = jnp.concatenate(o_rows, axis=0)
        out[slot, "d"] = o_d * lax.rsqrt(_dot(o_d * o_d, m64) + EPS) * prm["gdn_g"][...] * _silu(zd[:, 768:1024])

    def gla_stages(slot):
        zg = proj(slot, Z_GLA, Z_GLA + 768)
        sm = proj(slot, Z_SMALL, Z_WIDTH)
        q, k, v, r = zg[:, 0:128], zg[:, 128:256], zg[:, 256:512], zg[:, 512:768]
        gk = _dot(sm, prm["w_a2"][...]) + prm["b_a"][...]
        gk = (jnp.minimum(gk, 0.0) - jnp.log1p(jnp.exp(-jnp.abs(gk)))) / GLA_GATE_NORM
        yield
        bcum = _dot_exact_lhs(ltri, gk)
        caus4 = cst["caus4"][...]
        bmt = cst["bmt"][...]
        bmtb = bmt.astype(BF16)
        st = st_gla[slot]
        yield
        o_chunks = []
        for c in range(TL // C):
            sl = slice(c * C, (c + 1) * C)
            b_c = bcum[sl]
            b_last = b_c[C - 1:C, :]
            q_t = q[sl] * jnp.exp(b_c) * (GLA_DK ** -0.5)
            k_t = k[sl] * jnp.exp(-b_c)
            k_end = k[sl] * jnp.exp(b_last - b_c)
            v_c = v[sl]
            q_b = q_t.astype(BF16)
            att = lax.dot_general(q_b, stack4b(k_t) * bmtb, (((1,), (1,)), ((), ())),
                                  preferred_element_type=F32) * caus4
            cross = lax.dot_general(q_b, st.astype(BF16), (((1,), (1,)), ((), ())),
                                    preferred_element_type=F32)
            st = st * jnp.exp(b_last) + bmt * _dot(v_c.T, k_end)
            yield
            o_chunks.append(jnp.dot(att.astype(BF16), stack4b(v_c) * bmb, preferred_element_type=F32)
                            + cross)
            yield
        st_gla[slot] = st
        o_a = jnp.concatenate(o_chunks, axis=0)
        out[slot, "a"] =o_a * lax.rsqrt(_dot(o_a * o_a, m64) + EPS) * prm["gla_g"][...] * _silu(r)

    def ret_stages(slot):
        zr = proj(slot, Z_RET, Z_RET + 1024)
        cos2 = jnp.concatenate([cos_ref[slot], cos_ref[slot]], axis=1)
        sin2 = jnp.concatenate([sin_ref[slot], sin_ref[slot]], axis=1)
        first_half = (lax.broadcasted_iota(jnp.int32, (TL, 256), 1) % HEAD_DIM) < HEAD_DIM // 2

        def rope(x):
            rot = jnp.where(first_half, pltpu.roll(x, 256 - HEAD_DIM // 2, 1),
                            pltpu.roll(x, HEAD_DIM // 2, 1))
            return x * cos2 + rot * sin2

        rq = rope(zr[:, 0:256])
        rk = rope(zr[:, 256:512]) * (HEAD_DIM ** -0.5)
        rv = zr[:, 512:768]
        rg = zr[:, 768:1024]
        yield
        rq_b = rq.astype(BF16)
        qs = jnp.concatenate([rq_b * hm256b[hh:hh + 1] for hh in range(H)], axis=0)
        sc = lax.dot_general(qs, rk.astype(BF16), (((1,), (1,)), ((), ())),
                             preferred_element_type=F32) * cst["dmask"][...]
        s_prev = s_ret[slot]
        cross = jnp.dot(rq_b, s_prev.astype(BF16), preferred_element_type=F32) * cst["xi"][...]
        s_ret[slot] = s_prev * cst["gchunk"][...] + bm * _dot((rk * cst["zeta"][...]).T, rv)
        yield
        sc_wide = jnp.concatenate([sc[hh * TL:(hh + 1) * TL] for hh in range(H)], axis=1).astype(BF16)
        o_c = jnp.dot(sc_wide, stack4b(rv) * cst["bm4"][...].astype(BF16), preferred_element_type=F32) + cross
        yield
        o_c = o_c - _dot_exact_rhs(o_c, m64)
        yield
        out[slot, "c"] = o_c * lax.rsqrt(_dot(o_c * o_c, m64) + EPS) * prm["ret_g"][...] * _silu(rg)

    def s5_stages(slot):
        y = ys5_ref[slot]
        y = 0.5 * y * (1.0 + jnp.tanh(math.sqrt(2.0 / math.pi) * (y + 0.044715 * (y * y * y))))
        yield
        out[slot, "b"] = y * _sigmoid(_dot(y, prm["w_glu"][...]) + prm["b_glu"][...])

    running = [stages(slot) for stages in (gdn_stages, gla_stages, ret_stages, s5_stages) for slot in range(NS)]
    while running:
        for gen in list(running):
            if next(gen, "done") == "done":
                running.remove(gen)

    mix = jnp.concatenate([jnp.concatenate([out[s, "a"], out[s, "b"], out[s, "c"], out[s, "d"]], axis=1)
                           for s in range(NS)], axis=0).astype(BF16)
    res = jnp.dot(mix, prm["w_out"][...], preferred_element_type=F32)
    for s in range(NS):
        o_ref[s] = h_ref[s] + res[s * TL:(s + 1) * TL]


def _mixer_call(h, ys5, cos_t, sin_t, consts, params, layer):
    B, L, D = h.shape
    TL, NS = MIX_TL, MIX_SLOTS

    def const(arr):
        nd = arr.ndim
        return pl.BlockSpec(arr.shape, lambda b, i: (0,) * nd, pipeline_mode=pl.Buffered(1))

    def tile(width):
        return pl.BlockSpec((NS, TL, width), lambda b, i: (b, i, 0))

    c_list = [consts[n] for n in _CONST_ORDER]
    p_list = [params[n] for n in _PARAM_ORDER]
    return pl.pallas_call(
        _mixer_kernel,
        grid=(B // NS, L // TL),
        in_specs=[tile(D), tile(GROUP_WIDTH), tile(128), tile(128)] + [const(a) for a in c_list]
        + [_layer_spec(a, layer, 2) for a in p_list],
        out_specs=tile(D),
        out_shape=jax.ShapeDtypeStruct((B, L, D), F32),
        scratch_shapes=[pltpu.VMEM((NS, 256, 128), F32), pltpu.VMEM((NS, 256, 256), F32),
                        pltpu.VMEM((NS, 2, 128, 128), F32), pltpu.VMEM((NS, TL + 8, 3 * GROUP_WIDTH), F32)],
        compiler_params=pltpu.CompilerParams(dimension_semantics=("arbitrary", "arbitrary"),
                                             vmem_limit_bytes=VMEM_LIMIT),
        name="mixer",
    )(h, ys5, cos_t, sin_t, *c_list, *p_list)


def _ffn_kernel(h_ref, p_ref, gf_ref, wup_ref, wdn_ref, gp_ref, wg_ref, wp_ref, gl_ref, o_ref, act_ref, *,
                final):
    h = h_ref[...]
    hn = _rms(h, gf_ref[...]).astype(BF16)
    for s in range(FFN_HIDDEN // FFN_SLAB):
        lo = s * FFN_SLAB
        g = jnp.dot(hn, wup_ref[:, lo:lo + FFN_SLAB], preferred_element_type=F32)
        u = jnp.dot(hn, wup_ref[:, FFN_HIDDEN + lo:FFN_HIDDEN + lo + FFN_SLAB], preferred_element_type=F32)
        act_ref[:, lo:lo + FFN_SLAB] = (_silu(g) * u).astype(BF16)
    h2 = h + jnp.dot(act_ref[...], wdn_ref[...], preferred_element_type=F32)
    gate = _sigmoid(jnp.dot(_rms(h2, gp_ref[...]).astype(BF16), wg_ref[...], preferred_element_type=F32))
    h3 = h2 + jnp.dot(p_ref[...].astype(BF16), wp_ref[...], preferred_element_type=F32) * gate
    if final:
        h3 = _rms(h3, gl_ref[...])
    o_ref[...] = h3


def _ffn_call(h2d, p3d, layer, weights, final):
    T, D = h2d.shape
    TM = FFN_TM
    return pl.pallas_call(
        functools.partial(_ffn_kernel, final=final),
        grid=(T // TM,),
        in_specs=[pl.BlockSpec((TM, D), lambda i: (i, 0)),
                  pl.BlockSpec((None, TM, PLE_DIM), lambda i: (layer, i, 0))]
        + [_layer_spec(w, layer, 1) for w in weights],
        out_specs=pl.BlockSpec((TM, D), lambda i: (i, 0)),
        out_shape=jax.ShapeDtypeStruct((T, D), F32),
        scratch_shapes=[pltpu.VMEM((TM, FFN_HIDDEN), BF16)],
        compiler_params=pltpu.CompilerParams(dimension_semantics=("arbitrary",), vmem_limit_bytes=VMEM_LIMIT),
        name="ffn_ple",
    )(h2d, p3d, *weights)


def _row(v):
    return v[:, None, :]


def _cast_kernel(x_ref, o_ref):
    o_ref[...] = x_ref[...].astype(BF16)


def _to_bf16(w):
    depth, K, N = w.shape
    rows = depth * K
    blk = 512 if rows % 512 == 0 else 256
    out = pl.pallas_call(
        _cast_kernel,
        grid=(rows // blk,),
        in_specs=[pl.BlockSpec((blk, N), lambda i: (i, 0))],
        out_specs=pl.BlockSpec((blk, N), lambda i: (i, 0)),
        out_shape=jax.ShapeDtypeStruct((rows, N), BF16),
        name="weight_to_bf16",
    )(w.reshape(rows, N))
    return out.reshape(depth, K, N)


_WZ_SEGMENTS = (_AQ, _AK, _AV, _AR, _RQ, _RK, _RV, _RG, _DQ, _DK, _DV, _DG)


def _regroup_kernel(wt_ref, wz_ref, wsu_ref):
    col = 0
    for lo, hi in _WZ_SEGMENTS:
        wz_ref[:, col:col + hi - lo] = wt_ref[lo:hi, :].T.astype(BF16)
        col += hi - lo
    assert col == Z_SMALL and _DA[0] == _DB[1] and (_DA[1] - _DB[0]) % 8 == 0
    small = jnp.concatenate([wt_ref[_ALOW[0]:_ALOW[1], :], wt_ref[_DB[0]:_DA[1], :],
                             jnp.zeros((128 - GLA_RANK - 2 * NUM_HEADS, D_MODEL), F32)], axis=0)
    wz_ref[:, Z_SMALL:Z_WIDTH] = small.T.astype(BF16)
    wsu_ref[...] = wt_ref[_SU[0]:_SU[1], :].T.astype(BF16)


def _regroup_w_in(w_in):
    depth, K, N = w_in.shape
    return pl.pallas_call(
        _regroup_kernel,
        grid=(depth,),
        in_specs=[pl.BlockSpec((None, N, K), lambda d: (d, 0, 0))],
        out_specs=[pl.BlockSpec((None, K, Z_WIDTH), lambda d: (d, 0, 0)),
                   pl.BlockSpec((None, K, GROUP_WIDTH), lambda d: (d, 0, 0))],
        out_shape=[jax.ShapeDtypeStruct((depth, K, Z_WIDTH), BF16),
                   jax.ShapeDtypeStruct((depth, K, GROUP_WIDTH), BF16)],
        compiler_params=pltpu.CompilerParams(vmem_limit_bytes=VMEM_LIMIT),
        name="regroup_w_in",
    )(jnp.swapaxes(w_in, 1, 2))


def _mixer_params(norm_mix, w_z, w_out, gla_w_a2, gla_b_a, gla_norm, ret_norm, gdn_conv, gdn_a_log,
                  gdn_dt_bias, gdn_norm, s5_w_glu, s5_b_glu):
    w_a2 = jnp.pad(gla_w_a2, ((0, 0), (0, 128 - GLA_RANK), (0, 0)))
    lane_pad = ((0, 0), (SM_A, 128 - SM_A - NUM_HEADS))
    return dict(norm_mix=_row(norm_mix), w_z=w_z, w_out=_to_bf16(w_out), w_a2=w_a2, b_a=_row(gla_b_a),
                gla_g=_row(jnp.tile(gla_norm, (1, NUM_HEADS))), ret_g=_row(ret_norm),
                gdn_g=_row(jnp.tile(gdn_norm, (1, NUM_HEADS))), conv_w=gdn_conv,
                alog_v=_row(jnp.pad(gdn_a_log, lane_pad)), dtb_v=_row(jnp.pad(gdn_dt_bias, lane_pad)),
                w_glu=_to_bf16(s5_w_glu), b_glu=_row(s5_b_glu))


def kernel(x, p, positions, norm_mix, w_in, w_out, gla_w_a2, gla_b_a, gla_norm, s5_lam_re, s5_lam_im, s5_log_dt,
           s5_b_re, s5_b_im, s5_c_re, s5_c_im, s5_d, s5_w_glu, s5_b_glu, ret_norm, gdn_conv, gdn_a_log,
           gdn_dt_bias, gdn_norm, norm_ffn, w_ffn_up, w_ffn_down, norm_ple, w_ple_gate, w_ple_proj, norm_final):
    B, L, D = x.shape
    depth = w_in.shape[0]
    assert D == D_MODEL and L % max(MIX_TL, S5_SUB * S5_ROWS) == 0 and (B * L) % FFN_TM == 0
    assert B % MIX_SLOTS == 0
    consts = {k: jnp.asarray(v) for k, v in _mixer_consts().items()}
    cos_t, sin_t = _rope_tables(positions)
    prep = _s5_prepare(s5_lam_re, s5_lam_im, s5_log_dt, s5_b_re, s5_b_im, s5_c_re, s5_c_im, s5_d)
    w_z, w_su = _regroup_w_in(w_in)
    params = _mixer_params(norm_mix, w_z, w_out, gla_w_a2, gla_b_a, gla_norm, ret_norm, gdn_conv,
                           gdn_a_log, gdn_dt_bias, gdn_norm, s5_w_glu, s5_b_glu)
    ffn_w = [_row(norm_ffn), _to_bf16(w_ffn_up), _to_bf16(w_ffn_down), _row(norm_ple),
             _to_bf16(w_ple_gate), _to_bf16(w_ple_proj),
             jnp.broadcast_to(norm_final[None, None, :], (depth, 1, D))]
    p3d = p.reshape(depth, B * L, PLE_DIM)
    h = x
    for i in range(depth):
        ys5 = _s5_call(h, i, _row(norm_mix), w_su, prep)
        h = _mixer_call(h, ys5, cos_t, sin_t, consts, params, i)
        h = _ffn_call(h.reshape(B * L, D), p3d, i, ffn_w, final=(i == depth - 1)).reshape(B, L, D)
    return h
```

```python
import functools
import math

import numpy as np
import jax
import jax.numpy as jnp
from jax import lax
from jax.experimental import pallas as pl
from jax.experimental.pallas import tpu as pltpu

F32 = jnp.float32
BF16 = jnp.bfloat16
HI = lax.Precision.HIGHEST

D_MODEL = 1024
PLE_DIM = 256
GROUP_WIDTH = 256
NUM_HEADS = 4
CHUNK = 64
EPS = 1e-6

GLA_DK = 32
GLA_RANK = 16
GLA_GATE_NORM = 16.0
S5_CH = 16
S5_GROUPS = 16
S5_STATE = 64
S5_SUB = 8
HEAD_DIM = 64
ROPE_BASE = 10000.0
GDN_CONV = 4
FFN_HIDDEN = 2816

MIX_TL = 256
MIX_SLOTS = 2
S5_ROWS = 256
S5_PARTS = 2
FFN_TM = 1024
FFN_SLAB = 256

VMEM_LIMIT = 56 * 1024 * 1024

_IN_OFFS = np.cumsum([0, 128, 128, 256, 16, 256, 256, 256, 256, 256, 256, 256, 256, 256, 4, 4, 256])
(_AQ, _AK, _AV, _ALOW, _AR, _SU, _RQ, _RK, _RV, _RG, _DQ, _DK, _DV, _DB, _DA, _DG) = [
    (int(_IN_OFFS[i]), int(_IN_OFFS[i + 1])) for i in range(16)]
Z_GLA = 0
Z_RET = 768
Z_GDN = 1792
Z_SMALL = 2816
Z_WIDTH = 2944
SM_BETA = 16
SM_A = 20


def _dot(a, b):
    return jnp.dot(a.astype(BF16), b.astype(BF16), preferred_element_type=F32)


def _dot_nt(a, b):
    return lax.dot_general(a.astype(BF16), b.astype(BF16), (((1,), (1,)), ((), ())),
                           preferred_element_type=F32)


def _split(x):
    hi = x.astype(BF16)
    return hi, (x - hi.astype(F32)).astype(BF16)


def _dot_exact_lhs(a, b):
    hi, lo = _split(b)
    n = b.shape[1]
    if n <= 128:
        r = jnp.dot(a.astype(BF16), jnp.concatenate([hi, lo], axis=1), preferred_element_type=F32)
        return r[:, :n] + r[:, n:]
    return (jnp.dot(a.astype(BF16), hi, preferred_element_type=F32)
            + jnp.dot(a.astype(BF16), lo, preferred_element_type=F32))


def _dot_exact_rhs(a, b):
    hi, lo = _split(a)
    m = a.shape[0]
    r = jnp.dot(jnp.concatenate([hi, lo], axis=0), b.astype(BF16), preferred_element_type=F32)
    return r[:m] + r[m:]


def _sigmoid(x):
    return 0.5 * jnp.tanh(0.5 * x) + 0.5


def _silu(x):
    return x * _sigmoid(x)


def _softplus(x):
    return jnp.maximum(x, 0.0) + jnp.log1p(jnp.exp(-jnp.abs(x)))


def _rms(x, g):
    return x * lax.rsqrt(jnp.mean(x * x, axis=-1, keepdims=True) + EPS) * g


def _rope_kernel(pos_ref, invf_ref, cos_ref, sin_ref):
    ang = invf_ref[...] * pos_ref[...].astype(F32)
    c = jnp.cos(ang)
    s = jnp.sin(ang)
    cos_ref[...] = jnp.concatenate([c, c, c, c], axis=0).T
    sin_ref[...] = jnp.concatenate([-s, s, -s, s], axis=0).T


def _rope_tables(positions):
    B, L = positions.shape
    inv_freq = ROPE_BASE ** (-jnp.linspace(0.0, 1.0, HEAD_DIM // 2, dtype=F32))
    spec = pl.BlockSpec((None, L, 128), lambda b: (b, 0, 0))
    return pl.pallas_call(
        _rope_kernel,
        grid=(B,),
        in_specs=[pl.BlockSpec((None, 1, L), lambda b: (b, 0, 0)),
                  pl.BlockSpec((HEAD_DIM // 2, 1), lambda b: (0, 0))],
        out_specs=[spec, spec],
        out_shape=[jax.ShapeDtypeStruct((B, L, 128), F32)] * 2,
        name="rope_tables",
    )(positions[:, None, :], inv_freq[:, None])


def _s5_kernel(*refs, n_cast):
    (h_ref, g_ref, wsu_ref, krev_ref, bbig_ref, cbig_ref, pwr_ref, pwi_ref, d8_ref) = refs[:9]
    y_ref = refs[9 + n_cast]
    cr_ref, ci_ref, su_scr, y_scr = refs[10 + 2 * n_cast:]
    _cast_ride_along(refs[9:9 + n_cast], refs[10 + n_cast:10 + 2 * n_cast])
    n_state = S5_GROUPS * S5_STATE
    W = GROUP_WIDTH
    n_slab = W // 128

    @pl.when(pl.program_id(1) == 0)
    def _():
        cr_ref[...] = jnp.zeros_like(cr_ref)
        ci_ref[...] = jnp.zeros_like(ci_ref)

    R = S5_ROWS // S5_PARTS
    tok = R * S5_SUB
    n_sub = S5_SUB
    carry = {"r": cr_ref[...], "i": ci_ref[...], "parts_done": 0}
    rows = lax.broadcasted_iota(jnp.int32, (R, n_state), 0)
    first = rows == 0

    def shifted(x, shift):
        if shift % 8 == 0:
            return jnp.concatenate([jnp.zeros((shift, x.shape[1]), F32), x[:R - shift]], axis=0)
        return jnp.where(rows >= shift, pltpu.roll(x, shift, 0), 0.0)

    def part_stages(part):
        base = part * tok
        blk = tok // n_sub
        for q in range(n_sub):
            lo = base + q * blk
            su = _dot(_rms(h_ref[lo:lo + blk, :], g_ref[...]), wsu_ref[...])
            for s in range(n_slab):
                su_scr[s, lo:lo + blk, :] = su[:, s * 128:(s + 1) * 128]
            yield
        u8 = jnp.concatenate([su_scr[s, pl.ds(base + j, R, stride=S5_SUB), :]
                              for j in range(S5_SUB) for s in range(n_slab)], axis=1)
        u8b = u8.astype(BF16)
        y_loc, inc = [], []
        n_col = 2 * n_state // n_sub
        for t in range(n_sub):
            y_loc.append(jnp.dot(u8b[:, 0:(t + 1) * W], krev_ref[(S5_SUB - 1 - t) * W:S5_SUB * W, :],
                                 preferred_element_type=F32)
                         + d8_ref[:, t * W:(t + 1) * W] * u8[:, t * W:(t + 1) * W])
            inc.append(jnp.dot(u8b, bbig_ref[:, t * n_col:(t + 1) * n_col], preferred_element_type=F32))
            yield
        inc = jnp.concatenate(inc, axis=1)
        xr, xi = inc[:, :n_state], inc[:, n_state:]
        assert carry["parts_done"] == part
        cr, ci = carry["r"], carry["i"]
        ar0, ai0 = pwr_ref[0:1, :], pwi_ref[0:1, :]
        xr = xr + jnp.where(first, ar0 * cr - ai0 * ci, 0.0)
        xi = xi + jnp.where(first, ar0 * ci + ai0 * cr, 0.0)
        shift, lvl = 1, 0
        while shift < R:
            ar, ai = pwr_ref[lvl:lvl + 1, :], pwi_ref[lvl:lvl + 1, :]
            sr, si = shifted(xr, shift), shifted(xi, shift)
            xr, xi = xr + ar * sr - ai * si, xi + ar * si + ai * sr
            shift, lvl = shift * 2, lvl + 1
            yield
        pr = jnp.where(first, cr, pltpu.roll(xr, 1, 0))
        pi = jnp.where(first, ci, pltpu.roll(xi, 1, 0))
        carry.update(r=xr[R - 1:R, :], i=xi[R - 1:R, :], parts_done=part + 1)
        xprev = jnp.concatenate([pr, pi], axis=1).astype(BF16)
        yield
        for j in range(S5_SUB):
            yj = y_loc[j] + jnp.dot(xprev, cbig_ref[:, j * W:(j + 1) * W], preferred_element_type=F32)
            for s in range(n_slab):
                y_scr[s, pl.ds(base + j, R, stride=S5_SUB), :] = yj[:, s * 128:(s + 1) * 128]
            yield
        y_ref[base:base + tok, :] = jnp.concatenate([y_scr[s, base:base + tok, :] for s in range(n_slab)],
                                                    axis=1)

    gens = [part_stages(part) for part in range(S5_PARTS)]
    for lead in range(S5_PARTS):
        for _ in range(n_sub):
            for gen in gens[:lead + 1]:
                next(gen, None)
    live = list(gens)
    while live:
        for gen in list(live):
            if next(gen, "done") == "done":
                live.remove(gen)
    cr_ref[...] = carry["r"]
    ci_ref[...] = carry["i"]


def _s5_prepare(lam_re, lam_im, log_dt, b_re, b_im, c_re, c_im, d_skip):
    G, P, H, M = S5_GROUPS, S5_STATE, S5_CH, S5_SUB
    depth = lam_re.shape[0]
    lr = jnp.minimum(lam_re.astype(F32), -1e-4)
    li = lam_im.astype(F32)
    dt = jnp.exp(log_dt.astype(F32))[:, :, None]

    def apow(t):
        tt = jnp.asarray(t, F32)[None, :, None, None]
        mag = jnp.exp((lr * dt)[:, None] * tt)
        ang = (li * dt)[:, None] * tt
        return mag * jnp.cos(ang), mag * jnp.sin(ang)

    n_lvl = int(math.log2(S5_ROWS))
    exps = list(range(M + 1)) + list(range(M - 1, -1, -1)) + [M * 2 ** s for s in range(n_lvl)]
    pw_r, pw_i = apow(np.asarray(exps, np.float32))
    rev_r, rev_i = pw_r[:, M + 1:2 * M + 1], pw_i[:, M + 1:2 * M + 1]
    ar, ai = pw_r[:, 1], pw_i[:, 1]
    nr, ni = ar - 1.0, ai
    den = lr * lr + li * li
    fr = ((nr * lr + ni * li) / den)[..., None]
    fi = ((ni * lr - nr * li) / den)[..., None]
    bbr = fr * b_re - fi * b_im
    bbi = fr * b_im + fi * b_re

    def embed(narrow, tile, row_div, col_div):
        wide = jnp.einsum('dnk,km->dnm', narrow.astype(BF16), jnp.asarray(tile, BF16),
                          preferred_element_type=F32)
        rg = (lax.broadcasted_iota(jnp.int32, wide.shape, 1) // row_div) % G
        cg = (lax.broadcasted_iota(jnp.int32, wide.shape, 2) // col_div) % G
        return jnp.where(rg == cg, wide, 0.0).astype(BF16)

    pr, pi = rev_r[:, :, :, None, :], rev_i[:, :, :, None, :]
    car = c_re[:, None] * pr - c_im[:, None] * pi
    cai = c_re[:, None] * pi + c_im[:, None] * pr
    bbr_t = bbr.transpose(0, 1, 3, 2)[:, None, :, :, None, :]
    bbi_t = bbi.transpose(0, 1, 3, 2)[:, None, :, :, None, :]
    kt = jnp.sum(car[:, :, :, None] * bbr_t - cai[:, :, :, None] * bbi_t, axis=-1)
    k_narrow = kt.reshape(depth, M * G * H, H)
    k_tile = np.tile(np.eye(H, dtype=np.float32), (1, G))
    krev = embed(k_narrow, k_tile, H, H)

    pr, pi = rev_r[..., None], rev_i[..., None]
    b_narrow = jnp.stack([pr * bbr[:, None] - pi * bbi[:, None], pr * bbi[:, None] + pi * bbr[:, None]],
                         axis=1)
    b_narrow = b_narrow.transpose(0, 2, 3, 5, 1, 4).reshape(depth, M * G * H, 2 * P)
    cp = np.arange(2 * P)
    col = np.arange(2 * G * P)
    b_tile = ((cp[:, None] // P == (col // (G * P))[None, :]) & (cp[:, None] % P == (col % P)[None, :]))
    bbig = embed(b_narrow, b_tile.astype(np.float32), H, P)

    pr, pi = pw_r[:, 1:M + 1, :, None, :], pw_i[:, 1:M + 1, :, None, :]
    qr = c_re[:, None] * pr - c_im[:, None] * pi
    qi = -(c_re[:, None] * pi + c_im[:, None] * pr)
    c_narrow = jnp.stack([qr, qi], axis=1).transpose(0, 1, 3, 5, 2, 4).reshape(depth, 2 * G * P, M * H)
    th = np.arange(M * H)
    col = np.arange(M * G * H)
    c_tile = ((th[:, None] // H == (col // (G * H))[None, :]) & (th[:, None] % H == (col % H)[None, :]))
    cbig = embed(c_narrow, c_tile.astype(np.float32), P, H)

    pad = (-n_lvl) % 8
    pwr = jnp.pad(pw_r[:, 2 * M + 1:].reshape(depth, n_lvl, G * P), ((0, 0), (0, pad), (0, 0)))
    pwi = jnp.pad(pw_i[:, 2 * M + 1:].reshape(depth, n_lvl, G * P), ((0, 0), (0, pad), (0, 0)))
    d8 = jnp.tile(d_skip.astype(F32), (1, M))[:, None, :]
    return krev, bbig, cbig, pwr, pwi, d8


def _layer_spec(arr, layer, n_grid):
    if arr.ndim == 2:
        return pl.BlockSpec(arr.shape, lambda *g: (0, 0), pipeline_mode=pl.Buffered(1))
    zeros = (0,) * (arr.ndim - 1)
    return pl.BlockSpec((None,) + arr.shape[1:], lambda *g: (layer,) + zeros, pipeline_mode=pl.Buffered(1))


def _cast_stream(w, layer, n_steps, step_of):
    _, K, N = w.shape
    nb = max(d for d in range(1, n_steps + 1) if n_steps % d == 0 and K % d == 0 and (K // d) % 16 == 0)
    rows, per = K // nb, n_steps // nb
    in_spec = pl.BlockSpec((None, rows, N), lambda *g: (layer, step_of(*g) // per, 0))
    out_spec = pl.BlockSpec((rows, N), lambda *g: (step_of(*g) // per, 0))
    return in_spec, out_spec, jax.ShapeDtypeStruct((K, N), BF16)


def _cast_ride_along(src_refs, dst_refs):
    for src, dst in zip(src_refs, dst_refs):
        dst[...] = src[...].astype(BF16)


def _s5_call(h, layer, norm_g, w_su, prep, cast_weights):
    B, L, D = h.shape
    krev, bbig, cbig, pwr, pwi, d8 = prep
    tokens = S5_SUB * S5_ROWS
    n_state = S5_GROUPS * S5_STATE
    n_slab = GROUP_WIDTH // 128
    params = [norm_g, w_su, krev, bbig, cbig, pwr, pwi, d8]
    n_i = L // tokens
    casts = [_cast_stream(w, layer, B * n_i, lambda b, i: b * n_i + i) for w in cast_weights]

    return pl.pallas_call(
        functools.partial(_s5_kernel, n_cast=len(casts)),
        grid=(B, n_i),
        in_specs=[pl.BlockSpec((None, tokens, D), lambda b, i: (b, i, 0))]
        + [_layer_spec(a, layer, 2) for a in params] + [c[0] for c in casts],
        out_specs=[pl.BlockSpec((None, tokens, GROUP_WIDTH), lambda b, i: (b, i, 0))] + [c[1] for c in casts],
        out_shape=[jax.ShapeDtypeStruct((B, L, GROUP_WIDTH), F32)] + [c[2] for c in casts],
        scratch_shapes=[pltpu.VMEM((1, n_state), F32), pltpu.VMEM((1, n_state), F32),
                        pltpu.VMEM((n_slab, tokens, 128), F32), pltpu.VMEM((n_slab, tokens, 128), F32)],
        compiler_params=pltpu.CompilerParams(dimension_semantics=("arbitrary", "arbitrary"),
                                             vmem_limit_bytes=VMEM_LIMIT),
        name="s5_mixer",
    )(h, *params, *cast_weights)


def _mixer_consts():
    TL, C, H = MIX_TL, CHUNK, NUM_HEADS
    t = np.arange(TL)
    ltri = ((t[:, None] // C == t[None, :] // C) & (t[None, :] <= t[:, None])).astype(np.float32)
    lane256 = np.arange(256)
    lane128 = np.arange(128)
    hm128 = (lane128[None, :] // GLA_DK == np.arange(H)[:, None]).astype(np.float32)
    hm256 = (lane256[None, :] // HEAD_DIM == np.arange(H)[:, None]).astype(np.float32)
    bm = (lane256[:, None] // HEAD_DIM == lane256[None, :] // HEAD_DIM).astype(np.float32)
    bmt = (lane256[:, None] // HEAD_DIM == lane128[None, :] // GLA_DK).astype(np.float32)
    c = np.arange(C)
    caus4 = (c[:, None] >= (lane256 % HEAD_DIM)[None, :]).astype(np.float32)
    bm4 = np.repeat(hm256, TL, axis=0)
    s_lane = lane128 % HEAD_DIM
    bm2 = bm[0:128, 0:128]
    incl = (c[:, None] >= s_lane[None, :]).astype(np.float32)
    strict = (c[:, None] > s_lane[None, :]).astype(np.float32)
    idiag = (c[:, None] == s_lane[None, :]).astype(np.float32)
    log_gamma = np.log1p(-(2.0 ** (-5.0 - np.arange(H, dtype=np.float32)))).astype(np.float32)
    idx = np.arange(TL, dtype=np.float32)
    rel = idx[:, None] - idx[None, :]
    dmask = np.where(rel >= 0, np.exp(np.maximum(rel, 0.0)[None] * log_gamma[:, None, None]), 0.0)
    dmask = dmask.reshape(H * TL, TL).astype(np.float32)
    lg_lane = log_gamma[lane256 // HEAD_DIM]
    xi = np.exp((idx[:, None] + 1.0) * lg_lane[None, :]).astype(np.float32)
    zeta = np.exp((TL - 1.0 - idx[:, None]) * lg_lane[None, :]).astype(np.float32)
    gchunk = np.exp(TL * lg_lane)[None, :].astype(np.float32)
    e_beta = np.zeros((128, 256), np.float32)
    e_a = np.zeros((128, 256), np.float32)
    for h in range(H):
        e_beta[SM_BETA + h, h * HEAD_DIM:(h + 1) * HEAD_DIM] = 1.0
        e_a[SM_A + h, h * HEAD_DIM:(h + 1) * HEAD_DIM] = 1.0
    return dict(ltri=ltri, hm256=hm256, bm=bm, bmt=bmt, caus4=caus4, bm4=bm4, incl=incl, strict=strict,
                idiag=idiag, bm2=bm2, dmask=dmask, xi=xi, zeta=zeta, gchunk=gchunk, e_beta=e_beta, e_a=e_a,
                m64=bm / HEAD_DIM)


_CONST_ORDER = ("ltri", "hm256", "bm", "bmt", "caus4", "bm4", "incl", "strict", "idiag", "bm2", "dmask", "xi",
                "zeta", "gchunk", "e_beta", "e_a", "m64")
_PARAM_ORDER = ("norm_mix", "w_z", "w_out", "w_a2", "b_a", "gla_g", "ret_g", "gdn_g", "conv_w", "alog_v",
                "dtb_v", "w_glu", "b_glu")


def _mixer_kernel(*refs, n_cast):
    n_c, n_p = len(_CONST_ORDER), len(_PARAM_ORDER)
    h_ref, ys5_ref, cos_ref, sin_ref = refs[:4]
    cst = dict(zip(_CONST_ORDER, refs[4:4 + n_c]))
    prm = dict(zip(_PARAM_ORDER, refs[4 + n_c:4 + n_c + n_p]))
    n_in = 4 + n_c + n_p + n_cast
    o_ref = refs[n_in]
    st_gla, s_ret, s_gdn, xpad = refs[n_in + 1 + n_cast:]
    _cast_ride_along(refs[n_in - n_cast:n_in], refs[n_in + 1:n_in + 1 + n_cast])
    TL, C, H, NS = MIX_TL, CHUNK, NUM_HEADS, MIX_SLOTS

    @pl.when(pl.program_id(1) == 0)
    def _():
        st_gla[...] = jnp.zeros_like(st_gla)
        s_ret[...] = jnp.zeros_like(s_ret)
        s_gdn[...] = jnp.zeros_like(s_gdn)
        xpad[:, 0:8, :] = jnp.zeros((NS, 8, 3 * GROUP_WIDTH), F32)

    hn = _rms(jnp.concatenate([h_ref[s] for s in range(NS)], axis=0), prm["norm_mix"][...]).astype(BF16)
    w_z = prm["w_z"]
    z_cache = {}

    def proj(slot, lo, hi):
        if (lo, hi) not in z_cache:
            z_cache[lo, hi] = jnp.dot(hn, w_z[:, lo:hi], preferred_element_type=F32)
        return z_cache[lo, hi][slot * TL:(slot + 1) * TL]

    hm256b = cst["hm256"][...].astype(BF16)
    bmb = cst["bm"][...].astype(BF16)
    bm = cst["bm"][...]
    m64 = cst["m64"][...]
    ltri = cst["ltri"][...]

    def stack4b(x):
        xb = x.astype(BF16)
        return jnp.concatenate([xb, xb, xb, xb], axis=0)

    out = {}

    def gdn_stages(slot):
        zd = proj(slot, Z_GDN, Z_GDN + 1024)
        sm = proj(slot, Z_SMALL, Z_WIDTH)
        xp = xpad.at[slot]
        xp[8:8 + TL, :] = zd[:, 0:768]
        cw = prm["conv_w"][...]
        xc = cw[GDN_CONV - 1:GDN_CONV, :] * zd[:, 0:768]
        for j in range(GDN_CONV - 1):
            xc = xc + cw[j:j + 1, :] * xp[5 + j:5 + j + TL, :]
        xp[0:8, :] = zd[TL - 8:TL, 0:768]
        xc = _silu(xc)
        dq, dk, dv = xc[:, 0:256], xc[:, 256:512], xc[:, 512:768]
        yield
        dq = dq * lax.rsqrt(_dot(dq * dq, bm) + EPS) * (HEAD_DIM ** -0.5)
        dk = dk * lax.rsqrt(_dot(dk * dk, bm) + EPS)
        beta = _dot_exact_rhs(_sigmoid(sm), cst["e_beta"][...])
        g_log = _dot_exact_rhs(-jnp.exp(prm["alog_v"][...]) * _softplus(sm + prm["dtb_v"][...]),
                               cst["e_a"][...])
        yield
        gcum = _dot_exact_lhs(ltri, g_log)
        yield
        incl = cst["incl"][...]
        strict = cst["strict"][...]
        idiag = cst["idiag"][...]
        bm2 = cst["bm2"][...]
        bm2b = bm2.astype(BF16)
        n_pair = NUM_HEADS // 2
        probs = [(c, p) for c in range(TL // C) for p in range(n_pair)]

        def blk(x, c, p):
            return x[c * C:(c + 1) * C, p * 128:(p + 1) * 128]

        def bd2(x):
            xb = x.astype(BF16)
            return jnp.concatenate([xb, xb], axis=0) * bm2b

        def bdot(a, b):
            return jnp.dot(a.astype(BF16), b, preferred_element_type=F32)

        gk_, gq_, gv_, gb_, gg_, p_, t_, aqk_ = {}, {}, {}, {}, {}, {}, {}, {}
        for cp in probs:
            k_c, q_c, g_c = blk(dk, *cp), blk(dq, *cp), blk(gcum, *cp)
            gk_[cp], gq_[cp], gv_[cp], gb_[cp], gg_[cp] = k_c, q_c, blk(dv, *cp), blk(beta, *cp), g_c
            g_row = jnp.sum(g_c * idiag, axis=0, keepdims=True)
            dec = jnp.where(incl > 0.0, jnp.exp(jnp.where(incl > 0.0, g_c - g_row, 0.0)), 0.0)
            kq = lax.dot_general(jnp.concatenate([k_c, q_c], axis=0).astype(BF16), bd2(k_c),
                                 (((1,), (1,)), ((), ())), preferred_element_type=F32)
            aqk_[cp] = kq[C:2 * C] * dec
            p_[cp] = -(kq[0:C] * dec * strict * gb_[cp])
            t_[cp] = idiag + p_[cp]
        yield
        for cp in probs:
            p_[cp] = bdot(p_[cp], bd2(p_[cp]))
        yield
        for _ in range(int(math.log2(C)) - 2):
            for cp in probs:
                pt = bdot(jnp.concatenate([p_[cp], t_[cp]], axis=0), bd2(p_[cp]))
                p_[cp] = pt[0:C]
                t_[cp] = t_[cp] + pt[C:2 * C]
            yield
        for cp in probs:
            t_[cp] = t_[cp] + bdot(t_[cp], bd2(p_[cp]))
        yield
        uw_, eg_ = {}, {}
        for cp in probs:
            eg_[cp] = jnp.exp(gg_[cp])
            b_c = gb_[cp]
            uw_[cp] = bdot(t_[cp], jnp.concatenate([bd2(gv_[cp] * b_c), bd2(gk_[cp] * b_c * eg_[cp])],
                                                   axis=1))
        yield
        mm_, nn_, dl_, qp_, op_ = {}, {}, {}, {}, {}
        for cp in probs:
            k_c, g_c = gk_[cp], gg_[cp]
            g_last = g_c[C - 1:C, :]
            u_c, w_c = uw_[cp][:, 0:128], uw_[cp][:, 128:256]
            k_end_t = (k_c * jnp.exp(g_last - g_c)).T
            mn = bdot(k_end_t, jnp.concatenate([w_c, u_c], axis=1).astype(BF16))
            mm_[cp] = bm2 * mn[:, 0:128]
            nn_[cp] = bm2 * mn[:, 128:256]
            dl_[cp] = jnp.exp(g_last)
            qo = bdot(aqk_[cp], jnp.concatenate([bd2(w_c), bd2(u_c)], axis=1))
            qp_[cp] = gq_[cp] * eg_[cp] - qo[:, 0:128]
            op_[cp] = qo[:, 128:256]
        yield
        o_rows = []
        sgp = [s_gdn[slot, p] for p in range(n_pair)]
        for c in range(TL // C):
            o_pair = []
            for p in range(n_pair):
                cp = (c, p)
                s_b = sgp[p].astype(BF16)
                o_pair.append(bdot(qp_[cp], s_b) + op_[cp])
                sgp[p] = sgp[p] * dl_[cp] - bdot(mm_[cp], s_b) + nn_[cp]
            o_rows.append(jnp.concatenate(o_pair, axis=1))
            yield
        for p in range(n_pair):
            s_gdn[slot, p] = sgp[p]
        o_d = jnp.concatenate(o_rows, axis=0)
        out[slot, "d"] = o_d * lax.rsqrt(_dot(o_d * o_d, m64) + EPS) * prm["gdn_g"][...] * _silu(zd[:, 768:1024])

    def gla_stages(slot):
        zg = proj(slot, Z_GLA, Z_GLA + 768)
        sm = proj(slot, Z_SMALL, Z_WIDTH)
        q, k, v, r = zg[:, 0:128], zg[:, 128:256], zg[:, 256:512], zg[:, 512:768]
        gk = _dot(sm, prm["w_a2"][...]) + prm["b_a"][...]
        gk = (jnp.minimum(gk, 0.0) - jnp.log1p(jnp.exp(-jnp.abs(gk)))) / GLA_GATE_NORM
        yield
        bcum = _dot_exact_lhs(ltri, gk)
        caus4 = cst["caus4"][...]
        bmt = cst["bmt"][...]
        bmtb = bmt.astype(BF16)
        st = st_gla[slot]
        yield
        o_chunks = []
        for c in range(TL // C):
            sl = slice(c * C, (c + 1) * C)
            b_c = bcum[sl]
            b_last = b_c[C - 1:C, :]
            q_t = q[sl] * jnp.exp(b_c) * (GLA_DK ** -0.5)
            k_t = k[sl] * jnp.exp(-b_c)
            k_end = k[sl] * jnp.exp(b_last - b_c)
            v_c = v[sl]
            q_b = q_t.astype(BF16)
            att = lax.dot_general(q_b, stack4b(k_t) * bmtb, (((1,), (1,)), ((), ())),
                                  preferred_element_type=F32) * caus4
            cross = lax.dot_general(q_b, st.astype(BF16), (((1,), (1,)), ((), ())),
                                    preferred_element_type=F32)
            st = st * jnp.exp(b_last) + bmt * _dot(v_c.T, k_end)
            yield
            o_chunks.append(jnp.dot(att.astype(BF16), stack4b(v_c) * bmb, preferred_element_type=F32)
                            + cross)
            yield
        st_gla[slot] = st
        o_a = jnp.concatenate(o_chunks, axis=0)
        out[slot, "a"] =o_a * lax.rsqrt(_dot(o_a * o_a, m64) + EPS) * prm["gla_g"][...] * _silu(r)

    def ret_stages(slot):
        zr = proj(slot, Z_RET, Z_RET + 1024)
        cos2 = jnp.concatenate([cos_ref[slot], cos_ref[slot]], axis=1)
        sin2 = jnp.concatenate([sin_ref[slot], sin_ref[slot]], axis=1)
        first_half = (lax.broadcasted_iota(jnp.int32, (TL, 256), 1) % HEAD_DIM) < HEAD_DIM // 2

        def rope(x):
            rot = jnp.where(first_half, pltpu.roll(x, 256 - HEAD_DIM // 2, 1),
                            pltpu.roll(x, HEAD_DIM // 2, 1))
            return x * cos2 + rot * sin2

        rq = rope(zr[:, 0:256])
        rk = rope(zr[:, 256:512]) * (HEAD_DIM ** -0.5)
        rv = zr[:, 512:768]
        rg = zr[:, 768:1024]
        yield
        rq_b = rq.astype(BF16)
        qs = jnp.concatenate([rq_b * hm256b[hh:hh + 1] for hh in range(H)], axis=0)
        sc = lax.dot_general(qs, rk.astype(BF16), (((1,), (1,)), ((), ())),
                             preferred_element_type=F32) * cst["dmask"][...]
        s_prev = s_ret[slot]
        cross = jnp.dot(rq_b, s_prev.astype(BF16), preferred_element_type=F32) * cst["xi"][...]
        s_ret[slot] = s_prev * cst["gchunk"][...] + bm * _dot((rk * cst["zeta"][...]).T, rv)
        yield
        sc_wide = jnp.concatenate([sc[hh * TL:(hh + 1) * TL] for hh in range(H)], axis=1).astype(BF16)
        o_c = jnp.dot(sc_wide, stack4b(rv) * cst["bm4"][...].astype(BF16), preferred_element_type=F32) + cross
        yield
        o_c = o_c - _dot_exact_rhs(o_c, m64)
        yield
        out[slot, "c"] = o_c * lax.rsqrt(_dot(o_c * o_c, m64) + EPS) * prm["ret_g"][...] * _silu(rg)

    def s5_stages(slot):
        y = ys5_ref[slot]
        y = 0.5 * y * (1.0 + jnp.tanh(math.sqrt(2.0 / math.pi) * (y + 0.044715 * (y * y * y))))
        yield
        out[slot, "b"] = y * _sigmoid(_dot(y, prm["w_glu"][...]) + prm["b_glu"][...])

    running = [stages(slot) for stages in (gdn_stages, gla_stages, ret_stages, s5_stages) for slot in range(NS)]
    while running:
        for gen in list(running):
            if next(gen, "done") == "done":
                running.remove(gen)

    mix = jnp.concatenate([jnp.concatenate([out[s, "a"], out[s, "b"], out[s, "c"], out[s, "d"]], axis=1)
                           for s in range(NS)], axis=0).astype(BF16)
    res = jnp.dot(mix, prm["w_out"][...], preferred_element_type=F32)
    for s in range(NS):
        o_ref[s] = h_ref[s] + res[s * TL:(s + 1) * TL]


def _mixer_call(h, ys5, cos_t, sin_t, consts, params, layer, cast_weights):
    B, L, D = h.shape
    TL, NS = MIX_TL, MIX_SLOTS
    n_i = L // TL
    casts = [_cast_stream(w, layer, (B // NS) * n_i, lambda b, i: b * n_i + i) for w in cast_weights]

    def const(arr):
        nd = arr.ndim
        return pl.BlockSpec(arr.shape, lambda b, i: (0,) * nd, pipeline_mode=pl.Buffered(1))

    def tile(width):
        return pl.BlockSpec((NS, TL, width), lambda b, i: (b, i, 0))

    c_list = [consts[n] for n in _CONST_ORDER]
    p_list = [params[n] for n in _PARAM_ORDER]
    return pl.pallas_call(
        functools.partial(_mixer_kernel, n_cast=len(casts)),
        grid=(B // NS, n_i),
        in_specs=[tile(D), tile(GROUP_WIDTH), tile(128), tile(128)] + [const(a) for a in c_list]
        + [_layer_spec(a, layer, 2) for a in p_list] + [c[0] for c in casts],
        out_specs=[tile(D)] + [c[1] for c in casts],
        out_shape=[jax.ShapeDtypeStruct((B, L, D), F32)] + [c[2] for c in casts],
        scratch_shapes=[pltpu.VMEM((NS, 256, 128), F32), pltpu.VMEM((NS, 256, 256), F32),
                        pltpu.VMEM((NS, 2, 128, 128), F32), pltpu.VMEM((NS, TL + 8, 3 * GROUP_WIDTH), F32)],
        compiler_params=pltpu.CompilerParams(dimension_semantics=("arbitrary", "arbitrary"),
                                             vmem_limit_bytes=VMEM_LIMIT),
        name="mixer",
    )(h, ys5, cos_t, sin_t, *c_list, *p_list, *cast_weights)


def _ffn_kernel(h_ref, p_ref, gf_ref, wup_ref, wdn_ref, gp_ref, wg_ref, wp_ref, gl_ref, o_ref, act_ref, *,
                final):
    h = h_ref[...]
    hn = _rms(h, gf_ref[...]).astype(BF16)
    for s in range(FFN_HIDDEN // FFN_SLAB):
        lo = s * FFN_SLAB
        g = jnp.dot(hn, wup_ref[:, lo:lo + FFN_SLAB], preferred_element_type=F32)
        u = jnp.dot(hn, wup_ref[:, FFN_HIDDEN + lo:FFN_HIDDEN + lo + FFN_SLAB], preferred_element_type=F32)
        act_ref[:, lo:lo + FFN_SLAB] = (_silu(g) * u).astype(BF16)
    h2 = h + jnp.dot(act_ref[...], wdn_ref[...], preferred_element_type=F32)
    gate = _sigmoid(jnp.dot(_rms(h2, gp_ref[...]).astype(BF16), wg_ref[...], preferred_element_type=F32))
    h3 = h2 + jnp.dot(p_ref[...].astype(BF16), wp_ref[...], preferred_element_type=F32) * gate
    if final:
        h3 = _rms(h3, gl_ref[...])
    o_ref[...] = h3


def _ffn_call(h2d, p3d, layer, weights, final):
    T, D = h2d.shape
    TM = FFN_TM
    return pl.pallas_call(
        functools.partial(_ffn_kernel, final=final),
        grid=(T // TM,),
        in_specs=[pl.BlockSpec((TM, D), lambda i: (i, 0)),
                  pl.BlockSpec((None, TM, PLE_DIM), lambda i: (layer, i, 0))]
        + [_layer_spec(w, layer, 1) for w in weights],
        out_specs=pl.BlockSpec((TM, D), lambda i: (i, 0)),
        out_shape=jax.ShapeDtypeStruct((T, D), F32),
        scratch_shapes=[pltpu.VMEM((TM, FFN_HIDDEN), BF16)],
        compiler_params=pltpu.CompilerParams(dimension_semantics=("arbitrary",), vmem_limit_bytes=VMEM_LIMIT),
        name="ffn_ple",
    )(h2d, p3d, *weights)


def _row(v):
    return v[:, None, :]


_WZ_SEGMENTS = (_AQ, _AK, _AV, _AR, _RQ, _RK, _RV, _RG, _DQ, _DK, _DV, _DG)


def _regroup_kernel(wt_ref, wz_ref, wsu_ref):
    col = 0
    for lo, hi in _WZ_SEGMENTS:
        wz_ref[:, col:col + hi - lo] = wt_ref[lo:hi, :].T.astype(BF16)
        col += hi - lo
    assert col == Z_SMALL and _DA[0] == _DB[1] and (_DA[1] - _DB[0]) % 8 == 0
    small = jnp.concatenate([wt_ref[_ALOW[0]:_ALOW[1], :], wt_ref[_DB[0]:_DA[1], :],
                             jnp.zeros((128 - GLA_RANK - 2 * NUM_HEADS, D_MODEL), F32)], axis=0)
    wz_ref[:, Z_SMALL:Z_WIDTH] = small.T.astype(BF16)
    wsu_ref[...] = wt_ref[_SU[0]:_SU[1], :].T.astype(BF16)


def _regroup_w_in(w_in):
    depth, K, N = w_in.shape
    return pl.pallas_call(
        _regroup_kernel,
        grid=(depth,),
        in_specs=[pl.BlockSpec((None, N, K), lambda d: (d, 0, 0))],
        out_specs=[pl.BlockSpec((None, K, Z_WIDTH), lambda d: (d, 0, 0)),
                   pl.BlockSpec((None, K, GROUP_WIDTH), lambda d: (d, 0, 0))],
        out_shape=[jax.ShapeDtypeStruct((depth, K, Z_WIDTH), BF16),
                   jax.ShapeDtypeStruct((depth, K, GROUP_WIDTH), BF16)],
        compiler_params=pltpu.CompilerParams(vmem_limit_bytes=VMEM_LIMIT),
        name="regroup_w_in",
    )(jnp.swapaxes(w_in, 1, 2))


def _mixer_params(norm_mix, w_z, gla_w_a2, gla_b_a, gla_norm, ret_norm, gdn_conv, gdn_a_log,
                  gdn_dt_bias, gdn_norm, s5_b_glu):
    w_a2 = jnp.pad(gla_w_a2, ((0, 0), (0, 128 - GLA_RANK), (0, 0)))
    lane_pad = ((0, 0), (SM_A, 128 - SM_A - NUM_HEADS))
    return dict(norm_mix=_row(norm_mix), w_z=w_z, w_a2=w_a2, b_a=_row(gla_b_a),
                gla_g=_row(jnp.tile(gla_norm, (1, NUM_HEADS))), ret_g=_row(ret_norm),
                gdn_g=_row(jnp.tile(gdn_norm, (1, NUM_HEADS))), conv_w=gdn_conv,
                alog_v=_row(jnp.pad(gdn_a_log, lane_pad)), dtb_v=_row(jnp.pad(gdn_dt_bias, lane_pad)),
                b_glu=_row(s5_b_glu))


def kernel(x, p, positions, norm_mix, w_in, w_out, gla_w_a2, gla_b_a, gla_norm, s5_lam_re, s5_lam_im, s5_log_dt,
           s5_b_re, s5_b_im, s5_c_re, s5_c_im, s5_d, s5_w_glu, s5_b_glu, ret_norm, gdn_conv, gdn_a_log,
           gdn_dt_bias, gdn_norm, norm_ffn, w_ffn_up, w_ffn_down, norm_ple, w_ple_gate, w_ple_proj, norm_final):
    B, L, D = x.shape
    depth = w_in.shape[0]
    assert D == D_MODEL and L % max(MIX_TL, S5_SUB * S5_ROWS) == 0 and (B * L) % FFN_TM == 0
    assert B % MIX_SLOTS == 0
    consts = {k: jnp.asarray(v) for k, v in _mixer_consts().items()}
    cos_t, sin_t = _rope_tables(positions)
    prep = _s5_prepare(s5_lam_re, s5_lam_im, s5_log_dt, s5_b_re, s5_b_im, s5_c_re, s5_c_im, s5_d)
    w_z, w_su = _regroup_w_in(w_in)
    params = _mixer_params(norm_mix, w_z, gla_w_a2, gla_b_a, gla_norm, ret_norm, gdn_conv,
                           gdn_a_log, gdn_dt_bias, gdn_norm, s5_b_glu)
    g_ffn, g_ple = _row(norm_ffn), _row(norm_ple)
    g_final = jnp.broadcast_to(norm_final[None, None, :], (depth, 1, D))
    p3d = p.reshape(depth, B * L, PLE_DIM)
    h = x
    for i in range(depth):
        ys5, w_out_b, w_glu_b = _s5_call(h, i, _row(norm_mix), w_su, prep, [w_out, s5_w_glu])
        h, w_up_b, w_down_b, w_gate_b, w_proj_b = _mixer_call(
            h, ys5, cos_t, sin_t, consts, dict(params, w_out=w_out_b, w_glu=w_glu_b), i,
            [w_ffn_up, w_ffn_down, w_ple_gate, w_ple_proj])
        ffn_w = [g_ffn, w_up_b, w_down_b, g_ple, w_gate_b, w_proj_b, g_final]
        h = _ffn_call(h.reshape(B * L, D), p3d, i, ffn_w, final=(i == depth - 1)).reshape(B, L, D)
    return h
```

```python
import functools
import math

import numpy as np
import jax
import jax.numpy as jnp
from jax import lax
from jax.experimental import pallas as pl
from jax.experimental.pallas import tpu as pltpu

F32 = jnp.float32
BF16 = jnp.bfloat16

D_MODEL = 1024
PLE_DIM = 256
GROUP_WIDTH = 256
NUM_HEADS = 4
CHUNK = 64
EPS = 1e-6

GLA_DK = 32
GLA_RANK = 16
GLA_GATE_NORM = 16.0
S5_CH = 16
S5_GROUPS = 16
S5_STATE = 64
S5_SUB = 8
HEAD_DIM = 64
ROPE_BASE = 10000.0
GDN_CONV = 4
FFN_HIDDEN = 2816

MIX_TL = 256
MIX_SLOTS = 2
S5_ROWS = 256
S5_PARTS = 2
FFN_TM = 1024
FFN_SLAB = 256

VMEM_LIMIT = 56 * 1024 * 1024

_IN_OFFS = np.cumsum([0, 128, 128, 256, 16, 256, 256, 256, 256, 256, 256, 256, 256, 256, 4, 4, 256])
(_AQ, _AK, _AV, _ALOW, _AR, _SU, _RQ, _RK, _RV, _RG, _DQ, _DK, _DV, _DB, _DA, _DG) = [
    (int(_IN_OFFS[i]), int(_IN_OFFS[i + 1])) for i in range(16)]
Z_GLA = 0
Z_RET = 768
Z_GDN = 1792
Z_SMALL = 2816
Z_WIDTH = 2944
SM_BETA = 16
SM_A = 20


def _dot(a, b):
    return jnp.dot(a.astype(BF16), b.astype(BF16), preferred_element_type=F32)


def _split(x):
    hi = x.astype(BF16)
    return hi, (x - hi.astype(F32)).astype(BF16)


def _dot_exact_lhs(a, b):
    hi, lo = _split(b)
    n = b.shape[1]
    if n <= 128:
        r = jnp.dot(a.astype(BF16), jnp.concatenate([hi, lo], axis=1), preferred_element_type=F32)
        return r[:, :n] + r[:, n:]
    return (jnp.dot(a.astype(BF16), hi, preferred_element_type=F32)
            + jnp.dot(a.astype(BF16), lo, preferred_element_type=F32))


def _dot_exact_rhs(a, b):
    hi, lo = _split(a)
    m = a.shape[0]
    r = jnp.dot(jnp.concatenate([hi, lo], axis=0), b.astype(BF16), preferred_element_type=F32)
    return r[:m] + r[m:]


def _sigmoid(x):
    return 0.5 * jnp.tanh(0.5 * x) + 0.5


def _silu(x):
    return x * _sigmoid(x)


def _softplus(x):
    return jnp.maximum(x, 0.0) + jnp.log1p(jnp.exp(-jnp.abs(x)))


def _rms(x, g):
    return x * lax.rsqrt(jnp.mean(x * x, axis=-1, keepdims=True) + EPS) * g


def _rope_kernel(pos_ref, invf_ref, cos_ref, sin_ref):
    ang = invf_ref[...] * pos_ref[...].astype(F32)
    c = jnp.cos(ang)
    s = jnp.sin(ang)
    cos_ref[...] = jnp.concatenate([c, c, c, c], axis=0).T
    sin_ref[...] = jnp.concatenate([-s, s, -s, s], axis=0).T


def _rope_tables(positions):
    B, L = positions.shape
    inv_freq = ROPE_BASE ** (-jnp.linspace(0.0, 1.0, HEAD_DIM // 2, dtype=F32))
    spec = pl.BlockSpec((None, L, 128), lambda b: (b, 0, 0))
    return pl.pallas_call(
        _rope_kernel,
        grid=(B,),
        in_specs=[pl.BlockSpec((None, 1, L), lambda b: (b, 0, 0)),
                  pl.BlockSpec((HEAD_DIM // 2, 1), lambda b: (0, 0))],
        out_specs=[spec, spec],
        out_shape=[jax.ShapeDtypeStruct((B, L, 128), F32)] * 2,
        name="rope_tables",
    )(positions[:, None, :], inv_freq[:, None])


def _s5_kernel(*refs, n_cast):
    (h_ref, g_ref, wsu_ref, krev_ref, bbig_ref, cbig_ref, pwr_ref, pwi_ref, d8_ref) = refs[:9]
    y_ref = refs[9 + n_cast]
    cr_ref, ci_ref, su_scr, y_scr = refs[10 + 2 * n_cast:]
    n_state = S5_GROUPS * S5_STATE
    W = GROUP_WIDTH
    n_slab = W // 128

    @pl.when(pl.program_id(1) == 0)
    def _():
        cr_ref[...] = jnp.zeros_like(cr_ref)
        ci_ref[...] = jnp.zeros_like(ci_ref)

    R = S5_ROWS // S5_PARTS
    tok = R * S5_SUB
    n_sub = S5_SUB
    carry = {"r": cr_ref[...], "i": ci_ref[...], "parts_done": 0}
    rows = lax.broadcasted_iota(jnp.int32, (R, n_state), 0)
    first = rows == 0

    def shifted(x, shift):
        if shift % 8 == 0:
            return jnp.concatenate([jnp.zeros((shift, x.shape[1]), F32), x[:R - shift]], axis=0)
        return jnp.where(rows >= shift, pltpu.roll(x, shift, 0), 0.0)

    def part_stages(part):
        base = part * tok
        n_proj = 2
        blk = tok // n_proj
        for q in range(n_sub):
            if q % (n_sub // n_proj) == 0:
                lo = base + (q // (n_sub // n_proj)) * blk
                su = _dot(_rms(h_ref[lo:lo + blk, :], g_ref[...]), wsu_ref[...])
                for s in range(n_slab):
                    su_scr[s, lo:lo + blk, :] = su[:, s * 128:(s + 1) * 128]
            yield
        u8 = jnp.concatenate([su_scr[s, pl.ds(base + j, R, stride=S5_SUB), :]
                              for j in range(S5_SUB) for s in range(n_slab)], axis=1)
        u8b = u8.astype(BF16)
        y_loc, inc = [], []
        n_col = 2 * n_state // n_sub
        for t in range(n_sub):
            y_loc.append(jnp.dot(u8b[:, 0:(t + 1) * W], krev_ref[(S5_SUB - 1 - t) * W:S5_SUB * W, :],
                                 preferred_element_type=F32)
                         + d8_ref[:, t * W:(t + 1) * W] * u8[:, t * W:(t + 1) * W])
            inc.append(jnp.dot(u8b, bbig_ref[:, t * n_col:(t + 1) * n_col], preferred_element_type=F32))
            yield
        inc = jnp.concatenate(inc, axis=1)
        xr, xi = inc[:, :n_state], inc[:, n_state:]
        assert carry["parts_done"] == part
        cr, ci = carry["r"], carry["i"]
        ar0, ai0 = pwr_ref[0:1, :], pwi_ref[0:1, :]
        xr = xr + jnp.where(first, ar0 * cr - ai0 * ci, 0.0)
        xi = xi + jnp.where(first, ar0 * ci + ai0 * cr, 0.0)
        shift, lvl = 1, 0
        while shift < R:
            ar, ai = pwr_ref[lvl:lvl + 1, :], pwi_ref[lvl:lvl + 1, :]
            sr, si = shifted(xr, shift), shifted(xi, shift)
            xr, xi = xr + ar * sr - ai * si, xi + ar * si + ai * sr
            shift, lvl = shift * 2, lvl + 1
            yield
        pr = jnp.where(first, cr, pltpu.roll(xr, 1, 0))
        pi = jnp.where(first, ci, pltpu.roll(xi, 1, 0))
        carry.update(r=xr[R - 1:R, :], i=xi[R - 1:R, :], parts_done=part + 1)
        xprev = jnp.concatenate([pr, pi], axis=1).astype(BF16)
        yield
        for j in range(S5_SUB):
            yj = y_loc[j] + jnp.dot(xprev, cbig_ref[:, j * W:(j + 1) * W], preferred_element_type=F32)
            for s in range(n_slab):
                y_scr[s, pl.ds(base + j, R, stride=S5_SUB), :] = yj[:, s * 128:(s + 1) * 128]
            yield
        y_ref[base:base + tok, :] = jnp.concatenate([y_scr[s, base:base + tok, :] for s in range(n_slab)],
                                                    axis=1)

    gens = [part_stages(part) for part in range(S5_PARTS)]
    for lead in range(S5_PARTS):
        for _ in range(n_sub):
            for gen in gens[:lead + 1]:
                next(gen, None)
    live = list(gens)
    while live:
        for gen in list(live):
            if next(gen, "done") == "done":
                live.remove(gen)
    cr_ref[...] = carry["r"]
    ci_ref[...] = carry["i"]
    _cast_ride_along(refs[9:9 + n_cast], refs[10 + n_cast:10 + 2 * n_cast])


def _s5_prepare(lam_re, lam_im, log_dt, b_re, b_im, c_re, c_im, d_skip):
    G, P, H, M = S5_GROUPS, S5_STATE, S5_CH, S5_SUB
    depth = lam_re.shape[0]
    lr = jnp.minimum(lam_re.astype(F32), -1e-4)
    li = lam_im.astype(F32)
    dt = jnp.exp(log_dt.astype(F32))[:, :, None]

    def apow(t):
        tt = jnp.asarray(t, F32)[None, :, None, None]
        mag = jnp.exp((lr * dt)[:, None] * tt)
        ang = (li * dt)[:, None] * tt
        return mag * jnp.cos(ang), mag * jnp.sin(ang)

    n_lvl = int(math.log2(S5_ROWS))
    exps = list(range(M + 1)) + list(range(M - 1, -1, -1)) + [M * 2 ** s for s in range(n_lvl)]
    pw_r, pw_i = apow(np.asarray(exps, np.float32))
    rev_r, rev_i = pw_r[:, M + 1:2 * M + 1], pw_i[:, M + 1:2 * M + 1]
    ar, ai = pw_r[:, 1], pw_i[:, 1]
    nr, ni = ar - 1.0, ai
    den = lr * lr + li * li
    fr = ((nr * lr + ni * li) / den)[..., None]
    fi = ((ni * lr - nr * li) / den)[..., None]
    bbr = fr * b_re - fi * b_im
    bbi = fr * b_im + fi * b_re

    def embed(narrow, tile, row_div, col_div):
        wide = jnp.einsum('dnk,km->dnm', narrow.astype(BF16), jnp.asarray(tile, BF16),
                          preferred_element_type=F32)
        rg = (lax.broadcasted_iota(jnp.int32, wide.shape, 1) // row_div) % G
        cg = (lax.broadcasted_iota(jnp.int32, wide.shape, 2) // col_div) % G
        return jnp.where(rg == cg, wide, 0.0).astype(BF16)

    pr, pi = rev_r[:, :, :, None, :], rev_i[:, :, :, None, :]
    car = c_re[:, None] * pr - c_im[:, None] * pi
    cai = c_re[:, None] * pi + c_im[:, None] * pr
    bbr_t = bbr.transpose(0, 1, 3, 2)[:, None, :, :, None, :]
    bbi_t = bbi.transpose(0, 1, 3, 2)[:, None, :, :, None, :]
    kt = jnp.sum(car[:, :, :, None] * bbr_t - cai[:, :, :, None] * bbi_t, axis=-1)
    k_narrow = kt.reshape(depth, M * G * H, H)
    k_tile = np.tile(np.eye(H, dtype=np.float32), (1, G))
    krev = embed(k_narrow, k_tile, H, H)

    pr, pi = rev_r[..., None], rev_i[..., None]
    b_narrow = jnp.stack([pr * bbr[:, None] - pi * bbi[:, None], pr * bbi[:, None] + pi * bbr[:, None]],
                         axis=1)
    b_narrow = b_narrow.transpose(0, 2, 3, 5, 1, 4).reshape(depth, M * G * H, 2 * P)
    cp = np.arange(2 * P)
    col = np.arange(2 * G * P)
    b_tile = ((cp[:, None] // P == (col // (G * P))[None, :]) & (cp[:, None] % P == (col % P)[None, :]))
    bbig = embed(b_narrow, b_tile.astype(np.float32), H, P)

    pr, pi = pw_r[:, 1:M + 1, :, None, :], pw_i[:, 1:M + 1, :, None, :]
    qr = c_re[:, None] * pr - c_im[:, None] * pi
    qi = -(c_re[:, None] * pi + c_im[:, None] * pr)
    c_narrow = jnp.stack([qr, qi], axis=1).transpose(0, 1, 3, 5, 2, 4).reshape(depth, 2 * G * P, M * H)
    th = np.arange(M * H)
    col = np.arange(M * G * H)
    c_tile = ((th[:, None] // H == (col // (G * H))[None, :]) & (th[:, None] % H == (col % H)[None, :]))
    cbig = embed(c_narrow, c_tile.astype(np.float32), P, H)

    pad = (-n_lvl) % 8
    pwr = jnp.pad(pw_r[:, 2 * M + 1:].reshape(depth, n_lvl, G * P), ((0, 0), (0, pad), (0, 0)))
    pwi = jnp.pad(pw_i[:, 2 * M + 1:].reshape(depth, n_lvl, G * P), ((0, 0), (0, pad), (0, 0)))
    d8 = jnp.tile(d_skip.astype(F32), (1, M))[:, None, :]
    return krev, bbig, cbig, pwr, pwi, d8


def _layer_spec(arr, layer, n_grid):
    if arr.ndim == 2:
        return pl.BlockSpec(arr.shape, lambda *g: (0, 0), pipeline_mode=pl.Buffered(1))
    zeros = (0,) * (arr.ndim - 1)
    return pl.BlockSpec((None,) + arr.shape[1:], lambda *g: (layer,) + zeros, pipeline_mode=pl.Buffered(1))


def _cast_stream(w, layer, n_steps, step_of):
    _, K, N = w.shape
    nb = max(d for d in range(1, n_steps + 1) if n_steps % d == 0 and K % d == 0 and (K // d) % 16 == 0)
    rows, per = K // nb, n_steps // nb
    in_spec = pl.BlockSpec((None, rows, N), lambda *g: (layer, step_of(*g) // per, 0))
    out_spec = pl.BlockSpec((rows, N), lambda *g: (step_of(*g) // per, 0))
    return in_spec, out_spec, jax.ShapeDtypeStruct((K, N), BF16)


def _cast_ride_along(src_refs, dst_refs):
    for src, dst in zip(src_refs, dst_refs):
        dst[...] = src[...].astype(BF16)


def _s5_call(h, layer, norm_g, w_su, prep, cast_weights):
    B, L, D = h.shape
    krev, bbig, cbig, pwr, pwi, d8 = prep
    tokens = S5_SUB * S5_ROWS
    n_state = S5_GROUPS * S5_STATE
    n_slab = GROUP_WIDTH // 128
    params = [norm_g, w_su, krev, bbig, cbig, pwr, pwi, d8]
    n_i = L // tokens
    casts = [_cast_stream(w, layer, B * n_i, lambda b, i: b * n_i + i) for w in cast_weights]

    return pl.pallas_call(
        functools.partial(_s5_kernel, n_cast=len(casts)),
        grid=(B, n_i),
        in_specs=[pl.BlockSpec((None, tokens, D), lambda b, i: (b, i, 0))]
        + [_layer_spec(a, layer, 2) for a in params] + [c[0] for c in casts],
        out_specs=[pl.BlockSpec((None, tokens, GROUP_WIDTH), lambda b, i: (b, i, 0))] + [c[1] for c in casts],
        out_shape=[jax.ShapeDtypeStruct((B, L, GROUP_WIDTH), F32)] + [c[2] for c in casts],
        scratch_shapes=[pltpu.VMEM((1, n_state), F32), pltpu.VMEM((1, n_state), F32),
                        pltpu.VMEM((n_slab, tokens, 128), F32), pltpu.VMEM((n_slab, tokens, 128), F32)],
        compiler_params=pltpu.CompilerParams(dimension_semantics=("arbitrary", "arbitrary"),
                                             vmem_limit_bytes=VMEM_LIMIT),
        name="s5_mixer",
    )(h, *params, *cast_weights)


def _mixer_consts():
    TL, C, H = MIX_TL, CHUNK, NUM_HEADS
    t = np.arange(TL)
    ltri = ((t[:, None] // C == t[None, :] // C) & (t[None, :] <= t[:, None])).astype(np.float32)
    lane256 = np.arange(256)
    lane128 = np.arange(128)
    hm256 = (lane256[None, :] // HEAD_DIM == np.arange(H)[:, None]).astype(np.float32)
    bm = (lane256[:, None] // HEAD_DIM == lane256[None, :] // HEAD_DIM).astype(np.float32)
    bmt = (lane256[:, None] // HEAD_DIM == lane128[None, :] // GLA_DK).astype(np.float32)
    c = np.arange(C)
    caus4 = (c[:, None] >= (lane256 % HEAD_DIM)[None, :]).astype(np.float32)
    bm4 = np.repeat(hm256, TL, axis=0)
    s_lane = lane128 % HEAD_DIM
    bm2 = bm[0:128, 0:128]
    incl = (c[:, None] >= s_lane[None, :]).astype(np.float32)
    strict = (c[:, None] > s_lane[None, :]).astype(np.float32)
    idiag = (c[:, None] == s_lane[None, :]).astype(np.float32)
    log_gamma = np.log1p(-(2.0 ** (-5.0 - np.arange(H, dtype=np.float32)))).astype(np.float32)
    idx = np.arange(TL, dtype=np.float32)
    rel = idx[:, None] - idx[None, :]
    dmask = np.where(rel >= 0, np.exp(np.maximum(rel, 0.0)[None] * log_gamma[:, None, None]), 0.0)
    dmask = dmask.reshape(H * TL, TL).astype(np.float32)
    lg_lane = log_gamma[lane256 // HEAD_DIM]
    xi = np.exp((idx[:, None] + 1.0) * lg_lane[None, :]).astype(np.float32)
    zeta = np.exp((TL - 1.0 - idx[:, None]) * lg_lane[None, :]).astype(np.float32)
    gchunk = np.exp(TL * lg_lane)[None, :].astype(np.float32)
    e_beta = np.zeros((128, 256), np.float32)
    e_a = np.zeros((128, 256), np.float32)
    for h in range(H):
        e_beta[SM_BETA + h, h * HEAD_DIM:(h + 1) * HEAD_DIM] = 1.0
        e_a[SM_A + h, h * HEAD_DIM:(h + 1) * HEAD_DIM] = 1.0
    return dict(ltri=ltri, hm256=hm256, bm=bm, bmt=bmt, caus4=caus4, bm4=bm4, incl=incl, strict=strict,
                idiag=idiag, bm2=bm2, dmask=dmask, xi=xi, zeta=zeta, gchunk=gchunk, e_beta=e_beta, e_a=e_a,
                m64=bm / HEAD_DIM)


_CONST_ORDER = ("ltri", "hm256", "bm", "bmt", "caus4", "bm4", "incl", "strict", "idiag", "bm2", "dmask", "xi",
                "zeta", "gchunk", "e_beta", "e_a", "m64")
_PARAM_ORDER = ("norm_mix", "w_z", "w_out", "w_a2", "b_a", "gla_g", "ret_g", "gdn_g", "conv_w", "alog_v",
                "dtb_v", "w_glu", "b_glu")


def _mixer_kernel(*refs, n_cast):
    n_c, n_p = len(_CONST_ORDER), len(_PARAM_ORDER)
    h_ref, ys5_ref, cos_ref, sin_ref = refs[:4]
    cst = dict(zip(_CONST_ORDER, refs[4:4 + n_c]))
    prm = dict(zip(_PARAM_ORDER, refs[4 + n_c:4 + n_c + n_p]))
    n_in = 4 + n_c + n_p + n_cast
    o_ref = refs[n_in]
    st_gla, s_ret, s_gdn, xpad = refs[n_in + 1 + n_cast:]
    cast_src, cast_dst = refs[n_in - n_cast:n_in], refs[n_in + 1:n_in + 1 + n_cast]
    TL, C, H, NS = MIX_TL, CHUNK, NUM_HEADS, MIX_SLOTS

    @pl.when(pl.program_id(1) == 0)
    def _():
        st_gla[...] = jnp.zeros_like(st_gla)
        s_ret[...] = jnp.zeros_like(s_ret)
        s_gdn[...] = jnp.zeros_like(s_gdn)
        xpad[:, 0:8, :] = jnp.zeros((NS, 8, 3 * GROUP_WIDTH), F32)

    hn = _rms(jnp.concatenate([h_ref[s] for s in range(NS)], axis=0), prm["norm_mix"][...]).astype(BF16)
    w_z = prm["w_z"]
    z_cache = {}

    def proj(slot, lo, hi):
        if (lo, hi) not in z_cache:
            z_cache[lo, hi] = jnp.dot(hn, w_z[:, lo:hi], preferred_element_type=F32)
        return z_cache[lo, hi][slot * TL:(slot + 1) * TL]

    hm256b = cst["hm256"][...].astype(BF16)
    bmb = cst["bm"][...].astype(BF16)
    bm = cst["bm"][...]
    m64 = cst["m64"][...]
    ltri = cst["ltri"][...]

    def stack4b(x):
        xb = x.astype(BF16)
        return jnp.concatenate([xb, xb, xb, xb], axis=0)

    out = {}

    def gdn_stages(slot):
        zd = proj(slot, Z_GDN, Z_GDN + 1024)
        sm = proj(slot, Z_SMALL, Z_WIDTH)
        xp = xpad.at[slot]
        xp[8:8 + TL, :] = zd[:, 0:768]
        cw = prm["conv_w"][...]
        xc = cw[GDN_CONV - 1:GDN_CONV, :] * zd[:, 0:768]
        for j in range(GDN_CONV - 1):
            xc = xc + cw[j:j + 1, :] * xp[5 + j:5 + j + TL, :]
        xp[0:8, :] = zd[TL - 8:TL, 0:768]
        xc = _silu(xc)
        dq, dk, dv = xc[:, 0:256], xc[:, 256:512], xc[:, 512:768]
        yield
        dq = dq * lax.rsqrt(_dot(dq * dq, bm) + EPS) * (HEAD_DIM ** -0.5)
        dk = dk * lax.rsqrt(_dot(dk * dk, bm) + EPS)
        beta = _dot_exact_rhs(_sigmoid(sm), cst["e_beta"][...])
        g_log = _dot_exact_rhs(-jnp.exp(prm["alog_v"][...]) * _softplus(sm + prm["dtb_v"][...]),
                               cst["e_a"][...])
        yield
        gcum = _dot_exact_lhs(ltri, g_log)
        yield
        incl = cst["incl"][...]
        strict = cst["strict"][...]
        idiag = cst["idiag"][...]
        bm2 = cst["bm2"][...]
        bm2b = bm2.astype(BF16)
        n_pair = NUM_HEADS // 2
        probs = [(c, p) for c in range(TL // C) for p in range(n_pair)]

        def blk(x, c, p):
            return x[c * C:(c + 1) * C, p * 128:(p + 1) * 128]

        def bd2(x):
            xb = x.astype(BF16)
            return jnp.concatenate([xb, xb], axis=0) * bm2b

        def bdot(a, b):
            return jnp.dot(a.astype(BF16), b, preferred_element_type=F32)

        gk_, gq_, gv_, gb_, gg_, p_, t_, aqk_ = {}, {}, {}, {}, {}, {}, {}, {}
        for cp in probs:
            k_c, q_c, g_c = blk(dk, *cp), blk(dq, *cp), blk(gcum, *cp)
            gk_[cp], gq_[cp], gv_[cp], gb_[cp], gg_[cp] = k_c, q_c, blk(dv, *cp), blk(beta, *cp), g_c
            g_row = jnp.sum(g_c * idiag, axis=0, keepdims=True)
            dec = jnp.where(incl > 0.0, jnp.exp(jnp.where(incl > 0.0, g_c - g_row, 0.0)), 0.0)
            kq = lax.dot_general(jnp.concatenate([k_c, q_c], axis=0).astype(BF16), bd2(k_c),
                                 (((1,), (1,)), ((), ())), preferred_element_type=F32)
            aqk_[cp] = kq[C:2 * C] * dec
            p_[cp] = -(kq[0:C] * dec * strict * gb_[cp])
            t_[cp] = idiag + p_[cp]
        yield
        for cp in probs:
            p_[cp] = bdot(p_[cp], bd2(p_[cp]))
        yield
        for _ in range(int(math.log2(C)) - 2):
            for cp in probs:
                pt = bdot(jnp.concatenate([p_[cp], t_[cp]], axis=0), bd2(p_[cp]))
                p_[cp] = pt[0:C]
                t_[cp] = t_[cp] + pt[C:2 * C]
            yield
        for cp in probs:
            t_[cp] = t_[cp] + bdot(t_[cp], bd2(p_[cp]))
        yield
        uw_, eg_ = {}, {}
        for cp in probs:
            eg_[cp] = jnp.exp(gg_[cp])
            b_c = gb_[cp]
            uw_[cp] = bdot(t_[cp], jnp.concatenate([bd2(gv_[cp] * b_c), bd2(gk_[cp] * b_c * eg_[cp])],
                                                   axis=1))
        yield
        mm_, nn_, dl_, qp_, op_ = {}, {}, {}, {}, {}
        for cp in probs:
            k_c, g_c = gk_[cp], gg_[cp]
            g_last = g_c[C - 1:C, :]
            u_c, w_c = uw_[cp][:, 0:128], uw_[cp][:, 128:256]
            k_end_t = (k_c * jnp.exp(g_last - g_c)).T
            mn = bdot(k_end_t, jnp.concatenate([w_c, u_c], axis=1).astype(BF16))
            mm_[cp] = bm2 * mn[:, 0:128]
            nn_[cp] = bm2 * mn[:, 128:256]
            dl_[cp] = jnp.exp(g_last)
            qo = bdot(aqk_[cp], jnp.concatenate([bd2(w_c), bd2(u_c)], axis=1))
            qp_[cp] = gq_[cp] * eg_[cp] - qo[:, 0:128]
            op_[cp] = qo[:, 128:256]
        yield
        o_rows = []
        sgp = [s_gdn[slot, p] for p in range(n_pair)]
        for c in range(TL // C):
            o_pair = []
            for p in range(n_pair):
                cp = (c, p)
                s_b = sgp[p].astype(BF16)
                o_pair.append(bdot(qp_[cp], s_b) + op_[cp])
                sgp[p] = sgp[p] * dl_[cp] - bdot(mm_[cp], s_b) + nn_[cp]
            o_rows.append(jnp.concatenate(o_pair, axis=1))
            yield
        for p in range(n_pair):
            s_gdn[slot, p] = sgp[p]
        o_d = jnp.concatenate(o_rows, axis=0)
        out[slot, "d"] = o_d * lax.rsqrt(_dot(o_d * o_d, m64) + EPS) * prm["gdn_g"][...] * _silu(zd[:, 768:1024])

    def gla_stages(slot):
        zg = proj(slot, Z_GLA, Z_GLA + 768)
        sm = proj(slot, Z_SMALL, Z_WIDTH)
        q, k, v, r = zg[:, 0:128], zg[:, 128:256], zg[:, 256:512], zg[:, 512:768]
        gk = _dot(sm, prm["w_a2"][...]) + prm["b_a"][...]
        gk = (jnp.minimum(gk, 0.0) - jnp.log1p(jnp.exp(-jnp.abs(gk)))) / GLA_GATE_NORM
        yield
        bcum = _dot_exact_lhs(ltri, gk)
        caus4 = cst["caus4"][...]
        bmt = cst["bmt"][...]
        bmtb = bmt.astype(BF16)
        st = st_gla[slot]
        yield
        o_chunks = []
        for c in range(TL // C):
            sl = slice(c * C, (c + 1) * C)
            b_c = bcum[sl]
            b_last = b_c[C - 1:C, :]
            q_t = q[sl] * jnp.exp(b_c) * (GLA_DK ** -0.5)
            k_t = k[sl] * jnp.exp(-b_c)
            k_end = k[sl] * jnp.exp(b_last - b_c)
            v_c = v[sl]
            q_b = q_t.astype(BF16)
            att = lax.dot_general(q_b, stack4b(k_t) * bmtb, (((1,), (1,)), ((), ())),
                                  preferred_element_type=F32) * caus4
            cross = lax.dot_general(q_b, st.astype(BF16), (((1,), (1,)), ((), ())),
                                    preferred_element_type=F32)
            st = st * jnp.exp(b_last) + bmt * _dot(v_c.T, k_end)
            yield
            o_chunks.append(jnp.dot(att.astype(BF16), stack4b(v_c) * bmb, preferred_element_type=F32)
                            + cross)
            yield
        st_gla[slot] = st
        o_a = jnp.concatenate(o_chunks, axis=0)
        out[slot, "a"] =o_a * lax.rsqrt(_dot(o_a * o_a, m64) + EPS) * prm["gla_g"][...] * _silu(r)

    def ret_stages(slot):
        zr = proj(slot, Z_RET, Z_RET + 1024)
        cos2 = jnp.concatenate([cos_ref[slot], cos_ref[slot]], axis=1)
        sin2 = jnp.concatenate([sin_ref[slot], sin_ref[slot]], axis=1)
        first_half = (lax.broadcasted_iota(jnp.int32, (TL, 256), 1) % HEAD_DIM) < HEAD_DIM // 2

        def rope(x):
            rot = jnp.where(first_half, pltpu.roll(x, 256 - HEAD_DIM // 2, 1),
                            pltpu.roll(x, HEAD_DIM // 2, 1))
            return x * cos2 + rot * sin2

        rq = rope(zr[:, 0:256])
        rk = rope(zr[:, 256:512]) * (HEAD_DIM ** -0.5)
        rv = zr[:, 512:768]
        rg = zr[:, 768:1024]
        yield
        rq_b = rq.astype(BF16)
        qs = jnp.concatenate([rq_b * hm256b[hh:hh + 1] for hh in range(H)], axis=0)
        sc = lax.dot_general(qs, rk.astype(BF16), (((1,), (1,)), ((), ())),
                             preferred_element_type=F32) * cst["dmask"][...]
        s_prev = s_ret[slot]
        cross = jnp.dot(rq_b, s_prev.astype(BF16), preferred_element_type=F32) * cst["xi"][...]
        s_ret[slot] = s_prev * cst["gchunk"][...] + bm * _dot((rk * cst["zeta"][...]).T, rv)
        yield
        sc_wide = jnp.concatenate([sc[hh * TL:(hh + 1) * TL] for hh in range(H)], axis=1).astype(BF16)
        o_c = jnp.dot(sc_wide, stack4b(rv) * cst["bm4"][...].astype(BF16), preferred_element_type=F32) + cross
        yield
        o_c = o_c - _dot_exact_rhs(o_c, m64)
        yield
        out[slot, "c"] = o_c * lax.rsqrt(_dot(o_c * o_c, m64) + EPS) * prm["ret_g"][...] * _silu(rg)

    def s5_stages(slot):
        y = ys5_ref[slot]
        y = 0.5 * y * (1.0 + jnp.tanh(math.sqrt(2.0 / math.pi) * (y + 0.044715 * (y * y * y))))
        yield
        out[slot, "b"] = y * _sigmoid(_dot(y, prm["w_glu"][...]) + prm["b_glu"][...])

    def cast_stages():
        for src, dst in zip(cast_src, cast_dst):
            yield
            _cast_ride_along([src], [dst])

    running = [stages(slot) for stages in (gdn_stages, gla_stages, ret_stages, s5_stages) for slot in range(NS)]
    running.append(cast_stages())
    while running:
        for gen in list(running):
            if next(gen, "done") == "done":
                running.remove(gen)

    mix = jnp.concatenate([jnp.concatenate([out[s, "a"], out[s, "b"], out[s, "c"], out[s, "d"]], axis=1)
                           for s in range(NS)], axis=0).astype(BF16)
    res = jnp.dot(mix, prm["w_out"][...], preferred_element_type=F32)
    for s in range(NS):
        o_ref[s] = h_ref[s] + res[s * TL:(s + 1) * TL]


def _mixer_call(h, ys5, cos_t, sin_t, consts, params, layer, cast_weights):
    B, L, D = h.shape
    TL, NS = MIX_TL, MIX_SLOTS
    n_i = L // TL
    casts = [_cast_stream(w, layer, (B // NS) * n_i, lambda b, i: b * n_i + i) for w in cast_weights]

    def const(arr):
        nd = arr.ndim
        return pl.BlockSpec(arr.shape, lambda b, i: (0,) * nd, pipeline_mode=pl.Buffered(1))

    def tile(width):
        return pl.BlockSpec((NS, TL, width), lambda b, i: (b, i, 0))

    c_list = [consts[n] for n in _CONST_ORDER]
    p_list = [params[n] for n in _PARAM_ORDER]
    return pl.pallas_call(
        functools.partial(_mixer_kernel, n_cast=len(casts)),
        grid=(B // NS, n_i),
        in_specs=[tile(D), tile(GROUP_WIDTH), tile(128), tile(128)] + [const(a) for a in c_list]
        + [_layer_spec(a, layer, 2) for a in p_list] + [c[0] for c in casts],
        out_specs=[tile(D)] + [c[1] for c in casts],
        out_shape=[jax.ShapeDtypeStruct((B, L, D), F32)] + [c[2] for c in casts],
        scratch_shapes=[pltpu.VMEM((NS, 256, 128), F32), pltpu.VMEM((NS, 256, 256), F32),
                        pltpu.VMEM((NS, 2, 128, 128), F32), pltpu.VMEM((NS, TL + 8, 3 * GROUP_WIDTH), F32)],
        compiler_params=pltpu.CompilerParams(dimension_semantics=("arbitrary", "arbitrary"),
                                             vmem_limit_bytes=VMEM_LIMIT),
        name="mixer",
    )(h, ys5, cos_t, sin_t, *c_list, *p_list, *cast_weights)


def _ffn_kernel(h_ref, p_ref, gf_ref, wup_ref, wdn_ref, gp_ref, wg_ref, wp_ref, gl_ref, o_ref, act_ref, *,
                final):
    h = h_ref[...]
    hn = _rms(h, gf_ref[...]).astype(BF16)
    for s in range(FFN_HIDDEN // FFN_SLAB):
        lo = s * FFN_SLAB
        g = jnp.dot(hn, wup_ref[:, lo:lo + FFN_SLAB], preferred_element_type=F32)
        u = jnp.dot(hn, wup_ref[:, FFN_HIDDEN + lo:FFN_HIDDEN + lo + FFN_SLAB], preferred_element_type=F32)
        act_ref[:, lo:lo + FFN_SLAB] = (_silu(g) * u).astype(BF16)
    h2 = h + jnp.dot(act_ref[...], wdn_ref[...], preferred_element_type=F32)
    gate = _sigmoid(jnp.dot(_rms(h2, gp_ref[...]).astype(BF16), wg_ref[...], preferred_element_type=F32))
    h3 = h2 + jnp.dot(p_ref[...].astype(BF16), wp_ref[...], preferred_element_type=F32) * gate
    if final:
        h3 = _rms(h3, gl_ref[...])
    o_ref[...] = h3


def _ffn_call(h2d, p3d, layer, weights, final):
    T, D = h2d.shape
    TM = FFN_TM
    return pl.pallas_call(
        functools.partial(_ffn_kernel, final=final),
        grid=(T // TM,),
        in_specs=[pl.BlockSpec((TM, D), lambda i: (i, 0)),
                  pl.BlockSpec((None, TM, PLE_DIM), lambda i: (layer, i, 0))]
        + [_layer_spec(w, layer, 1) for w in weights],
        out_specs=pl.BlockSpec((TM, D), lambda i: (i, 0)),
        out_shape=jax.ShapeDtypeStruct((T, D), F32),
        scratch_shapes=[pltpu.VMEM((TM, FFN_HIDDEN), BF16)],
        compiler_params=pltpu.CompilerParams(dimension_semantics=("arbitrary",), vmem_limit_bytes=VMEM_LIMIT),
        name="ffn_ple",
    )(h2d, p3d, *weights)


def _row(v):
    return v[:, None, :]


_WZ_SEGMENTS = (_AQ, _AK, _AV, _AR, _RQ, _RK, _RV, _RG, _DQ, _DK, _DV, _DG)


def _regroup_kernel(wt_ref, wz_ref, wsu_ref):
    col = 0
    for lo, hi in _WZ_SEGMENTS:
        wz_ref[:, col:col + hi - lo] = wt_ref[lo:hi, :].T.astype(BF16)
        col += hi - lo
    assert col == Z_SMALL and _DA[0] == _DB[1] and (_DA[1] - _DB[0]) % 8 == 0
    small = jnp.concatenate([wt_ref[_ALOW[0]:_ALOW[1], :], wt_ref[_DB[0]:_DA[1], :],
                             jnp.zeros((128 - GLA_RANK - 2 * NUM_HEADS, D_MODEL), F32)], axis=0)
    wz_ref[:, Z_SMALL:Z_WIDTH] = small.T.astype(BF16)
    wsu_ref[...] = wt_ref[_SU[0]:_SU[1], :].T.astype(BF16)


def _regroup_w_in(w_in):
    depth, K, N = w_in.shape
    return pl.pallas_call(
        _regroup_kernel,
        grid=(depth,),
        in_specs=[pl.BlockSpec((None, N, K), lambda d: (d, 0, 0))],
        out_specs=[pl.BlockSpec((None, K, Z_WIDTH), lambda d: (d, 0, 0)),
                   pl.BlockSpec((None, K, GROUP_WIDTH), lambda d: (d, 0, 0))],
        out_shape=[jax.ShapeDtypeStruct((depth, K, Z_WIDTH), BF16),
                   jax.ShapeDtypeStruct((depth, K, GROUP_WIDTH), BF16)],
        compiler_params=pltpu.CompilerParams(vmem_limit_bytes=VMEM_LIMIT),
        name="regroup_w_in",
    )(jnp.swapaxes(w_in, 1, 2))


def _mixer_params(norm_mix, w_z, gla_w_a2, gla_b_a, gla_norm, ret_norm, gdn_conv, gdn_a_log,
                  gdn_dt_bias, gdn_norm, s5_b_glu):
    w_a2 = jnp.pad(gla_w_a2, ((0, 0), (0, 128 - GLA_RANK), (0, 0)))
    lane_pad = ((0, 0), (SM_A, 128 - SM_A - NUM_HEADS))
    return dict(norm_mix=_row(norm_mix), w_z=w_z, w_a2=w_a2, b_a=_row(gla_b_a),
                gla_g=_row(jnp.tile(gla_norm, (1, NUM_HEADS))), ret_g=_row(ret_norm),
                gdn_g=_row(jnp.tile(gdn_norm, (1, NUM_HEADS))), conv_w=gdn_conv,
                alog_v=_row(jnp.pad(gdn_a_log, lane_pad)), dtb_v=_row(jnp.pad(gdn_dt_bias, lane_pad)),
                b_glu=_row(s5_b_glu))


def kernel(x, p, positions, norm_mix, w_in, w_out, gla_w_a2, gla_b_a, gla_norm, s5_lam_re, s5_lam_im, s5_log_dt,
           s5_b_re, s5_b_im, s5_c_re, s5_c_im, s5_d, s5_w_glu, s5_b_glu, ret_norm, gdn_conv, gdn_a_log,
           gdn_dt_bias, gdn_norm, norm_ffn, w_ffn_up, w_ffn_down, norm_ple, w_ple_gate, w_ple_proj, norm_final):
    B, L, D = x.shape
    depth = w_in.shape[0]
    assert D == D_MODEL and L % max(MIX_TL, S5_SUB * S5_ROWS) == 0 and (B * L) % FFN_TM == 0
    assert B % MIX_SLOTS == 0
    consts = {k: jnp.asarray(v) for k, v in _mixer_consts().items()}
    cos_t, sin_t = _rope_tables(positions)
    prep = _s5_prepare(s5_lam_re, s5_lam_im, s5_log_dt, s5_b_re, s5_b_im, s5_c_re, s5_c_im, s5_d)
    w_z, w_su = _regroup_w_in(w_in)
    params = _mixer_params(norm_mix, w_z, gla_w_a2, gla_b_a, gla_norm, ret_norm, gdn_conv,
                           gdn_a_log, gdn_dt_bias, gdn_norm, s5_b_glu)
    g_ffn, g_ple = _row(norm_ffn), _row(norm_ple)
    g_final = jnp.broadcast_to(norm_final[None, None, :], (depth, 1, D))
    p3d = p.reshape(depth, B * L, PLE_DIM)
    h = x
    for i in range(depth):
        ys5, w_out_b, w_glu_b = _s5_call(h, i, _row(norm_mix), w_su, prep, [w_out, s5_w_glu])
        h, w_up_b, w_down_b, w_gate_b, w_proj_b = _mixer_call(
            h, ys5, cos_t, sin_t, consts, dict(params, w_out=w_out_b, w_glu=w_glu_b), i,
            [w_ffn_up, w_ffn_down, w_ple_gate, w_ple_proj])
        ffn_w = [g_ffn, w_up_b, w_down_b, g_ple, w_gate_b, w_proj_b, g_final]
        h = _ffn_call(h.reshape(B * L, D), p3d, i, ffn_w, final=(i == depth - 1)).reshape(B, L, D)
    return h
```

```python
import functools
import math

import numpy as np
import jax
import jax.numpy as jnp
from jax import lax
from jax.experimental import pallas as pl
from jax.experimental.pallas import tpu as pltpu

F32 = jnp.float32
BF16 = jnp.bfloat16

D_MODEL = 1024
PLE_DIM = 256
GROUP_WIDTH = 256
NUM_HEADS = 4
CHUNK = 64
EPS = 1e-6

GLA_DK = 32
GLA_RANK = 16
GLA_GATE_NORM = 16.0
S5_CH = 16
S5_GROUPS = 16
S5_STATE = 64
S5_SUB = 8
HEAD_DIM = 64
ROPE_BASE = 10000.0
GDN_CONV = 4
FFN_HIDDEN = 2816

MIX_TL = 256
MIX_SLOTS = 2
S5_ROWS = 256
S5_PARTS = 2
FFN_TM = 1024
FFN_SLAB = 256

VMEM_LIMIT = 56 * 1024 * 1024

_IN_OFFS = np.cumsum([0, 128, 128, 256, 16, 256, 256, 256, 256, 256, 256, 256, 256, 256, 4, 4, 256])
(_AQ, _AK, _AV, _ALOW, _AR, _SU, _RQ, _RK, _RV, _RG, _DQ, _DK, _DV, _DB, _DA, _DG) = [
    (int(_IN_OFFS[i]), int(_IN_OFFS[i + 1])) for i in range(16)]
Z_GLA = 0
Z_RET = 768
Z_GDN = 1792
Z_SMALL = 2816
Z_WIDTH = 2944
SM_BETA = 16
SM_A = 20


def _dot(a, b):
    return jnp.dot(a.astype(BF16), b.astype(BF16), preferred_element_type=F32)


def _split(x):
    hi = x.astype(BF16)
    return hi, (x - hi.astype(F32)).astype(BF16)


def _dot_exact_lhs(a, b):
    hi, lo = _split(b)
    n = b.shape[1]
    if n <= 128:
        r = jnp.dot(a.astype(BF16), jnp.concatenate([hi, lo], axis=1), preferred_element_type=F32)
        return r[:, :n] + r[:, n:]
    return (jnp.dot(a.astype(BF16), hi, preferred_element_type=F32)
            + jnp.dot(a.astype(BF16), lo, preferred_element_type=F32))


def _dot_exact_rhs(a, b):
    hi, lo = _split(a)
    m = a.shape[0]
    r = jnp.dot(jnp.concatenate([hi, lo], axis=0), b.astype(BF16), preferred_element_type=F32)
    return r[:m] + r[m:]


def _sigmoid(x):
    return 0.5 * jnp.tanh(0.5 * x) + 0.5


def _silu(x):
    return x * _sigmoid(x)


def _softplus(x):
    return jnp.maximum(x, 0.0) + jnp.log1p(jnp.exp(-jnp.abs(x)))


def _rms(x, g):
    return x * lax.rsqrt(jnp.mean(x * x, axis=-1, keepdims=True) + EPS) * g


def _rope_kernel(pos_ref, invf_ref, cos_ref, sin_ref):
    ang = invf_ref[...] * pos_ref[...].astype(F32)
    c = jnp.cos(ang)
    s = jnp.sin(ang)
    cos_ref[...] = jnp.concatenate([c, c, c, c], axis=0).T
    sin_ref[...] = jnp.concatenate([-s, s, -s, s], axis=0).T


def _rope_tables(positions):
    B, L = positions.shape
    inv_freq = ROPE_BASE ** (-jnp.linspace(0.0, 1.0, HEAD_DIM // 2, dtype=F32))
    spec = pl.BlockSpec((None, L, 128), lambda b: (b, 0, 0))
    return pl.pallas_call(
        _rope_kernel,
        grid=(B,),
        in_specs=[pl.BlockSpec((None, 1, L), lambda b: (b, 0, 0)),
                  pl.BlockSpec((HEAD_DIM // 2, 1), lambda b: (0, 0))],
        out_specs=[spec, spec],
        out_shape=[jax.ShapeDtypeStruct((B, L, 128), F32)] * 2,
        name="rope_tables",
    )(positions[:, None, :], inv_freq[:, None])


def _s5_kernel(*refs, n_cast):
    (h_ref, g_ref, wsu_ref, krev_ref, bbig_ref, cbig_ref, pwr_ref, pwi_ref, d8_ref) = refs[:9]
    y_ref = refs[9 + n_cast]
    cr_ref, ci_ref, su_scr, y_scr = refs[10 + 2 * n_cast:]
    n_state = S5_GROUPS * S5_STATE
    W = GROUP_WIDTH
    n_slab = W // 128

    @pl.when(pl.program_id(1) == 0)
    def _():
        cr_ref[...] = jnp.zeros_like(cr_ref)
        ci_ref[...] = jnp.zeros_like(ci_ref)

    R = S5_ROWS // S5_PARTS
    tok = R * S5_SUB
    n_sub = S5_SUB
    carry = {"r": cr_ref[...], "i": ci_ref[...], "parts_done": 0}
    rows = lax.broadcasted_iota(jnp.int32, (R, n_state), 0)
    first = rows == 0

    def shifted(x, shift):
        if shift % 8 == 0:
            return jnp.concatenate([jnp.zeros((shift, x.shape[1]), F32), x[:R - shift]], axis=0)
        return jnp.where(rows >= shift, pltpu.roll(x, shift, 0), 0.0)

    def part_stages(part):
        base = part * tok
        n_proj = 2
        blk = tok // n_proj
        for q in range(n_sub):
            if q % (n_sub // n_proj) == 0:
                lo = base + (q // (n_sub // n_proj)) * blk
                su = _dot(_rms(h_ref[lo:lo + blk, :], g_ref[...]), wsu_ref[...])
                for s in range(n_slab):
                    su_scr[s, lo:lo + blk, :] = su[:, s * 128:(s + 1) * 128]
            yield
        u8 = jnp.concatenate([su_scr[s, pl.ds(base + j, R, stride=S5_SUB), :]
                              for j in range(S5_SUB) for s in range(n_slab)], axis=1)
        u8b = u8.astype(BF16)
        y_loc, inc = [], []
        n_col = 2 * n_state // n_sub
        for t in range(n_sub):
            y_loc.append(jnp.dot(u8b[:, 0:(t + 1) * W], krev_ref[(S5_SUB - 1 - t) * W:S5_SUB * W, :],
                                 preferred_element_type=F32)
                         + d8_ref[:, t * W:(t + 1) * W] * u8[:, t * W:(t + 1) * W])
            inc.append(jnp.dot(u8b, bbig_ref[:, t * n_col:(t + 1) * n_col], preferred_element_type=F32))
            yield
        inc = jnp.concatenate(inc, axis=1)
        xr, xi = inc[:, :n_state], inc[:, n_state:]
        assert carry["parts_done"] == part
        cr, ci = carry["r"], carry["i"]
        ar0, ai0 = pwr_ref[0:1, :], pwi_ref[0:1, :]
        xr = xr + jnp.where(first, ar0 * cr - ai0 * ci, 0.0)
        xi = xi + jnp.where(first, ar0 * ci + ai0 * cr, 0.0)
        shift, lvl = 1, 0
        while shift < R:
            ar, ai = pwr_ref[lvl:lvl + 1, :], pwi_ref[lvl:lvl + 1, :]
            sr, si = shifted(xr, shift), shifted(xi, shift)
            xr, xi = xr + ar * sr - ai * si, xi + ar * si + ai * sr
            shift, lvl = shift * 2, lvl + 1
            yield
        pr = jnp.where(first, cr, pltpu.roll(xr, 1, 0))
        pi = jnp.where(first, ci, pltpu.roll(xi, 1, 0))
        carry.update(r=xr[R - 1:R, :], i=xi[R - 1:R, :], parts_done=part + 1)
        xprev = jnp.concatenate([pr, pi], axis=1).astype(BF16)
        yield
        for j in range(S5_SUB):
            yj = y_loc[j] + jnp.dot(xprev, cbig_ref[:, j * W:(j + 1) * W], preferred_element_type=F32)
            for s in range(n_slab):
                y_scr[s, pl.ds(base + j, R, stride=S5_SUB), :] = yj[:, s * 128:(s + 1) * 128]
            yield
        y_ref[base:base + tok, :] = jnp.concatenate([y_scr[s, base:base + tok, :] for s in range(n_slab)],
                                                    axis=1)

    gens = [part_stages(part) for part in range(S5_PARTS)]
    for lead in range(S5_PARTS):
        for _ in range(n_sub):
            for gen in gens[:lead + 1]:
                next(gen, None)
    live = list(gens)
    while live:
        for gen in list(live):
            if next(gen, "done") == "done":
                live.remove(gen)
    cr_ref[...] = carry["r"]
    ci_ref[...] = carry["i"]
    _cast_ride_along(refs[9:9 + n_cast], refs[10 + n_cast:10 + 2 * n_cast])


def _s5_prepare(lam_re, lam_im, log_dt, b_re, b_im, c_re, c_im, d_skip):
    G, P, H, M = S5_GROUPS, S5_STATE, S5_CH, S5_SUB
    depth = lam_re.shape[0]
    lr = jnp.minimum(lam_re.astype(F32), -1e-4)
    li = lam_im.astype(F32)
    dt = jnp.exp(log_dt.astype(F32))[:, :, None]

    def apow(t):
        tt = jnp.asarray(t, F32)[None, :, None, None]
        mag = jnp.exp((lr * dt)[:, None] * tt)
        ang = (li * dt)[:, None] * tt
        return mag * jnp.cos(ang), mag * jnp.sin(ang)

    n_lvl = int(math.log2(S5_ROWS))
    exps = list(range(M + 1)) + list(range(M - 1, -1, -1)) + [M * 2 ** s for s in range(n_lvl)]
    pw_r, pw_i = apow(np.asarray(exps, np.float32))
    rev_r, rev_i = pw_r[:, M + 1:2 * M + 1], pw_i[:, M + 1:2 * M + 1]
    ar, ai = pw_r[:, 1], pw_i[:, 1]
    nr, ni = ar - 1.0, ai
    den = lr * lr + li * li
    fr = ((nr * lr + ni * li) / den)[..., None]
    fi = ((ni * lr - nr * li) / den)[..., None]
    bbr = fr * b_re - fi * b_im
    bbi = fr * b_im + fi * b_re

    def embed(narrow, tile, row_div, col_div):
        wide = jnp.einsum('dnk,km->dnm', narrow.astype(BF16), jnp.asarray(tile, BF16),
                          preferred_element_type=F32)
        rg = (lax.broadcasted_iota(jnp.int32, wide.shape, 1) // row_div) % G
        cg = (lax.broadcasted_iota(jnp.int32, wide.shape, 2) // col_div) % G
        return jnp.where(rg == cg, wide, 0.0).astype(BF16)

    pr, pi = rev_r[:, :, :, None, :], rev_i[:, :, :, None, :]
    car = c_re[:, None] * pr - c_im[:, None] * pi
    cai = c_re[:, None] * pi + c_im[:, None] * pr
    bbr_t = bbr.transpose(0, 1, 3, 2)[:, None, :, :, None, :]
    bbi_t = bbi.transpose(0, 1, 3, 2)[:, None, :, :, None, :]
    kt = jnp.sum(car[:, :, :, None] * bbr_t - cai[:, :, :, None] * bbi_t, axis=-1)
    k_narrow = kt.reshape(depth, M * G * H, H)
    k_tile = np.tile(np.eye(H, dtype=np.float32), (1, G))
    krev = embed(k_narrow, k_tile, H, H)

    pr, pi = rev_r[..., None], rev_i[..., None]
    b_narrow = jnp.stack([pr * bbr[:, None] - pi * bbi[:, None], pr * bbi[:, None] + pi * bbr[:, None]],
                         axis=1)
    b_narrow = b_narrow.transpose(0, 2, 3, 5, 1, 4).reshape(depth, M * G * H, 2 * P)
    cp = np.arange(2 * P)
    col = np.arange(2 * G * P)
    b_tile = ((cp[:, None] // P == (col // (G * P))[None, :]) & (cp[:, None] % P == (col % P)[None, :]))
    bbig = embed(b_narrow, b_tile.astype(np.float32), H, P)

    pr, pi = pw_r[:, 1:M + 1, :, None, :], pw_i[:, 1:M + 1, :, None, :]
    qr = c_re[:, None] * pr - c_im[:, None] * pi
    qi = -(c_re[:, None] * pi + c_im[:, None] * pr)
    c_narrow = jnp.stack([qr, qi], axis=1).transpose(0, 1, 3, 5, 2, 4).reshape(depth, 2 * G * P, M * H)
    th = np.arange(M * H)
    col = np.arange(M * G * H)
    c_tile = ((th[:, None] // H == (col // (G * H))[None, :]) & (th[:, None] % H == (col % H)[None, :]))
    cbig = embed(c_narrow, c_tile.astype(np.float32), P, H)

    pad = (-n_lvl) % 8
    pwr = jnp.pad(pw_r[:, 2 * M + 1:].reshape(depth, n_lvl, G * P), ((0, 0), (0, pad), (0, 0)))
    pwi = jnp.pad(pw_i[:, 2 * M + 1:].reshape(depth, n_lvl, G * P), ((0, 0), (0, pad), (0, 0)))
    d8 = jnp.tile(d_skip.astype(F32), (1, M))[:, None, :]
    return krev, bbig, cbig, pwr, pwi, d8


def _layer_spec(arr, layer, n_grid):
    if arr.ndim == 2:
        return pl.BlockSpec(arr.shape, lambda *g: (0, 0), pipeline_mode=pl.Buffered(1))
    zeros = (0,) * (arr.ndim - 1)
    return pl.BlockSpec((None,) + arr.shape[1:], lambda *g: (layer,) + zeros, pipeline_mode=pl.Buffered(1))


def _cast_stream(w, layer, n_steps, step_of):
    _, K, N = w.shape
    nb = max(d for d in range(1, n_steps + 1) if n_steps % d == 0 and K % d == 0 and (K // d) % 16 == 0)
    rows, per = K // nb, n_steps // nb
    in_spec = pl.BlockSpec((None, rows, N), lambda *g: (layer, step_of(*g) // per, 0))
    out_spec = pl.BlockSpec((rows, N), lambda *g: (step_of(*g) // per, 0))
    return in_spec, out_spec, jax.ShapeDtypeStruct((K, N), BF16)


def _cast_ride_along(src_refs, dst_refs):
    for src, dst in zip(src_refs, dst_refs):
        dst[...] = src[...].astype(BF16)


def _s5_call(h, layer, norm_g, w_su, prep, cast_weights):
    B, L, D = h.shape
    krev, bbig, cbig, pwr, pwi, d8 = prep
    tokens = S5_SUB * S5_ROWS
    n_state = S5_GROUPS * S5_STATE
    n_slab = GROUP_WIDTH // 128
    params = [norm_g, w_su, krev, bbig, cbig, pwr, pwi, d8]
    n_i = L // tokens
    casts = [_cast_stream(w, layer, B * n_i, lambda b, i: b * n_i + i) for w in cast_weights]

    return pl.pallas_call(
        functools.partial(_s5_kernel, n_cast=len(casts)),
        grid=(B, n_i),
        in_specs=[pl.BlockSpec((None, tokens, D), lambda b, i: (b, i, 0))]
        + [_layer_spec(a, layer, 2) for a in params] + [c[0] for c in casts],
        out_specs=[pl.BlockSpec((None, tokens, GROUP_WIDTH), lambda b, i: (b, i, 0))] + [c[1] for c in casts],
        out_shape=[jax.ShapeDtypeStruct((B, L, GROUP_WIDTH), F32)] + [c[2] for c in casts],
        scratch_shapes=[pltpu.VMEM((1, n_state), F32), pltpu.VMEM((1, n_state), F32),
                        pltpu.VMEM((n_slab, tokens, 128), F32), pltpu.VMEM((n_slab, tokens, 128), F32)],
        compiler_params=pltpu.CompilerParams(dimension_semantics=("arbitrary", "arbitrary"),
                                             vmem_limit_bytes=VMEM_LIMIT),
        name="s5_mixer",
    )(h, *params, *cast_weights)


def _mixer_consts():
    TL, C, H = MIX_TL, CHUNK, NUM_HEADS
    t = np.arange(TL)
    ltri = ((t[:, None] // C == t[None, :] // C) & (t[None, :] <= t[:, None])).astype(np.float32)
    lane256 = np.arange(256)
    lane128 = np.arange(128)
    hm256 = (lane256[None, :] // HEAD_DIM == np.arange(H)[:, None]).astype(np.float32)
    bm = (lane256[:, None] // HEAD_DIM == lane256[None, :] // HEAD_DIM).astype(np.float32)
    bmt = (lane256[:, None] // HEAD_DIM == lane128[None, :] // GLA_DK).astype(np.float32)
    c = np.arange(C)
    caus4 = (c[:, None] >= (lane256 % HEAD_DIM)[None, :]).astype(np.float32)
    bm4 = np.repeat(hm256, TL, axis=0)
    s_lane = lane128 % HEAD_DIM
    bm2 = bm[0:128, 0:128]
    incl = (c[:, None] >= s_lane[None, :]).astype(np.float32)
    strict = (c[:, None] > s_lane[None, :]).astype(np.float32)
    idiag = (c[:, None] == s_lane[None, :]).astype(np.float32)
    log_gamma = np.log1p(-(2.0 ** (-5.0 - np.arange(H, dtype=np.float32)))).astype(np.float32)
    idx = np.arange(TL, dtype=np.float32)
    rel = idx[:, None] - idx[None, :]
    dmask = np.where(rel >= 0, np.exp(np.maximum(rel, 0.0)[None] * log_gamma[:, None, None]), 0.0)
    dmask = dmask.reshape(H * TL, TL).astype(np.float32)
    lg_lane = log_gamma[lane256 // HEAD_DIM]
    xi = np.exp((idx[:, None] + 1.0) * lg_lane[None, :]).astype(np.float32)
    zeta = np.exp((TL - 1.0 - idx[:, None]) * lg_lane[None, :]).astype(np.float32)
    gchunk = np.exp(TL * lg_lane)[None, :].astype(np.float32)
    e_beta = np.zeros((128, 256), np.float32)
    e_a = np.zeros((128, 256), np.float32)
    for h in range(H):
        e_beta[SM_BETA + h, h * HEAD_DIM:(h + 1) * HEAD_DIM] = 1.0
        e_a[SM_A + h, h * HEAD_DIM:(h + 1) * HEAD_DIM] = 1.0
    return dict(ltri=ltri, hm256=hm256, bm=bm, bmt=bmt, caus4=caus4, bm4=bm4, incl=incl, strict=strict,
                idiag=idiag, bm2=bm2, dmask=dmask, xi=xi, zeta=zeta, gchunk=gchunk, e_beta=e_beta, e_a=e_a,
                m64=bm / HEAD_DIM)


_CONST_ORDER = ("ltri", "hm256", "bm", "bmt", "caus4", "bm4", "incl", "strict", "idiag", "bm2", "dmask", "xi",
                "zeta", "gchunk", "e_beta", "e_a", "m64")
_PARAM_ORDER = ("norm_mix", "w_z", "w_out", "w_a2", "b_a", "gla_g", "ret_g", "gdn_g", "conv_w", "alog_v",
                "dtb_v", "w_glu", "b_glu")


def _mixer_kernel(*refs, n_cast):
    n_c, n_p = len(_CONST_ORDER), len(_PARAM_ORDER)
    h_ref, ys5_ref, cos_ref, sin_ref = refs[:4]
    cst = dict(zip(_CONST_ORDER, refs[4:4 + n_c]))
    prm = dict(zip(_PARAM_ORDER, refs[4 + n_c:4 + n_c + n_p]))
    n_in = 4 + n_c + n_p + n_cast
    o_ref = refs[n_in]
    st_gla, s_ret, s_gdn, xpad = refs[n_in + 1 + n_cast:]
    cast_src, cast_dst = refs[n_in - n_cast:n_in], refs[n_in + 1:n_in + 1 + n_cast]
    TL, C, H, NS = MIX_TL, CHUNK, NUM_HEADS, MIX_SLOTS

    @pl.when(pl.program_id(1) == 0)
    def _():
        st_gla[...] = jnp.zeros_like(st_gla)
        s_ret[...] = jnp.zeros_like(s_ret)
        s_gdn[...] = jnp.zeros_like(s_gdn)
        xpad[:, 0:8, :] = jnp.zeros((NS, 8, 3 * GROUP_WIDTH), F32)

    hn = _rms(jnp.concatenate([h_ref[s] for s in range(NS)], axis=0), prm["norm_mix"][...]).astype(BF16)
    w_z = prm["w_z"]
    z_cache = {}

    def proj(slot, lo, hi):
        if (lo, hi) not in z_cache:
            z_cache[lo, hi] = jnp.dot(hn, w_z[:, lo:hi], preferred_element_type=F32)
        return z_cache[lo, hi][slot * TL:(slot + 1) * TL]

    hm256b = cst["hm256"][...].astype(BF16)
    bmb = cst["bm"][...].astype(BF16)
    bm = cst["bm"][...]
    m64 = cst["m64"][...]
    ltri = cst["ltri"][...]

    def stack4b(x):
        xb = x.astype(BF16)
        return jnp.concatenate([xb, xb, xb, xb], axis=0)

    out = {}

    def gdn_stages(slot):
        zd = proj(slot, Z_GDN, Z_GDN + 1024)
        sm = proj(slot, Z_SMALL, Z_WIDTH)
        xp = xpad.at[slot]
        xp[8:8 + TL, :] = zd[:, 0:768]
        cw = prm["conv_w"][...]
        xc = cw[GDN_CONV - 1:GDN_CONV, :] * zd[:, 0:768]
        for j in range(GDN_CONV - 1):
            xc = xc + cw[j:j + 1, :] * xp[5 + j:5 + j + TL, :]
        xp[0:8, :] = zd[TL - 8:TL, 0:768]
        xc = _silu(xc)
        dq, dk, dv = xc[:, 0:256], xc[:, 256:512], xc[:, 512:768]
        yield
        dq = dq * lax.rsqrt(_dot(dq * dq, bm) + EPS) * (HEAD_DIM ** -0.5)
        dk = dk * lax.rsqrt(_dot(dk * dk, bm) + EPS)
        beta = _dot_exact_rhs(_sigmoid(sm), cst["e_beta"][...])
        g_log = _dot_exact_rhs(-jnp.exp(prm["alog_v"][...]) * _softplus(sm + prm["dtb_v"][...]),
                               cst["e_a"][...])
        yield
        gcum = _dot_exact_lhs(ltri, g_log)
        yield
        incl = cst["incl"][...]
        strict = cst["strict"][...]
        idiag = cst["idiag"][...]
        bm2 = cst["bm2"][...]
        bm2b = bm2.astype(BF16)
        n_pair = NUM_HEADS // 2
        probs = [(c, p) for c in range(TL // C) for p in range(n_pair)]

        def blk(x, c, p):
            return x[c * C:(c + 1) * C, p * 128:(p + 1) * 128]

        def bd2(x):
            xb = x.astype(BF16)
            return jnp.concatenate([xb, xb], axis=0) * bm2b

        def bdot(a, b):
            return jnp.dot(a.astype(BF16), b, preferred_element_type=F32)

        gk_, gq_, gv_, gb_, gg_, p_, t_, aqk_ = {}, {}, {}, {}, {}, {}, {}, {}
        for cp in probs:
            k_c, q_c, g_c = blk(dk, *cp), blk(dq, *cp), blk(gcum, *cp)
            gk_[cp], gq_[cp], gv_[cp], gb_[cp], gg_[cp] = k_c, q_c, blk(dv, *cp), blk(beta, *cp), g_c
            g_row = jnp.sum(g_c * idiag, axis=0, keepdims=True)
            dec = jnp.where(incl > 0.0, jnp.exp(jnp.where(incl > 0.0, g_c - g_row, 0.0)), 0.0)
            kq = lax.dot_general(jnp.concatenate([k_c, q_c], axis=0).astype(BF16), bd2(k_c),
                                 (((1,), (1,)), ((), ())), preferred_element_type=F32)
            aqk_[cp] = kq[C:2 * C] * dec
            p_[cp] = -(kq[0:C] * dec * strict * gb_[cp])
            t_[cp] = idiag + p_[cp]
        yield
        for cp in probs:
            p_[cp] = bdot(p_[cp], bd2(p_[cp]))
        yield
        for _ in range(int(math.log2(C)) - 2):
            for cp in probs:
                pt = bdot(jnp.concatenate([p_[cp], t_[cp]], axis=0), bd2(p_[cp]))
                p_[cp] = pt[0:C]
                t_[cp] = t_[cp] + pt[C:2 * C]
            yield
        for cp in probs:
            t_[cp] = t_[cp] + bdot(t_[cp], bd2(p_[cp]))
        yield
        uw_, eg_ = {}, {}
        for cp in probs:
            eg_[cp] = jnp.exp(gg_[cp])
            b_c = gb_[cp]
            uw_[cp] = bdot(t_[cp], jnp.concatenate([bd2(gv_[cp] * b_c), bd2(gk_[cp] * b_c * eg_[cp])],
                                                   axis=1))
        yield
        mm_, nn_, dl_, qp_, op_ = {}, {}, {}, {}, {}
        for cp in probs:
            k_c, g_c = gk_[cp], gg_[cp]
            g_last = g_c[C - 1:C, :]
            u_c, w_c = uw_[cp][:, 0:128], uw_[cp][:, 128:256]
            k_end_t = (k_c * jnp.exp(g_last - g_c)).T
            mn = bdot(k_end_t, jnp.concatenate([w_c, u_c], axis=1).astype(BF16))
            mm_[cp] = bm2 * mn[:, 0:128]
            nn_[cp] = bm2 * mn[:, 128:256]
            dl_[cp] = jnp.exp(g_last)
            qo = bdot(aqk_[cp], jnp.concatenate([bd2(w_c), bd2(u_c)], axis=1))
            qp_[cp] = gq_[cp] * eg_[cp] - qo[:, 0:128]
            op_[cp] = qo[:, 128:256]
        yield
        o_rows = []
        sgp = [s_gdn[slot, p] for p in range(n_pair)]
        for c in range(TL // C):
            o_pair = []
            for p in range(n_pair):
                cp = (c, p)
                s_b = sgp[p].astype(BF16)
                o_pair.append(bdot(qp_[cp], s_b) + op_[cp])
                sgp[p] = sgp[p] * dl_[cp] - bdot(mm_[cp], s_b) + nn_[cp]
            o_rows.append(jnp.concatenate(o_pair, axis=1))
            yield
        for p in range(n_pair):
            s_gdn[slot, p] = sgp[p]
        o_d = jnp.concatenate(o_rows, axis=0)
        out[slot, "d"] = o_d * lax.rsqrt(_dot(o_d * o_d, m64) + EPS) * prm["gdn_g"][...] * _silu(zd[:, 768:1024])

    def gla_stages(slot):
        zg = proj(slot, Z_GLA, Z_GLA + 768)
        sm = proj(slot, Z_SMALL, Z_WIDTH)
        q, k, v, r = zg[:, 0:128], zg[:, 128:256], zg[:, 256:512], zg[:, 512:768]
        gk = _dot(sm, prm["w_a2"][...]) + prm["b_a"][...]
        gk = (jnp.minimum(gk, 0.0) - jnp.log1p(jnp.exp(-jnp.abs(gk)))) / GLA_GATE_NORM
        yield
        bcum = _dot_exact_lhs(ltri, gk)
        caus4 = cst["caus4"][...]
        bmt = cst["bmt"][...]
        bmtb = bmt.astype(BF16)
        st = st_gla[slot]
        yield
        o_chunks = []
        for c in range(TL // C):
            sl = slice(c * C, (c + 1) * C)
            b_c = bcum[sl]
            b_last = b_c[C - 1:C, :]
            q_t = q[sl] * jnp.exp(b_c) * (GLA_DK ** -0.5)
            k_t = k[sl] * jnp.exp(-b_c)
            k_end = k[sl] * jnp.exp(b_last - b_c)
            v_c = v[sl]
            q_b = q_t.astype(BF16)
            att = lax.dot_general(q_b, stack4b(k_t) * bmtb, (((1,), (1,)), ((), ())),
                                  preferred_element_type=F32) * caus4
            cross = lax.dot_general(q_b, st.astype(BF16), (((1,), (1,)), ((), ())),
                                    preferred_element_type=F32)
            st = st * jnp.exp(b_last) + bmt * _dot(v_c.T, k_end)
            yield
            o_chunks.append(jnp.dot(att.astype(BF16), stack4b(v_c) * bmb, preferred_element_type=F32)
                            + cross)
            yield
        st_gla[slot] = st
        o_a = jnp.concatenate(o_chunks, axis=0)
        out[slot, "a"] =o_a * lax.rsqrt(_dot(o_a * o_a, m64) + EPS) * prm["gla_g"][...] * _silu(r)

    def ret_stages(slot):
        zr = proj(slot, Z_RET, Z_RET + 1024)
        cos2 = jnp.concatenate([cos_ref[slot], cos_ref[slot]], axis=1)
        sin2 = jnp.concatenate([sin_ref[slot], sin_ref[slot]], axis=1)
        first_half = (lax.broadcasted_iota(jnp.int32, (TL, 256), 1) % HEAD_DIM) < HEAD_DIM // 2

        def rope(x):
            rot = jnp.where(first_half, pltpu.roll(x, 256 - HEAD_DIM // 2, 1),
                            pltpu.roll(x, HEAD_DIM // 2, 1))
            return x * cos2 + rot * sin2

        rq = rope(zr[:, 0:256])
        rk = rope(zr[:, 256:512]) * (HEAD_DIM ** -0.5)
        rv = zr[:, 512:768]
        rg = zr[:, 768:1024]
        yield
        rq_b = rq.astype(BF16)
        qs = jnp.concatenate([rq_b * hm256b[hh:hh + 1] for hh in range(H)], axis=0)
        sc = lax.dot_general(qs, rk.astype(BF16), (((1,), (1,)), ((), ())),
                             preferred_element_type=F32) * cst["dmask"][...]
        s_prev = s_ret[slot]
        cross = jnp.dot(rq_b, s_prev.astype(BF16), preferred_element_type=F32) * cst["xi"][...]
        s_ret[slot] = s_prev * cst["gchunk"][...] + bm * _dot((rk * cst["zeta"][...]).T, rv)
        yield
        sc_wide = jnp.concatenate([sc[hh * TL:(hh + 1) * TL] for hh in range(H)], axis=1).astype(BF16)
        o_c = jnp.dot(sc_wide, stack4b(rv) * cst["bm4"][...].astype(BF16), preferred_element_type=F32) + cross
        yield
        o_c = o_c - _dot_exact_rhs(o_c, m64)
        yield
        out[slot, "c"] = o_c * lax.rsqrt(_dot(o_c * o_c, m64) + EPS) * prm["ret_g"][...] * _silu(rg)

    def s5_stages(slot):
        y = ys5_ref[slot]
        y = 0.5 * y * (1.0 + jnp.tanh(math.sqrt(2.0 / math.pi) * (y + 0.044715 * (y * y * y))))
        yield
        out[slot, "b"] = y * _sigmoid(_dot(y, prm["w_glu"][...]) + prm["b_glu"][...])

    def cast_stages():
        for src, dst in zip(cast_src, cast_dst):
            yield
            _cast_ride_along([src], [dst])

    running = [stages(slot) for slot in range(NS) for stages in (gdn_stages, ret_stages, gla_stages, s5_stages)]
    running.append(cast_stages())
    while running:
        for gen in list(running):
            if next(gen, "done") == "done":
                running.remove(gen)

    mix = jnp.concatenate([jnp.concatenate([out[s, "a"], out[s, "b"], out[s, "c"], out[s, "d"]], axis=1)
                           for s in range(NS)], axis=0).astype(BF16)
    res = jnp.dot(mix, prm["w_out"][...], preferred_element_type=F32)
    for s in range(NS):
        o_ref[s] = h_ref[s] + res[s * TL:(s + 1) * TL]


def _mixer_call(h, ys5, cos_t, sin_t, consts, params, layer, cast_weights):
    B, L, D = h.shape
    TL, NS = MIX_TL, MIX_SLOTS
    n_i = L // TL
    casts = [_cast_stream(w, layer, (B // NS) * n_i, lambda b, i: b * n_i + i) for w in cast_weights]

    def const(arr):
        nd = arr.ndim
        return pl.BlockSpec(arr.shape, lambda b, i: (0,) * nd, pipeline_mode=pl.Buffered(1))

    def tile(width):
        return pl.BlockSpec((NS, TL, width), lambda b, i: (b, i, 0))

    c_list = [consts[n] for n in _CONST_ORDER]
    p_list = [params[n] for n in _PARAM_ORDER]
    return pl.pallas_call(
        functools.partial(_mixer_kernel, n_cast=len(casts)),
        grid=(B // NS, n_i),
        in_specs=[tile(D), tile(GROUP_WIDTH), tile(128), tile(128)] + [const(a) for a in c_list]
        + [_layer_spec(a, layer, 2) for a in p_list] + [c[0] for c in casts],
        out_specs=[tile(D)] + [c[1] for c in casts],
        out_shape=[jax.ShapeDtypeStruct((B, L, D), F32)] + [c[2] for c in casts],
        scratch_shapes=[pltpu.VMEM((NS, 256, 128), F32), pltpu.VMEM((NS, 256, 256), F32),
                        pltpu.VMEM((NS, 2, 128, 128), F32), pltpu.VMEM((NS, TL + 8, 3 * GROUP_WIDTH), F32)],
        compiler_params=pltpu.CompilerParams(dimension_semantics=("arbitrary", "arbitrary"),
                                             vmem_limit_bytes=VMEM_LIMIT),
        name="mixer",
    )(h, ys5, cos_t, sin_t, *c_list, *p_list, *cast_weights)


def _ffn_kernel(h_ref, p_ref, gf_ref, wup_ref, wdn_ref, gp_ref, wg_ref, wp_ref, gl_ref, o_ref, act_ref, *,
                final):
    h = h_ref[...]
    hn = _rms(h, gf_ref[...]).astype(BF16)
    for s in range(FFN_HIDDEN // FFN_SLAB):
        lo = s * FFN_SLAB
        g = jnp.dot(hn, wup_ref[:, lo:lo + FFN_SLAB], preferred_element_type=F32)
        u = jnp.dot(hn, wup_ref[:, FFN_HIDDEN + lo:FFN_HIDDEN + lo + FFN_SLAB], preferred_element_type=F32)
        act_ref[:, lo:lo + FFN_SLAB] = (_silu(g) * u).astype(BF16)
    h2 = h + jnp.dot(act_ref[...], wdn_ref[...], preferred_element_type=F32)
    gate = _sigmoid(jnp.dot(_rms(h2, gp_ref[...]).astype(BF16), wg_ref[...], preferred_element_type=F32))
    h3 = h2 + jnp.dot(p_ref[...].astype(BF16), wp_ref[...], preferred_element_type=F32) * gate
    if final:
        h3 = _rms(h3, gl_ref[...])
    o_ref[...] = h3


def _ffn_call(h2d, p3d, layer, weights, final):
    T, D = h2d.shape
    TM = FFN_TM
    return pl.pallas_call(
        functools.partial(_ffn_kernel, final=final),
        grid=(T // TM,),
        in_specs=[pl.BlockSpec((TM, D), lambda i: (i, 0)),
                  pl.BlockSpec((None, TM, PLE_DIM), lambda i: (layer, i, 0))]
        + [_layer_spec(w, layer, 1) for w in weights],
        out_specs=pl.BlockSpec((TM, D), lambda i: (i, 0)),
        out_shape=jax.ShapeDtypeStruct((T, D), F32),
        scratch_shapes=[pltpu.VMEM((TM, FFN_HIDDEN), BF16)],
        compiler_params=pltpu.CompilerParams(dimension_semantics=("arbitrary",), vmem_limit_bytes=VMEM_LIMIT),
        name="ffn_ple",
    )(h2d, p3d, *weights)


def _row(v):
    return v[:, None, :]


_WZ_SEGMENTS = (_AQ, _AK, _AV, _AR, _RQ, _RK, _RV, _RG, _DQ, _DK, _DV, _DG)


def _regroup_kernel(wt_ref, wz_ref, wsu_ref):
    col = 0
    for lo, hi in _WZ_SEGMENTS:
        wz_ref[:, col:col + hi - lo] = wt_ref[lo:hi, :].T.astype(BF16)
        col += hi - lo
    assert col == Z_SMALL and _DA[0] == _DB[1] and (_DA[1] - _DB[0]) % 8 == 0
    small = jnp.concatenate([wt_ref[_ALOW[0]:_ALOW[1], :], wt_ref[_DB[0]:_DA[1], :],
                             jnp.zeros((128 - GLA_RANK - 2 * NUM_HEADS, D_MODEL), F32)], axis=0)
    wz_ref[:, Z_SMALL:Z_WIDTH] = small.T.astype(BF16)
    wsu_ref[...] = wt_ref[_SU[0]:_SU[1], :].T.astype(BF16)


def _regroup_w_in(w_in):
    depth, K, N = w_in.shape
    return pl.pallas_call(
        _regroup_kernel,
        grid=(depth,),
        in_specs=[pl.BlockSpec((None, N, K), lambda d: (d, 0, 0))],
        out_specs=[pl.BlockSpec((None, K, Z_WIDTH), lambda d: (d, 0, 0)),
                   pl.BlockSpec((None, K, GROUP_WIDTH), lambda d: (d, 0, 0))],
        out_shape=[jax.ShapeDtypeStruct((depth, K, Z_WIDTH), BF16),
                   jax.ShapeDtypeStruct((depth, K, GROUP_WIDTH), BF16)],
        compiler_params=pltpu.CompilerParams(vmem_limit_bytes=VMEM_LIMIT),
        name="regroup_w_in",
    )(jnp.swapaxes(w_in, 1, 2))


def _mixer_params(norm_mix, w_z, gla_w_a2, gla_b_a, gla_norm, ret_norm, gdn_conv, gdn_a_log,
                  gdn_dt_bias, gdn_norm, s5_b_glu):
    w_a2 = jnp.pad(gla_w_a2, ((0, 0), (0, 128 - GLA_RANK), (0, 0)))
    lane_pad = ((0, 0), (SM_A, 128 - SM_A - NUM_HEADS))
    return dict(norm_mix=_row(norm_mix), w_z=w_z, w_a2=w_a2, b_a=_row(gla_b_a),
                gla_g=_row(jnp.tile(gla_norm, (1, NUM_HEADS))), ret_g=_row(ret_norm),
                gdn_g=_row(jnp.tile(gdn_norm, (1, NUM_HEADS))), conv_w=gdn_conv,
                alog_v=_row(jnp.pad(gdn_a_log, lane_pad)), dtb_v=_row(jnp.pad(gdn_dt_bias, lane_pad)),
                b_glu=_row(s5_b_glu))


def kernel(x, p, positions, norm_mix, w_in, w_out, gla_w_a2, gla_b_a, gla_norm, s5_lam_re, s5_lam_im, s5_log_dt,
           s5_b_re, s5_b_im, s5_c_re, s5_c_im, s5_d, s5_w_glu, s5_b_glu, ret_norm, gdn_conv, gdn_a_log,
           gdn_dt_bias, gdn_norm, norm_ffn, w_ffn_up, w_ffn_down, norm_ple, w_ple_gate, w_ple_proj, norm_final):
    B, L, D = x.shape
    depth = w_in.shape[0]
    assert D == D_MODEL and L % max(MIX_TL, S5_SUB * S5_ROWS) == 0 and (B * L) % FFN_TM == 0
    assert B % MIX_SLOTS == 0
    consts = {k: jnp.asarray(v) for k, v in _mixer_consts().items()}
    cos_t, sin_t = _rope_tables(positions)
    prep = _s5_prepare(s5_lam_re, s5_lam_im, s5_log_dt, s5_b_re, s5_b_im, s5_c_re, s5_c_im, s5_d)
    w_z, w_su = _regroup_w_in(w_in)
    params = _mixer_params(norm_mix, w_z, gla_w_a2, gla_b_a, gla_norm, ret_norm, gdn_conv,
                           gdn_a_log, gdn_dt_bias, gdn_norm, s5_b_glu)
    g_ffn, g_ple = _row(norm_ffn), _row(norm_ple)
    g_final = jnp.broadcast_to(norm_final[None, None, :], (depth, 1, D))
    p3d = p.reshape(depth, B * L, PLE_DIM)
    h = x
    for i in range(depth):
        ys5, w_out_b, w_glu_b = _s5_call(h, i, _row(norm_mix), w_su, prep, [w_out, s5_w_glu])
        h, w_up_b, w_down_b, w_gate_b, w_proj_b = _mixer_call(
            h, ys5, cos_t, sin_t, consts, dict(params, w_out=w_out_b, w_glu=w_glu_b), i,
            [w_ffn_up, w_ffn_down, w_ple_gate, w_ple_proj])
        ffn_w = [g_ffn, w_up_b, w_down_b, g_ple, w_gate_b, w_proj_b, g_final]
        h = _ffn_call(h.reshape(B * L, D), p3d, i, ffn_w, final=(i == depth - 1)).reshape(B, L, D)
    return h
```
